```python
import jax, jax.numpy as jnp
from jax import lax
import numpy as np

D_MODEL = 1024
BATCH = 4
SEQ = 4096
DEPTH = 1

HEAD_DIM = 64
ATTN_HEADS = 8
ATTN_KV_HEADS = 2
ATTN_GROUP = ATTN_HEADS // ATTN_KV_HEADS
RET_HEADS = 8
ATTN_WIDTH = ATTN_HEADS * HEAD_DIM
KV_WIDTH = ATTN_KV_HEADS * HEAD_DIM
RET_WIDTH = RET_HEADS * HEAD_DIM
MIX_WIDTH = ATTN_WIDTH + RET_WIDTH
IN_PROJ = ATTN_WIDTH + 2 * KV_WIDTH + 4 * RET_WIDTH
WINDOW = 128
BLOCK = 128
RET_CHUNK = 128
ROPE_THETA = 10000.0
D_FF = -(-8 * D_MODEL // (3 * 256)) * 256
EPS = 1e-6
NEG_INF = -1e30

kernel_name = "hymba_swa_retention_encoder_block"


def rms_norm(x, w):
    xf = x.astype(jnp.float32)
    y = xf * lax.rsqrt(jnp.mean(xf * xf, axis=-1, keepdims=True) + EPS)
    return (y * w.astype(jnp.float32)).astype(x.dtype)


def rope(x, pos):
    d = x.shape[-1]
    inv_freq = ROPE_THETA ** (-jnp.arange(0, d, 2, dtype=jnp.float32) / d)
    ang = pos[:, None] * inv_freq[None, :]
    cos = jnp.concatenate([jnp.cos(ang), jnp.cos(ang)], -1)[None, :, None, :]
    sin = jnp.concatenate([jnp.sin(ang), jnp.sin(ang)], -1)[None, :, None, :]
    xf = x.astype(jnp.float32)
    x1, x2 = xf[..., : d // 2], xf[..., d // 2:]
    rot = jnp.concatenate([-x2, x1], axis=-1)
    return (xf * cos + rot * sin).astype(x.dtype)


def windowed_gqa_sink(q, k, v, sink):
    B, S, H, D = q.shape
    nb = S // BLOCK
    pad = ((0, 0), (BLOCK, BLOCK), (0, 0), (0, 0))
    kp = jnp.pad(k, pad).reshape(B, nb + 2, BLOCK, ATTN_KV_HEADS, D)
    vp = jnp.pad(v, pad).reshape(B, nb + 2, BLOCK, ATTN_KV_HEADS, D)
    kb = jnp.concatenate([kp[:, :-2], kp[:, 1:-1], kp[:, 2:]], axis=2)
    vb = jnp.concatenate([vp[:, :-2], vp[:, 1:-1], vp[:, 2:]], axis=2)
    qb = q.reshape(B, nb, BLOCK, ATTN_KV_HEADS, ATTN_GROUP, D)
    scale = D ** -0.5
    s = jnp.einsum('bnqkgd,bnjkd->bnkgqj', qb, kb).astype(jnp.float32) * scale
    q_pos = jnp.arange(nb)[:, None] * BLOCK + jnp.arange(BLOCK)[None, :]
    k_pos = jnp.arange(nb)[:, None] * BLOCK - BLOCK + jnp.arange(3 * BLOCK)[None, :]
    valid = (jnp.abs(q_pos[:, :, None] - k_pos[:, None, :]) <= WINDOW) \
        & (k_pos >= 0)[:, None, :] & (k_pos < S)[:, None, :]
    s = jnp.where(valid[None, :, None, None], s, NEG_INF)
    sink_l = sink.astype(jnp.float32).reshape(ATTN_KV_HEADS, ATTN_GROUP)[None, None, :, :, None, None]
    m = jnp.maximum(jnp.max(s, axis=-1, keepdims=True), sink_l)
    p = jnp.exp(s - m)
    p = p / (jnp.sum(p, axis=-1, keepdims=True) + jnp.exp(sink_l - m))
    o = jnp.einsum('bnkgqj,bnjkd->bnqkgd', p.astype(v.dtype), vb)
    return o.reshape(B, S, H * D)


def retention_direction(q, k, v, log_gamma, strict):
    B, H, S, D = q.shape
    C = RET_CHUNK
    nc = S // C
    qc = q.reshape(B, H, nc, C, D)
    kc = k.reshape(B, H, nc, C, D)
    vc = v.reshape(B, H, nc, C, D)
    idx = jnp.arange(C, dtype=jnp.float32)
    diff = idx[:, None] - idx[None, :]
    mask = diff > 0 if strict else diff >= 0
    lg = log_gamma[:, None, None]
    dmat = jnp.where(mask[None], jnp.exp(lg * jnp.maximum(diff, 0.0)[None]), 0.0)
    xi = jnp.exp(log_gamma[:, None] * (idx + 1.0)[None])
    zeta = jnp.exp(log_gamma[:, None] * (C - 1.0 - idx)[None])
    g_chunk = jnp.exp(log_gamma * C)[None, :, None, None]
    inner = jnp.einsum('bhcnd,bhcmd->bhcnm', qc, kc) * dmat[None, :, None]
    out_inner = jnp.einsum('bhcnm,bhcme->bhcne', inner, vc)
    kv = jnp.einsum('bhcmd,bhcme->bhcde', kc * zeta[None, :, None, :, None], vc)

    def step(r, kv_c):
        return g_chunk * r + kv_c, r

    _, states = lax.scan(step, jnp.zeros((B, H, D, D), jnp.float32), jnp.moveaxis(kv, 2, 0))
    states = jnp.moveaxis(states, 0, 2)
    cross = jnp.einsum('bhcnd,bhcde->bhcne', qc * xi[None, :, None, :, None], states)
    return (out_inner + cross).reshape(B, H, S, D)


def bidirectional_retention(q, k, v, log_fwd, log_bwd):
    fwd = retention_direction(q, k, v, log_fwd, False)
    bwd = retention_direction(jnp.flip(q, 2), jnp.flip(k, 2), jnp.flip(v, 2), log_bwd, True)
    return fwd + jnp.flip(bwd, 2)


def setup_inputs(seed: int = 0) -> dict:
    key = jax.random.key(seed)
    ks = jax.random.split(key, 16)
    f32 = jnp.float32
    nrm = lambda k, shape, fan_in: jax.random.normal(k, shape, f32) * fan_in ** -0.5
    base_decay = np.log(1.0 - 2.0 ** (-5.0 - np.arange(RET_HEADS))).astype(np.float32)
    base_decay = jnp.asarray(base_decay)[None, :]
    return {
        "x": jax.random.normal(ks[0], (BATCH, SEQ, D_MODEL), f32),
        "attn_norm_w": 1.0 + 0.02 * jax.random.normal(ks[1], (DEPTH, D_MODEL), f32),
        "w_in": nrm(ks[2], (DEPTH, D_MODEL, IN_PROJ), D_MODEL),
        "q_norm_w": 1.0 + 0.02 * jax.random.normal(ks[3], (DEPTH, HEAD_DIM), f32),
        "k_norm_w": 1.0 + 0.02 * jax.random.normal(ks[4], (DEPTH, HEAD_DIM), f32),
        "attn_sink": 0.5 * jax.random.normal(ks[5], (DEPTH, ATTN_HEADS), f32),
        "ret_log_decay_fwd": base_decay * (1.0 + 0.05 * jax.random.normal(ks[6], (DEPTH, RET_HEADS), f32)),
        "ret_log_decay_bwd": base_decay * (1.0 + 0.05 * jax.random.normal(ks[7], (DEPTH, RET_HEADS), f32)),
        "ret_norm_w": 1.0 + 0.02 * jax.random.normal(ks[8], (DEPTH, RET_WIDTH), f32),
        "w_out": nrm(ks[9], (DEPTH, MIX_WIDTH, D_MODEL), MIX_WIDTH),
        "ffn_norm_w": 1.0 + 0.02 * jax.random.normal(ks[10], (DEPTH, D_MODEL), f32),
        "w_gate": nrm(ks[11], (DEPTH, D_MODEL, D_FF), D_MODEL),
        "w_up": nrm(ks[12], (DEPTH, D_MODEL, D_FF), D_MODEL),
        "w_down": nrm(ks[13], (DEPTH, D_FF, D_MODEL), D_FF),
    }


def reference(x, attn_norm_w, w_in, q_norm_w, k_norm_w, attn_sink, ret_log_decay_fwd,
              ret_log_decay_bwd, ret_norm_w, w_out, ffn_norm_w, w_gate, w_up, w_down):
    B, S, _ = x.shape
    pos = jnp.arange(S, dtype=jnp.float32)
    split_at = np.cumsum([ATTN_WIDTH, KV_WIDTH, KV_WIDTH, RET_WIDTH, RET_WIDTH, RET_WIDTH]).tolist()
    h = x
    for l in range(DEPTH):
        n = rms_norm(h, attn_norm_w[l])
        proj = n @ w_in[l]
        aq, ak, av, rq, rk, rv, rg = jnp.split(proj, split_at, axis=-1)
        aq = rope(rms_norm(aq.reshape(B, S, ATTN_HEADS, HEAD_DIM), q_norm_w[l]), pos)
        ak = rope(rms_norm(ak.reshape(B, S, ATTN_KV_HEADS, HEAD_DIM), k_norm_w[l]), pos)
        av = av.reshape(B, S, ATTN_KV_HEADS, HEAD_DIM)
        y_attn = windowed_gqa_sink(aq, ak, av, attn_sink[l])
        rq = rope(rq.reshape(B, S, RET_HEADS, HEAD_DIM), pos)
        rk = rope(rk.reshape(B, S, RET_HEADS, HEAD_DIM), pos) * (HEAD_DIM ** -0.5)
        rv = rv.reshape(B, S, RET_HEADS, HEAD_DIM)
        to_bhsd = lambda t: jnp.transpose(t, (0, 2, 1, 3)).astype(jnp.float32)
        ret = bidirectional_retention(to_bhsd(rq), to_bhsd(rk), to_bhsd(rv),
                                      -jnp.abs(ret_log_decay_fwd[l].astype(jnp.float32)),
                                      -jnp.abs(ret_log_decay_bwd[l].astype(jnp.float32)))
        ret = jnp.transpose(ret, (0, 2, 1, 3)).astype(h.dtype)
        ret = rms_norm(ret, ret_norm_w[l].reshape(RET_HEADS, HEAD_DIM)).reshape(B, S, RET_WIDTH)
        y_ret = jax.nn.silu(rg) * ret
        h = h + jnp.concatenate([y_attn, y_ret], axis=-1) @ w_out[l]
        m = rms_norm(h, ffn_norm_w[l])
        h = h + (jax.nn.silu(m @ w_gate[l]) * (m @ w_up[l])) @ w_down[l]
    return h
```

```python
import jax
import jax.numpy as jnp
import numpy as np
from jax import lax
from jax.experimental import pallas as pl
from jax.experimental.pallas import tpu as pltpu

HEAD_DIM = 64
ATTN_HEADS = 8
ATTN_KV_HEADS = 2
RET_HEADS = 8
CHUNK = 128
ROPE_THETA = 10000.0
EPS = 1e-6
NEG_INF = -1e30

LANES = 128
PAIRS = RET_HEADS // 2
ATTN_W = ATTN_HEADS * HEAD_DIM
RET_W = RET_HEADS * HEAD_DIM
KV2_W = 2 * ATTN_KV_HEADS * HEAD_DIM

C_AQ = 0
C_K2 = C_AQ + ATTN_W
C_V2 = C_K2 + KV2_W
C_RQ = C_V2 + KV2_W
C_RK = C_RQ + RET_W
C_RV = C_RK + RET_W
C_RG = C_RV + RET_W
PROJ_W = C_RG + RET_W

PROJ_TM = 512
FFN_TM = 512
FFN_CK = 512
VMEM_LIMIT = 56 * 1024 * 1024

BF16 = jnp.bfloat16
F32 = jnp.float32


def _dot(a, b):
    return jnp.dot(a, b, preferred_element_type=F32)


def _dot_nt(a, b):
    return lax.dot_general(a, b, (((1,), (1,)), ((), ())), preferred_element_type=F32)


def _dot_tn(a, b):
    return lax.dot_general(a, b, (((0,), (0,)), ((), ())), preferred_element_type=F32)


def _resident(shape):
    zeros = (0,) * len(shape)
    return pl.BlockSpec(shape, lambda *_: zeros, pipeline_mode=pl.Buffered(1))


def _proj_kernel(x_ref, anw_ref, w_ref, qnw_ref, knw_ref, cos_ref, sina_ref, sinb_ref,
                 bd_ref, o_ref):
    x = x_ref[...]
    ms = jnp.mean(x * x, axis=-1, keepdims=True)
    n = (x * lax.rsqrt(ms + EPS) * anw_ref[...]).astype(BF16)
    cos = cos_ref[...]
    sina = sina_ref[...]
    sinb = sinb_ref[...]

    def proj(c0, width):
        return _dot(n, w_ref[:, c0:c0 + width])

    def head_norm(y, w):
        width = y.shape[-1]
        ss = _dot((y * y).astype(BF16), bd_ref[:width, :width])
        return y * lax.rsqrt(ss * (1.0 / HEAD_DIM) + EPS) * w

    def rope_store(y, c0, scale):
        for s in range(y.shape[-1] // LANES):
            ys = y[:, s * LANES:(s + 1) * LANES]
            r = ys * cos + pltpu.roll(ys, LANES - 32, 1) * sina + pltpu.roll(ys, 32, 1) * sinb
            if scale != 1.0:
                r = r * scale
            o_ref[:, c0 + s * LANES:c0 + (s + 1) * LANES] = r.astype(BF16)

    scale = HEAD_DIM ** -0.5
    rope_store(head_norm(proj(C_AQ, ATTN_W), qnw_ref[...]), C_AQ, scale)
    rope_store(head_norm(proj(C_K2, KV2_W), knw_ref[...]), C_K2, 1.0)
    o_ref[:, C_V2:C_V2 + KV2_W] = proj(C_V2, KV2_W).astype(BF16)
    rope_store(proj(C_RQ, RET_W), C_RQ, 1.0)
    rope_store(proj(C_RK, RET_W), C_RK, scale)
    o_ref[:, C_RV:C_RV + RET_W] = proj(C_RV, RET_W).astype(BF16)
    o_ref[:, C_RG:C_RG + RET_W] = proj(C_RG, RET_W).astype(BF16)


def _proj_call(x2, anw, w_in2, qnw, knw, cos, sina, sinb, bd, seq):
    tokens, d_model = x2.shape
    tm = PROJ_TM
    pos_blocks = seq // tm
    row = lambda i: (i, 0)
    pos = lambda i: (i % pos_blocks, 0)
    return pl.pallas_call(
        _proj_kernel,
        grid=(tokens // tm,),
        in_specs=[
            pl.BlockSpec((tm, d_model), row),
            _resident((1, d_model)),
            _resident((d_model, PROJ_W)),
            _resident((1, ATTN_W)),
            _resident((1, KV2_W)),
            pl.BlockSpec((tm, LANES), pos),
            pl.BlockSpec((tm, LANES), pos),
            pl.BlockSpec((tm, LANES), pos),
            _resident((ATTN_W, ATTN_W)),
        ],
        out_specs=pl.BlockSpec((tm, PROJ_W), row),
        out_shape=jax.ShapeDtypeStruct((tokens, PROJ_W), BF16),
        compiler_params=pltpu.CompilerParams(
            dimension_semantics=("arbitrary",), vmem_limit_bytes=VMEM_LIMIT),
        name="proj",
    )(x2, anw, w_in2, qnw, knw, cos, sina, sinb, bd)


def _state_kernel(kf_ref, vf_ref, kb_ref, vb_ref, zf_ref, zb_ref, gf_ref, gb_ref, bdm_ref,
                  rf_ref, rb_ref, cf_ref, cb_ref):
    @pl.when(pl.program_id(0) == 0)
    def _():
        cf_ref[...] = jnp.zeros_like(cf_ref)
        cb_ref[...] = jnp.zeros_like(cb_ref)

    bdm = bdm_ref[...]
    batch = kf_ref.shape[0]
    for k_ref, v_ref, z_ref, g_ref, r_ref, c_ref in (
            (kf_ref, vf_ref, zf_ref, gf_ref, rf_ref, cf_ref),
            (kb_ref, vb_ref, zb_ref, gb_ref, rb_ref, cb_ref)):
        for b in range(batch):
            for p in range(PAIRS):
                sl = slice(p * LANES, (p + 1) * LANES)
                kz = (k_ref[b, :, sl].astype(F32) * z_ref[:, sl]).astype(BF16)
                kv = _dot_tn(kz, v_ref[b, :, sl])
                carry = c_ref[b, p]
                r_ref[b, p] = carry.astype(BF16)
                c_ref[b, p] = g_ref[p] * carry + kv * bdm


def _state_call(proj3, zf, zb, gf, gb, bdm):
    batch, seq, _ = proj3.shape
    nc = seq // CHUNK
    kcol, vcol = C_RK // RET_W, C_RV // RET_W
    blk = (batch, CHUNK, RET_W)
    out_blk = (batch, None, PAIRS, LANES, LANES)
    out_sds = jax.ShapeDtypeStruct((batch, nc, PAIRS, LANES, LANES), BF16)
    return pl.pallas_call(
        _state_kernel,
        grid=(nc,),
        in_specs=[
            pl.BlockSpec(blk, lambda s: (0, s, kcol)),
            pl.BlockSpec(blk, lambda s: (0, s, vcol)),
            pl.BlockSpec(blk, lambda s: (0, nc - 1 - s, kcol)),
            pl.BlockSpec(blk, lambda s: (0, nc - 1 - s, vcol)),
            _resident((CHUNK, RET_W)),
            _resident((CHUNK, RET_W)),
            _resident((PAIRS, LANES, LANES)),
            _resident((PAIRS, LANES, LANES)),
            _resident((LANES, LANES)),
        ],
        out_specs=[
            pl.BlockSpec(out_blk, lambda s: (0, s, 0, 0, 0)),
            pl.BlockSpec(out_blk, lambda s: (0, nc - 1 - s, 0, 0, 0)),
        ],
        out_shape=[out_sds, out_sds],
        scratch_shapes=[pltpu.VMEM((batch, PAIRS, LANES, LANES), F32),
                        pltpu.VMEM((batch, PAIRS, LANES, LANES), F32)],
        compiler_params=pltpu.CompilerParams(
            dimension_semantics=("arbitrary",), vmem_limit_bytes=VMEM_LIMIT),
        name="ret_state",
    )(proj3, proj3, proj3, proj3, zf, zb, gf, gb, bdm)


def _mix_kernel(sink_ref, aq_ref, kp_ref, kc_ref, kn_ref, vp_ref, vc_ref, vn_ref,
                rq_ref, rk_ref, rv_ref, rg_ref, rf_ref, rb_ref, x_ref, wo_ref,
                bias_ref, dm_ref, xif_ref, xib_ref, rnw_ref, bd_ref, o_ref):
    lane = lax.broadcasted_iota(jnp.int32, (CHUNK, LANES), 1).astype(F32).astype(BF16)
    lo = lane < HEAD_DIM
    hi = jnp.logical_not(lo)
    zero = jnp.zeros((), BF16)

    def keep(mask, a):
        reps = a.shape[0] // CHUNK
        m = mask if reps == 1 else jnp.concatenate([mask] * reps, axis=0)
        return jnp.where(m, a, zero)

    q = aq_ref[...]
    k3 = jnp.concatenate([kp_ref[...], kc_ref[...], kn_ref[...]], axis=0)
    v3 = jnp.concatenate([vp_ref[...], vc_ref[...], vn_ref[...]], axis=0)
    bias = bias_ref[...]
    group = ATTN_HEADS // ATTN_KV_HEADS
    y_pairs = []
    for g in range(ATTN_KV_HEADS):
        g_low = slice(g * LANES, (g + 1) * LANES)
        g_high = slice((1 - g) * LANES, (2 - g) * LANES)
        kg = jnp.where(jnp.concatenate([lo] * 3, axis=0), k3[:, g_low], k3[:, g_high])
        v_lo = keep(lo, v3[:, g_low])
        v_hi = keep(hi, v3[:, g_high])
        qs = []
        for j in range(group):
            h = g * group + j
            qp = q[:, (h // 2) * LANES:(h // 2 + 1) * LANES]
            qs.append(keep(lo if h % 2 == 0 else hi, qp))
        s_all = _dot_nt(jnp.concatenate(qs, axis=0), kg)
        ps, rs = [], []
        for j in range(group):
            h = g * group + j
            s = s_all[j * CHUNK:(j + 1) * CHUNK] + bias
            sink = sink_ref[h]
            m = jnp.maximum(jnp.max(s, axis=-1, keepdims=True), sink)
            e = jnp.exp(s - m)
            den = jnp.sum(e, axis=-1, keepdims=True) + jnp.exp(sink - m)
            ps.append(e.astype(BF16))
            rs.append(1.0 / den)
        o_lo = _dot(jnp.concatenate([ps[0], ps[2]], axis=0), v_lo)
        o_hi = _dot(jnp.concatenate([ps[1], ps[3]], axis=0), v_hi)
        for pp in range(group // 2):
            rows = slice(pp * CHUNK, (pp + 1) * CHUNK)
            y_pairs.append(o_lo[rows] * rs[2 * pp] + o_hi[rows] * rs[2 * pp + 1])
    y_attn = jnp.concatenate(y_pairs, axis=1)

    ret_pairs = []
    for p in range(PAIRS):
        sl = slice(p * LANES, (p + 1) * LANES)
        qp, kp, vp = rq_ref[:, sl], rk_ref[:, sl], rv_ref[:, sl]
        s2 = _dot_nt(jnp.concatenate([keep(lo, qp), keep(hi, qp)], axis=0), kp)
        p0 = (s2[:CHUNK] * dm_ref[2 * p]).astype(BF16)
        p1 = (s2[CHUNK:] * dm_ref[2 * p + 1]).astype(BF16)
        qf = (qp.astype(F32) * xif_ref[:, sl]).astype(BF16)
        qb = (qp.astype(F32) * xib_ref[:, sl]).astype(BF16)
        lhs = jnp.concatenate([p0, p1, qf, qb], axis=1)
        rhs = jnp.concatenate([keep(lo, vp), keep(hi, vp), rf_ref[p], rb_ref[p]], axis=0)
        ret_pairs.append(_dot(lhs, rhs))
    ret = jnp.concatenate(ret_pairs, axis=1)
    ss = _dot((ret * ret).astype(BF16), bd_ref[...])
    ret_n = ret * lax.rsqrt(ss * (1.0 / HEAD_DIM) + EPS) * rnw_ref[...]
    gate = rg_ref[...].astype(F32)
    y_ret = gate * (1.0 / (1.0 + jnp.exp(-gate))) * ret_n

    y = jnp.concatenate([y_attn, y_ret], axis=1).astype(BF16)
    o_ref[...] = x_ref[...] + _dot(y, wo_ref[...])


def _mix_call(sink, proj3, rf, rb, x3, w_out, bias, dmat, xif, xib, rnw, bd):
    batch, seq, d_model = x3.shape
    nc = seq // CHUNK
    wide = lambda col: pl.BlockSpec((None, CHUNK, RET_W), lambda b, c: (b, c, col))
    kv = lambda col, off: pl.BlockSpec(
        (None, CHUNK, KV2_W),
        lambda b, c: (b, jnp.clip(c + off, 0, nc - 1), col))
    st = pl.BlockSpec((None, None, PAIRS, LANES, LANES), lambda b, c: (b, c, 0, 0, 0))
    edge = lambda b, c: (jnp.where(c == 0, 0, jnp.where(c == nc - 1, 2, 1)), 0, 0)
    k2c, v2c = C_K2 // KV2_W, C_V2 // KV2_W
    return pl.pallas_call(
        _mix_kernel,
        grid=(batch, nc),
        in_specs=[
            pl.BlockSpec(memory_space=pltpu.SMEM),
            wide(C_AQ // ATTN_W),
            kv(k2c, -1), kv(k2c, 0), kv(k2c, 1),
            kv(v2c, -1), kv(v2c, 0), kv(v2c, 1),
            wide(C_RQ // RET_W), wide(C_RK // RET_W), wide(C_RV // RET_W), wide(C_RG // RET_W),
            st, st,
            pl.BlockSpec((None, CHUNK, d_model), lambda b, c: (b, c, 0)),
            _resident((d_model, d_model)),
            pl.BlockSpec((None, CHUNK, 3 * CHUNK), edge),
            _resident((RET_HEADS, CHUNK, CHUNK)),
            _resident((CHUNK, RET_W)),
            _resident((CHUNK, RET_W)),
            _resident((1, RET_W)),
            _resident((RET_W, RET_W)),
        ],
        out_specs=pl.BlockSpec((None, CHUNK, d_model), lambda b, c: (b, c, 0)),
        out_shape=jax.ShapeDtypeStruct((batch, seq, d_model), F32),
        compiler_params=pltpu.CompilerParams(
            dimension_semantics=("arbitrary", "arbitrary"), vmem_limit_bytes=VMEM_LIMIT),
        name="mix",
    )(sink, proj3, proj3, proj3, proj3, proj3, proj3, proj3, proj3, proj3, proj3, proj3,
      rf, rb, x3, w_out, bias, dmat, xif, xib, rnw, bd)


def _ffn_kernel(h_ref, fnw_ref, wg_ref, wu_ref, wd_ref, o_ref, a_ref):
    h = h_ref[...]
    ms = jnp.mean(h * h, axis=-1, keepdims=True)
    m = (h * lax.rsqrt(ms + EPS) * fnw_ref[...]).astype(BF16)
    d_ff = wg_ref.shape[1]
    for c0 in range(0, d_ff, FFN_CK):
        c1 = min(c0 + FFN_CK, d_ff)
        g = _dot(m, wg_ref[:, c0:c1])
        u = _dot(m, wu_ref[:, c0:c1])
        a_ref[:, c0:c1] = (g * (1.0 / (1.0 + jnp.exp(-g))) * u).astype(BF16)
    o_ref[...] = h + _dot(a_ref[...], wd_ref[...])


def _ffn_call(h2, fnw, wg, wu, wd):
    tokens, d_model = h2.shape
    d_ff = wg.shape[1]
    tm = FFN_TM
    row = lambda i: (i, 0)
    return pl.pallas_call(
        _ffn_kernel,
        grid=(tokens // tm,),
        in_specs=[
            pl.BlockSpec((tm, d_model), row),
            _resident((1, d_model)),
            _resident((d_model, d_ff)),
            _resident((d_model, d_ff)),
            _resident((d_ff, d_model)),
        ],
        out_specs=pl.BlockSpec((tm, d_model), row),
        out_shape=jax.ShapeDtypeStruct((tokens, d_model), F32),
        scratch_shapes=[pltpu.VMEM((tm, d_ff), BF16)],
        compiler_params=pltpu.CompilerParams(
            dimension_semantics=("arbitrary",), vmem_limit_bytes=VMEM_LIMIT),
        name="ffn",
    )(h2, fnw, wg, wu, wd)


def _rope_tables(seq):
    inv_freq = ROPE_THETA ** (-jnp.arange(0, HEAD_DIM, 2, dtype=F32) / HEAD_DIM)
    ang = jnp.arange(seq, dtype=F32)[:, None] * inv_freq[None, :]
    cos, sin = jnp.cos(ang), jnp.sin(ang)
    zeros = jnp.zeros_like(sin)
    reps = LANES // HEAD_DIM
    cos_t = jnp.tile(jnp.concatenate([cos, cos], -1), (1, reps))
    sina_t = jnp.tile(jnp.concatenate([-sin, zeros], -1), (1, reps))
    sinb_t = jnp.tile(jnp.concatenate([zeros, sin], -1), (1, reps))
    return cos_t, sina_t, sinb_t


def _attn_bias():
    i = np.arange(CHUNK)[:, None]
    j = np.arange(3 * CHUNK)[None, :]
    band = np.abs(i + CHUNK - j) <= CHUNK
    first = band & (j >= CHUNK)
    last = band & (j < 2 * CHUNK)
    tab = np.stack([first, band, last]).astype(np.float32)
    return jnp.asarray(np.where(tab > 0, 0.0, NEG_INF).astype(np.float32))


def _retention_tables(log_f, log_b):
    idx = jnp.arange(CHUNK, dtype=F32)
    diff = idx[:, None] - idx[None, :]
    lf, lb = log_f[:, None, None], log_b[:, None, None]
    dmat = jnp.where(diff[None] >= 0,
                     jnp.exp(lf * jnp.maximum(diff, 0.0)[None]),
                     jnp.exp(lb * jnp.maximum(-diff, 0.0)[None]))
    per_lane = lambda t: jnp.repeat(t.T, HEAD_DIM, axis=1)
    xif = per_lane(jnp.exp(log_f[:, None] * (idx + 1.0)[None]))
    xib = per_lane(jnp.exp(log_b[:, None] * (CHUNK - idx)[None]))
    zf = per_lane(jnp.exp(log_f[:, None] * (CHUNK - 1.0 - idx)[None]))
    zb = per_lane(jnp.exp(log_b[:, None] * idx[None]))
    per_row = lambda g: jnp.broadcast_to(
        jnp.repeat(g.reshape(PAIRS, 2), HEAD_DIM, axis=1)[:, :, None], (PAIRS, LANES, LANES))
    gf = per_row(jnp.exp(log_f * CHUNK))
    gb = per_row(jnp.exp(log_b * CHUNK))
    return dmat, xif, xib, zf, zb, gf, gb


def _block_diag_ones(width, dtype):
    r = np.arange(width) // HEAD_DIM
    return jnp.asarray((r[:, None] == r[None, :]).astype(np.float32), dtype=dtype)


def kernel(x, attn_norm_w, w_in, q_norm_w, k_norm_w, attn_sink, ret_log_decay_fwd,
           ret_log_decay_bwd, ret_norm_w, w_out, ffn_norm_w, w_gate, w_up, w_down):
    batch, seq, d_model = x.shape
    depth = w_in.shape[0]
    assert seq % PROJ_TM == 0 and (batch * seq) % FFN_TM == 0 and seq // CHUNK >= 3
    kvw = ATTN_KV_HEADS * HEAD_DIM

    cos_t, sina_t, sinb_t = _rope_tables(seq)
    bias = _attn_bias()
    bd = _block_diag_ones(RET_W, BF16)
    bdm = _block_diag_ones(LANES, F32)

    h = x
    for l in range(depth):
        wq, wk, wv, wrq, wrk, wrv, wrg = jnp.split(
            w_in[l], np.cumsum([ATTN_W, kvw, kvw, RET_W, RET_W, RET_W]).tolist(), axis=1)
        swap = lambda w: jnp.concatenate([w[:, HEAD_DIM:], w[:, :HEAD_DIM]], axis=1)
        w_in2 = jnp.concatenate(
            [wq, wk, swap(wk), wv, swap(wv), wrq, wrk, wrv, wrg], axis=1).astype(BF16)
        qnw = jnp.tile(q_norm_w[l], ATTN_HEADS)[None, :]
        knw = jnp.tile(k_norm_w[l], KV2_W // HEAD_DIM)[None, :]
        log_f = -jnp.abs(ret_log_decay_fwd[l].astype(F32))
        log_b = -jnp.abs(ret_log_decay_bwd[l].astype(F32))
        dmat, xif, xib, zf, zb, gf, gb = _retention_tables(log_f, log_b)

        proj = _proj_call(h.reshape(batch * seq, d_model), attn_norm_w[l][None, :], w_in2,
                          qnw, knw, cos_t, sina_t, sinb_t, bd, seq)
        proj3 = proj.reshape(batch, seq, PROJ_W)
        rf, rb = _state_call(proj3, zf, zb, gf, gb, bdm)
        h = _mix_call(attn_sink[l].astype(F32), proj3, rf, rb, h, w_out[l].astype(BF16),
                      bias, dmat, xif, xib, ret_norm_w[l][None, :], bd)
        h = _ffn_call(h.reshape(batch * seq, d_model), ffn_norm_w[l][None, :],
                      w_gate[l].astype(BF16), w_up[l].astype(BF16),
                      w_down[l].astype(BF16)).reshape(batch, seq, d_model)
    return h
```

```python
import jax
import jax.numpy as jnp
import numpy as np
from jax import lax
from jax.experimental import pallas as pl
from jax.experimental.pallas import tpu as pltpu

HEAD_DIM = 64
ATTN_HEADS = 8
ATTN_KV_HEADS = 2
RET_HEADS = 8
CHUNK = 128
ROPE_THETA = 10000.0
EPS = 1e-6
NEG_INF = -1e30

LANES = 128
PAIRS = RET_HEADS // 2
ATTN_W = ATTN_HEADS * HEAD_DIM
RET_W = RET_HEADS * HEAD_DIM
KV2_W = 2 * ATTN_KV_HEADS * HEAD_DIM

C_AQ = 0
C_K2 = C_AQ + ATTN_W
C_V2 = C_K2 + KV2_W
C_RQ = C_V2 + KV2_W
C_RK = C_RQ + RET_W
C_RV = C_RK + RET_W
C_RG = C_RV + RET_W
PROJ_W = C_RG + RET_W

PROJ_TM = 512
MIX_TM = 512
FFN_TM = 512
FFN_CK = 512
VMEM_LIMIT = 56 * 1024 * 1024

BF16 = jnp.bfloat16
F32 = jnp.float32


def _dot(a, b):
    return jnp.dot(a, b, preferred_element_type=F32)


def _dot_nt(a, b):
    return lax.dot_general(a, b, (((1,), (1,)), ((), ())), preferred_element_type=F32)


def _dot_tn(a, b):
    return lax.dot_general(a, b, (((0,), (0,)), ((), ())), preferred_element_type=F32)


def _resident(shape):
    zeros = (0,) * len(shape)
    return pl.BlockSpec(shape, lambda *_: zeros, pipeline_mode=pl.Buffered(1))


def _proj_kernel(x_ref, anw_ref, w_ref, qnw_ref, knw_ref, cos_ref, sina_ref, sinb_ref,
                 bd_ref, o_ref):
    x = x_ref[...]
    ms = jnp.mean(x * x, axis=-1, keepdims=True)
    n = (x * lax.rsqrt(ms + EPS) * anw_ref[...]).astype(BF16)
    cos = cos_ref[...]
    sina = sina_ref[...]
    sinb = sinb_ref[...]

    def proj(c0, width):
        return _dot(n, w_ref[:, c0:c0 + width])

    def head_norm(y, w):
        width = y.shape[-1]
        ss = _dot((y * y).astype(BF16), bd_ref[:width, :width])
        return y * lax.rsqrt(ss * (1.0 / HEAD_DIM) + EPS) * w

    def rope_store(y, c0, scale):
        for s in range(y.shape[-1] // LANES):
            ys = y[:, s * LANES:(s + 1) * LANES]
            r = ys * cos + pltpu.roll(ys, LANES - 32, 1) * sina + pltpu.roll(ys, 32, 1) * sinb
            if scale != 1.0:
                r = r * scale
            o_ref[:, c0 + s * LANES:c0 + (s + 1) * LANES] = r.astype(BF16)

    scale = HEAD_DIM ** -0.5
    rope_store(head_norm(proj(C_AQ, ATTN_W), qnw_ref[...]), C_AQ, scale)
    rope_store(head_norm(proj(C_K2, KV2_W), knw_ref[...]), C_K2, 1.0)
    o_ref[:, C_V2:C_V2 + KV2_W] = proj(C_V2, KV2_W).astype(BF16)
    rope_store(proj(C_RQ, RET_W), C_RQ, 1.0)
    rope_store(proj(C_RK, RET_W), C_RK, scale)
    o_ref[:, C_RV:C_RV + RET_W] = proj(C_RV, RET_W).astype(BF16)
    o_ref[:, C_RG:C_RG + RET_W] = proj(C_RG, RET_W).astype(BF16)


def _proj_call(x2, anw, w_in2, qnw, knw, cos, sina, sinb, bd, seq):
    tokens, d_model = x2.shape
    tm = PROJ_TM
    pos_blocks = seq // tm
    row = lambda i: (i, 0)
    pos = lambda i: (i % pos_blocks, 0)
    return pl.pallas_call(
        _proj_kernel,
        grid=(tokens // tm,),
        in_specs=[
            pl.BlockSpec((tm, d_model), row),
            _resident((1, d_model)),
            _resident((d_model, PROJ_W)),
            _resident((1, ATTN_W)),
            _resident((1, KV2_W)),
            pl.BlockSpec((tm, LANES), pos),
            pl.BlockSpec((tm, LANES), pos),
            pl.BlockSpec((tm, LANES), pos),
            _resident((ATTN_W, ATTN_W)),
        ],
        out_specs=pl.BlockSpec((tm, PROJ_W), row),
        out_shape=jax.ShapeDtypeStruct((tokens, PROJ_W), BF16),
        compiler_params=pltpu.CompilerParams(
            dimension_semantics=("arbitrary",), vmem_limit_bytes=VMEM_LIMIT),
        name="proj",
    )(x2, anw, w_in2, qnw, knw, cos, sina, sinb, bd)


def _state_kernel(kf_ref, vf_ref, kb_ref, vb_ref, zf_ref, zb_ref, gf_ref, gb_ref, bdm_ref,
                  rf_ref, rb_ref, cf_ref, cb_ref):
    @pl.when(pl.program_id(0) == 0)
    def _():
        cf_ref[...] = jnp.zeros_like(cf_ref)
        cb_ref[...] = jnp.zeros_like(cb_ref)

    bdm = bdm_ref[...]
    batch = kf_ref.shape[0]
    for k_ref, v_ref, z_ref, g_ref, r_ref, c_ref in (
            (kf_ref, vf_ref, zf_ref, gf_ref, rf_ref, cf_ref),
            (kb_ref, vb_ref, zb_ref, gb_ref, rb_ref, cb_ref)):
        for b in range(batch):
            for p in range(PAIRS):
                sl = slice(p * LANES, (p + 1) * LANES)
                kz = (k_ref[b, :, sl].astype(F32) * z_ref[:, sl]).astype(BF16)
                kv = _dot_tn(kz, v_ref[b, :, sl])
                carry = c_ref[b, p]
                r_ref[b, p] = carry.astype(BF16)
                c_ref[b, p] = g_ref[p] * carry + kv * bdm


def _state_call(proj3, zf, zb, gf, gb, bdm):
    batch, seq, _ = proj3.shape
    nc = seq // CHUNK
    kcol, vcol = C_RK // RET_W, C_RV // RET_W
    blk = (batch, CHUNK, RET_W)
    out_blk = (batch, None, PAIRS, LANES, LANES)
    out_sds = jax.ShapeDtypeStruct((batch, nc, PAIRS, LANES, LANES), BF16)
    return pl.pallas_call(
        _state_kernel,
        grid=(nc,),
        in_specs=[
            pl.BlockSpec(blk, lambda s: (0, s, kcol)),
            pl.BlockSpec(blk, lambda s: (0, s, vcol)),
            pl.BlockSpec(blk, lambda s: (0, nc - 1 - s, kcol)),
            pl.BlockSpec(blk, lambda s: (0, nc - 1 - s, vcol)),
            _resident((CHUNK, RET_W)),
            _resident((CHUNK, RET_W)),
            _resident((PAIRS, LANES, LANES)),
            _resident((PAIRS, LANES, LANES)),
            _resident((LANES, LANES)),
        ],
        out_specs=[
            pl.BlockSpec(out_blk, lambda s: (0, s, 0, 0, 0)),
            pl.BlockSpec(out_blk, lambda s: (0, nc - 1 - s, 0, 0, 0)),
        ],
        out_shape=[out_sds, out_sds],
        scratch_shapes=[pltpu.VMEM((batch, PAIRS, LANES, LANES), F32),
                        pltpu.VMEM((batch, PAIRS, LANES, LANES), F32)],
        compiler_params=pltpu.CompilerParams(
            dimension_semantics=("arbitrary",), vmem_limit_bytes=VMEM_LIMIT),
        name="ret_state",
    )(proj3, proj3, proj3, proj3, zf, zb, gf, gb, bdm)


def _mix_kernel(sink_ref, aq_ref, kp_ref, kc_ref, kn_ref, vp_ref, vc_ref, vn_ref,
                rq_ref, rk_ref, rv_ref, rg_ref, rf_ref, rb_ref, x_ref, wo_ref,
                bias_ref, dm_ref, xif_ref, xib_ref, rnw_ref, bd_ref, o_ref, y_ref, ret_ref):
    lane = lax.broadcasted_iota(jnp.int32, (CHUNK, LANES), 1).astype(F32).astype(BF16)
    lo = lane < HEAD_DIM
    hi = jnp.logical_not(lo)
    zero = jnp.zeros((), BF16)

    def keep(mask, a):
        reps = a.shape[0] // CHUNK
        m = mask if reps == 1 else jnp.concatenate([mask] * reps, axis=0)
        return jnp.where(m, a, zero)

    tile = pl.program_id(1)
    n_sub = aq_ref.shape[0] // CHUNK
    for j in range(n_sub):
        rows = slice(j * CHUNK, (j + 1) * CHUNK)
        if j == 0:
            k3 = jnp.concatenate([kp_ref[...], kc_ref[0:2 * CHUNK]], axis=0)
            v3 = jnp.concatenate([vp_ref[...], vc_ref[0:2 * CHUNK]], axis=0)
            bias = bias_ref[jnp.where(tile == 0, 0, 1)]
        elif j == n_sub - 1:
            k3 = jnp.concatenate([kc_ref[(j - 1) * CHUNK:(j + 1) * CHUNK], kn_ref[...]], axis=0)
            v3 = jnp.concatenate([vc_ref[(j - 1) * CHUNK:(j + 1) * CHUNK], vn_ref[...]], axis=0)
            bias = bias_ref[jnp.where(tile == pl.num_programs(1) - 1, 2, 1)]
        else:
            k3 = kc_ref[(j - 1) * CHUNK:(j + 2) * CHUNK]
            v3 = vc_ref[(j - 1) * CHUNK:(j + 2) * CHUNK]
            bias = bias_ref[1]
        _mix_chunk(sink_ref, aq_ref[rows], k3, v3, bias, rq_ref, rk_ref, rv_ref, rf_ref, rb_ref,
                   dm_ref, xif_ref, xib_ref, y_ref, ret_ref, j, lo, hi, keep)

    ret = ret_ref[...]
    ss = _dot((ret * ret).astype(BF16), bd_ref[...])
    ret_n = ret * lax.rsqrt(ss * (1.0 / HEAD_DIM) + EPS) * rnw_ref[...]
    gate = rg_ref[...].astype(F32)
    y_ref[:, ATTN_W:] = (gate * (1.0 / (1.0 + jnp.exp(-gate))) * ret_n).astype(BF16)
    o_ref[...] = x_ref[...] + _dot(y_ref[...], wo_ref[...])


def _mix_chunk(sink_ref, q, k3, v3, bias, rq_ref, rk_ref, rv_ref, rf_ref, rb_ref,
               dm_ref, xif_ref, xib_ref, y_ref, ret_ref, j, lo, hi, keep):
    rows = slice(j * CHUNK, (j + 1) * CHUNK)
    group = ATTN_HEADS // ATTN_KV_HEADS
    y_pairs = []
    for g in range(ATTN_KV_HEADS):
        g_low = slice(g * LANES, (g + 1) * LANES)
        g_high = slice((1 - g) * LANES, (2 - g) * LANES)
        kg = jnp.where(jnp.concatenate([lo] * 3, axis=0), k3[:, g_low], k3[:, g_high])
        v_lo = keep(lo, v3[:, g_low])
        v_hi = keep(hi, v3[:, g_high])
        qs = []
        for i in range(group):
            h = g * group + i
            qp = q[:, (h // 2) * LANES:(h // 2 + 1) * LANES]
            qs.append(keep(lo if h % 2 == 0 else hi, qp))
        s_all = _dot_nt(jnp.concatenate(qs, axis=0), kg)
        ps, rs = [], []
        for i in range(group):
            h = g * group + i
            s = s_all[i * CHUNK:(i + 1) * CHUNK] + bias
            sink = sink_ref[h]
            m = jnp.maximum(jnp.max(s, axis=-1, keepdims=True), sink)
            e = jnp.exp(s - m)
            den = jnp.sum(e, axis=-1, keepdims=True) + jnp.exp(sink - m)
            ps.append(e.astype(BF16))
            rs.append(1.0 / den)
        o_lo = _dot(jnp.concatenate([ps[0], ps[2]], axis=0), v_lo)
        o_hi = _dot(jnp.concatenate([ps[1], ps[3]], axis=0), v_hi)
        for pp in range(group // 2):
            part = slice(pp * CHUNK, (pp + 1) * CHUNK)
            y_pairs.append(o_lo[part] * rs[2 * pp] + o_hi[part] * rs[2 * pp + 1])
    y_ref[rows, :ATTN_W] = jnp.concatenate(y_pairs, axis=1).astype(BF16)

    for p in range(PAIRS):
        sl = slice(p * LANES, (p + 1) * LANES)
        qp, kp, vp = rq_ref[rows, sl], rk_ref[rows, sl], rv_ref[rows, sl]
        s2 = _dot_nt(jnp.concatenate([keep(lo, qp), keep(hi, qp)], axis=0), kp)
        p0 = (s2[:CHUNK] * dm_ref[2 * p]).astype(BF16)
        p1 = (s2[CHUNK:] * dm_ref[2 * p + 1]).astype(BF16)
        qf = (qp.astype(F32) * xif_ref[:, sl]).astype(BF16)
        qb = (qp.astype(F32) * xib_ref[:, sl]).astype(BF16)
        lhs = jnp.concatenate([p0, p1, qf, qb], axis=1)
        rhs = jnp.concatenate([keep(lo, vp), keep(hi, vp), rf_ref[j, p], rb_ref[j, p]], axis=0)
        ret_ref[rows, sl] = _dot(lhs, rhs)


def _mix_call(sink, proj3, rf, rb, x3, w_out, bias, dmat, xif, xib, rnw, bd):
    batch, seq, d_model = x3.shape
    tm = MIX_TM
    n_sub = tm // CHUNK
    nt = seq // tm
    nc = seq // CHUNK
    wide = lambda col: pl.BlockSpec((None, tm, RET_W), lambda b, t: (b, t, col))
    kv_main = lambda col: pl.BlockSpec((None, tm, KV2_W), lambda b, t: (b, t, col))
    kv_prev = lambda col: pl.BlockSpec(
        (None, CHUNK, KV2_W), lambda b, t: (b, jnp.maximum(t * n_sub - 1, 0), col))
    kv_next = lambda col: pl.BlockSpec(
        (None, CHUNK, KV2_W), lambda b, t: (b, jnp.minimum((t + 1) * n_sub, nc - 1), col))
    st = pl.BlockSpec((None, n_sub, PAIRS, LANES, LANES), lambda b, t: (b, t, 0, 0, 0))
    k2c, v2c = C_K2 // KV2_W, C_V2 // KV2_W
    return pl.pallas_call(
        _mix_kernel,
        grid=(batch, nt),
        in_specs=[
            pl.BlockSpec(memory_space=pltpu.SMEM),
            wide(C_AQ // ATTN_W),
            kv_prev(k2c), kv_main(k2c), kv_next(k2c),
            kv_prev(v2c), kv_main(v2c), kv_next(v2c),
            wide(C_RQ // RET_W), wide(C_RK // RET_W), wide(C_RV // RET_W), wide(C_RG // RET_W),
            st, st,
            pl.BlockSpec((None, tm, d_model), lambda b, t: (b, t, 0)),
            _resident((d_model, d_model)),
            _resident((3, CHUNK, 3 * CHUNK)),
            _resident((RET_HEADS, CHUNK, CHUNK)),
            _resident((CHUNK, RET_W)),
            _resident((CHUNK, RET_W)),
            _resident((1, RET_W)),
            _resident((RET_W, RET_W)),
        ],
        out_specs=pl.BlockSpec((None, tm, d_model), lambda b, t: (b, t, 0)),
        out_shape=jax.ShapeDtypeStruct((batch, seq, d_model), F32),
        scratch_shapes=[pltpu.VMEM((tm, d_model), BF16), pltpu.VMEM((tm, RET_W), F32)],
        compiler_params=pltpu.CompilerParams(
            dimension_semantics=("arbitrary", "arbitrary"), vmem_limit_bytes=VMEM_LIMIT),
        name="mix",
    )(sink, proj3, proj3, proj3, proj3, proj3, proj3, proj3, proj3, proj3, proj3, proj3,
      rf, rb, x3, w_out, bias, dmat, xif, xib, rnw, bd)


def _ffn_kernel(h_ref, fnw_ref, wg_ref, wu_ref, wd_ref, o_ref, a_ref):
    h = h_ref[...]
    ms = jnp.mean(h * h, axis=-1, keepdims=True)
    m = (h * lax.rsqrt(ms + EPS) * fnw_ref[...]).astype(BF16)
    d_ff = wg_ref.shape[1]
    for c0 in range(0, d_ff, FFN_CK):
        c1 = min(c0 + FFN_CK, d_ff)
        g = _dot(m, wg_ref[:, c0:c1])
        u = _dot(m, wu_ref[:, c0:c1])
        a_ref[:, c0:c1] = (g * (1.0 / (1.0 + jnp.exp(-g))) * u).astype(BF16)
    o_ref[...] = h + _dot(a_ref[...], wd_ref[...])


def _ffn_call(h2, fnw, wg, wu, wd):
    tokens, d_model = h2.shape
    d_ff = wg.shape[1]
    tm = FFN_TM
    row = lambda i: (i, 0)
    return pl.pallas_call(
        _ffn_kernel,
        grid=(tokens // tm,),
        in_specs=[
            pl.BlockSpec((tm, d_model), row),
            _resident((1, d_model)),
            _resident((d_model, d_ff)),
            _resident((d_model, d_ff)),
            _resident((d_ff, d_model)),
        ],
        out_specs=pl.BlockSpec((tm, d_model), row),
        out_shape=jax.ShapeDtypeStruct((tokens, d_model), F32),
        scratch_shapes=[pltpu.VMEM((tm, d_ff), BF16)],
        compiler_params=pltpu.CompilerParams(
            dimension_semantics=("arbitrary",), vmem_limit_bytes=VMEM_LIMIT),
        name="ffn",
    )(h2, fnw, wg, wu, wd)


def _rope_tables(seq):
    inv_freq = ROPE_THETA ** (-jnp.arange(0, HEAD_DIM, 2, dtype=F32) / HEAD_DIM)
    ang = jnp.arange(seq, dtype=F32)[:, None] * inv_freq[None, :]
    cos, sin = jnp.cos(ang), jnp.sin(ang)
    zeros = jnp.zeros_like(sin)
    reps = LANES // HEAD_DIM
    cos_t = jnp.tile(jnp.concatenate([cos, cos], -1), (1, reps))
    sina_t = jnp.tile(jnp.concatenate([-sin, zeros], -1), (1, reps))
    sinb_t = jnp.tile(jnp.concatenate([zeros, sin], -1), (1, reps))
    return cos_t, sina_t, sinb_t


def _attn_bias():
    i = np.arange(CHUNK)[:, None]
    j = np.arange(3 * CHUNK)[None, :]
    band = np.abs(i + CHUNK - j) <= CHUNK
    first = band & (j >= CHUNK)
    last = band & (j < 2 * CHUNK)
    tab = np.stack([first, band, last]).astype(np.float32)
    return jnp.asarray(np.where(tab > 0, 0.0, NEG_INF).astype(np.float32))


def _retention_tables(log_f, log_b):
    idx = jnp.arange(CHUNK, dtype=F32)
    diff = idx[:, None] - idx[None, :]
    lf, lb = log_f[:, None, None], log_b[:, None, None]
    dmat = jnp.where(diff[None] >= 0,
                     jnp.exp(lf * jnp.maximum(diff, 0.0)[None]),
                     jnp.exp(lb * jnp.maximum(-diff, 0.0)[None]))
    per_lane = lambda t: jnp.repeat(t.T, HEAD_DIM, axis=1)
    xif = per_lane(jnp.exp(log_f[:, None] * (idx + 1.0)[None]))
    xib = per_lane(jnp.exp(log_b[:, None] * (CHUNK - idx)[None]))
    zf = per_lane(jnp.exp(log_f[:, None] * (CHUNK - 1.0 - idx)[None]))
    zb = per_lane(jnp.exp(log_b[:, None] * idx[None]))
    per_row = lambda g: jnp.broadcast_to(
        jnp.repeat(g.reshape(PAIRS, 2), HEAD_DIM, axis=1)[:, :, None], (PAIRS, LANES, LANES))
    gf = per_row(jnp.exp(log_f * CHUNK))
    gb = per_row(jnp.exp(log_b * CHUNK))
    return dmat, xif, xib, zf, zb, gf, gb


def _block_diag_ones(width, dtype):
    r = np.arange(width) // HEAD_DIM
    return jnp.asarray((r[:, None] == r[None, :]).astype(np.float32), dtype=dtype)


def kernel(x, attn_norm_w, w_in, q_norm_w, k_norm_w, attn_sink, ret_log_decay_fwd,
           ret_log_decay_bwd, ret_norm_w, w_out, ffn_norm_w, w_gate, w_up, w_down):
    batch, seq, d_model = x.shape
    depth = w_in.shape[0]
    assert seq % PROJ_TM == 0 and seq % MIX_TM == 0 and (batch * seq) % FFN_TM == 0
    assert MIX_TM // CHUNK >= 2
    kvw = ATTN_KV_HEADS * HEAD_DIM

    cos_t, sina_t, sinb_t = _rope_tables(seq)
    bias = _attn_bias()
    bd = _block_diag_ones(RET_W, BF16)
    bdm = _block_diag_ones(LANES, F32)

    h = x
    for l in range(depth):
        wq, wk, wv, wrq, wrk, wrv, wrg = jnp.split(
            w_in[l], np.cumsum([ATTN_W, kvw, kvw, RET_W, RET_W, RET_W]).tolist(), axis=1)
        swap = lambda w: jnp.concatenate([w[:, HEAD_DIM:], w[:, :HEAD_DIM]], axis=1)
        w_in2 = jnp.concatenate(
            [wq, wk, swap(wk), wv, swap(wv), wrq, wrk, wrv, wrg], axis=1).astype(BF16)
        qnw = jnp.tile(q_norm_w[l], ATTN_HEADS)[None, :]
        knw = jnp.tile(k_norm_w[l], KV2_W // HEAD_DIM)[None, :]
        log_f = -jnp.abs(ret_log_decay_fwd[l].astype(F32))
        log_b = -jnp.abs(ret_log_decay_bwd[l].astype(F32))
        dmat, xif, xib, zf, zb, gf, gb = _retention_tables(log_f, log_b)

        proj = _proj_call(h.reshape(batch * seq, d_model), attn_norm_w[l][None, :], w_in2,
                          qnw, knw, cos_t, sina_t, sinb_t, bd, seq)
        proj3 = proj.reshape(batch, seq, PROJ_W)
        rf, rb = _state_call(proj3, zf, zb, gf, gb, bdm)
        h = _mix_call(attn_sink[l].astype(F32), proj3, rf, rb, h, w_out[l].astype(BF16),
                      bias, dmat, xif, xib, ret_norm_w[l][None, :], bd)
        h = _ffn_call(h.reshape(batch * seq, d_model), ffn_norm_w[l][None, :],
                      w_gate[l].astype(BF16), w_up[l].astype(BF16),
                      w_down[l].astype(BF16)).reshape(batch, seq, d_model)
    return h
```

```python
import jax
import jax.numpy as jnp
import numpy as np
from jax import lax
from jax.experimental import pallas as pl
from jax.experimental.pallas import tpu as pltpu

HEAD_DIM = 64
ATTN_HEADS = 8
ATTN_KV_HEADS = 2
RET_HEADS = 8
CHUNK = 128
ROPE_THETA = 10000.0
EPS = 1e-6
NEG_INF = -1e30
LOG2_E = 1.4426950408889634

LANES = 128
PAIRS = RET_HEADS // 2
ATTN_W = ATTN_HEADS * HEAD_DIM
RET_W = RET_HEADS * HEAD_DIM
KV2_W = 2 * ATTN_KV_HEADS * HEAD_DIM

C_AQ = 0
C_K2 = C_AQ + ATTN_W
C_V2 = C_K2 + KV2_W
C_RQ = C_V2 + KV2_W
C_RK = C_RQ + RET_W
C_RV = C_RK + RET_W
C_RG = C_RV + RET_W
PROJ_W = C_RG + RET_W

PROJ_TM = 512
MIX_TM = 512
FFN_TM = 512
FFN_CK = 512
VMEM_LIMIT = 56 * 1024 * 1024

BF16 = jnp.bfloat16
F32 = jnp.float32


def _dot(a, b):
    return jnp.dot(a, b, preferred_element_type=F32)


def _dot_nt(a, b):
    return lax.dot_general(a, b, (((1,), (1,)), ((), ())), preferred_element_type=F32)


def _dot_tn(a, b):
    return lax.dot_general(a, b, (((0,), (0,)), ((), ())), preferred_element_type=F32)


def _resident(shape):
    zeros = (0,) * len(shape)
    return pl.BlockSpec(shape, lambda *_: zeros, pipeline_mode=pl.Buffered(1))


def _proj_kernel(x_ref, anw_ref, w_ref, qnw_ref, knw_ref, cos_ref, sina_ref, sinb_ref,
                 bd_ref, o_ref):
    x = x_ref[...]
    ms = jnp.mean(x * x, axis=-1, keepdims=True)
    n = (x * lax.rsqrt(ms + EPS) * anw_ref[...]).astype(BF16)
    cos = cos_ref[...]
    sina = sina_ref[...]
    sinb = sinb_ref[...]

    def proj(c0, width):
        return _dot(n, w_ref[:, c0:c0 + width])

    def head_norm(y, w):
        width = y.shape[-1]
        ss = _dot((y * y).astype(BF16), bd_ref[:width, :width])
        return y * lax.rsqrt(ss * (1.0 / HEAD_DIM) + EPS) * w

    def rope_store(y, c0, scale):
        for s in range(y.shape[-1] // LANES):
            ys = y[:, s * LANES:(s + 1) * LANES]
            r = ys * cos + pltpu.roll(ys, LANES - 32, 1) * sina + pltpu.roll(ys, 32, 1) * sinb
            if scale != 1.0:
                r = r * scale
            o_ref[:, c0 + s * LANES:c0 + (s + 1) * LANES] = r.astype(BF16)

    scale = HEAD_DIM ** -0.5
    rope_store(head_norm(proj(C_AQ, ATTN_W), qnw_ref[...]), C_AQ, scale * LOG2_E)
    rope_store(head_norm(proj(C_K2, KV2_W), knw_ref[...]), C_K2, 1.0)
    o_ref[:, C_V2:C_V2 + KV2_W] = proj(C_V2, KV2_W).astype(BF16)
    rope_store(proj(C_RQ, RET_W), C_RQ, 1.0)
    rope_store(proj(C_RK, RET_W), C_RK, scale)
    o_ref[:, C_RV:C_RV + RET_W] = proj(C_RV, RET_W).astype(BF16)
    o_ref[:, C_RG:C_RG + RET_W] = proj(C_RG, RET_W).astype(BF16)


def _proj_call(x2, anw, w_in2, qnw, knw, cos, sina, sinb, bd, seq):
    tokens, d_model = x2.shape
    tm = PROJ_TM
    pos_blocks = seq // tm
    row = lambda i: (i, 0)
    pos = lambda i: (i % pos_blocks, 0)
    return pl.pallas_call(
        _proj_kernel,
        grid=(tokens // tm,),
        in_specs=[
            pl.BlockSpec((tm, d_model), row),
            _resident((1, d_model)),
            _resident((d_model, PROJ_W)),
            _resident((1, ATTN_W)),
            _resident((1, KV2_W)),
            pl.BlockSpec((tm, LANES), pos),
            pl.BlockSpec((tm, LANES), pos),
            pl.BlockSpec((tm, LANES), pos),
            _resident((ATTN_W, ATTN_W)),
        ],
        out_specs=pl.BlockSpec((tm, PROJ_W), row),
        out_shape=jax.ShapeDtypeStruct((tokens, PROJ_W), BF16),
        compiler_params=pltpu.CompilerParams(
            dimension_semantics=("arbitrary",), vmem_limit_bytes=VMEM_LIMIT),
        name="proj",
    )(x2, anw, w_in2, qnw, knw, cos, sina, sinb, bd)


def _state_kernel(kf_ref, vf_ref, kb_ref, vb_ref, zf_ref, zb_ref, gf_ref, gb_ref, bdm_ref,
                  rf_ref, rb_ref, cf_ref, cb_ref):
    @pl.when(pl.program_id(0) == 0)
    def _():
        cf_ref[...] = jnp.zeros_like(cf_ref)
        cb_ref[...] = jnp.zeros_like(cb_ref)

    bdm = bdm_ref[...]
    batch = kf_ref.shape[0]
    for k_ref, v_ref, z_ref, g_ref, r_ref, c_ref in (
            (kf_ref, vf_ref, zf_ref, gf_ref, rf_ref, cf_ref),
            (kb_ref, vb_ref, zb_ref, gb_ref, rb_ref, cb_ref)):
        for b in range(batch):
            for p in range(PAIRS):
                sl = slice(p * LANES, (p + 1) * LANES)
                kz = (k_ref[b, :, sl].astype(F32) * z_ref[:, sl]).astype(BF16)
                kv = _dot_tn(kz, v_ref[b, :, sl])
                carry = c_ref[b, p]
                r_ref[b, p] = carry.astype(BF16)
                c_ref[b, p] = g_ref[p] * carry + kv * bdm


def _state_call(proj3, zf, zb, gf, gb, bdm):
    batch, seq, _ = proj3.shape
    nc = seq // CHUNK
    kcol, vcol = C_RK // RET_W, C_RV // RET_W
    blk = (batch, CHUNK, RET_W)
    out_blk = (batch, None, PAIRS, LANES, LANES)
    out_sds = jax.ShapeDtypeStruct((batch, nc, PAIRS, LANES, LANES), BF16)
    return pl.pallas_call(
        _state_kernel,
        grid=(nc,),
        in_specs=[
            pl.BlockSpec(blk, lambda s: (0, s, kcol)),
            pl.BlockSpec(blk, lambda s: (0, s, vcol)),
            pl.BlockSpec(blk, lambda s: (0, nc - 1 - s, kcol)),
            pl.BlockSpec(blk, lambda s: (0, nc - 1 - s, vcol)),
            _resident((CHUNK, RET_W)),
            _resident((CHUNK, RET_W)),
            _resident((PAIRS, LANES, LANES)),
            _resident((PAIRS, LANES, LANES)),
            _resident((LANES, LANES)),
        ],
        out_specs=[
            pl.BlockSpec(out_blk, lambda s: (0, s, 0, 0, 0)),
            pl.BlockSpec(out_blk, lambda s: (0, nc - 1 - s, 0, 0, 0)),
        ],
        out_shape=[out_sds, out_sds],
        scratch_shapes=[pltpu.VMEM((batch, PAIRS, LANES, LANES), F32),
                        pltpu.VMEM((batch, PAIRS, LANES, LANES), F32)],
        compiler_params=pltpu.CompilerParams(
            dimension_semantics=("arbitrary",), vmem_limit_bytes=VMEM_LIMIT),
        name="ret_state",
    )(proj3, proj3, proj3, proj3, zf, zb, gf, gb, bdm)


def _mix_kernel(sink_ref, aq_ref, kp_ref, kc_ref, kn_ref, vp_ref, vc_ref, vn_ref,
                rq_ref, rk_ref, rv_ref, rg_ref, rf_ref, rb_ref, x_ref, wo_ref,
                bias_ref, dm_ref, xif_ref, xib_ref, rnw_ref, bd_ref, o_ref, y_ref, ret_ref):
    lane = lax.broadcasted_iota(jnp.int32, (CHUNK, LANES), 1).astype(F32).astype(BF16)
    lo = lane < HEAD_DIM
    hi = jnp.logical_not(lo)
    zero = jnp.zeros((), BF16)

    def keep(mask, a):
        reps = a.shape[0] // CHUNK
        m = mask if reps == 1 else jnp.concatenate([mask] * reps, axis=0)
        return jnp.where(m, a, zero)

    tile = pl.program_id(1)
    n_sub = aq_ref.shape[0] // CHUNK
    group = ATTN_HEADS // ATTN_KV_HEADS

    def kv_window(j):
        if j == 0:
            k3 = jnp.concatenate([kp_ref[...], kc_ref[0:2 * CHUNK]], axis=0)
            v3 = jnp.concatenate([vp_ref[...], vc_ref[0:2 * CHUNK]], axis=0)
            bias = bias_ref[jnp.where(tile == 0, 0, 1)]
        elif j == n_sub - 1:
            k3 = jnp.concatenate([kc_ref[(j - 1) * CHUNK:(j + 1) * CHUNK], kn_ref[...]], axis=0)
            v3 = jnp.concatenate([vc_ref[(j - 1) * CHUNK:(j + 1) * CHUNK], vn_ref[...]], axis=0)
            bias = bias_ref[jnp.where(tile == pl.num_programs(1) - 1, 2, 1)]
        else:
            k3 = kc_ref[(j - 1) * CHUNK:(j + 2) * CHUNK]
            v3 = vc_ref[(j - 1) * CHUNK:(j + 2) * CHUNK]
            bias = bias_ref[1]
        return k3, v3, bias

    def attn_scores(j, g):
        k3, v3, bias = kv_window(j)
        g_low = slice(g * LANES, (g + 1) * LANES)
        g_high = slice((1 - g) * LANES, (2 - g) * LANES)
        kg = jnp.where(jnp.concatenate([lo] * 3, axis=0), k3[:, g_low], k3[:, g_high])
        v_lo = keep(lo, v3[:, g_low])
        v_hi = keep(hi, v3[:, g_high])
        qs = []
        for i in range(group):
            h = g * group + i
            qp = aq_ref[j * CHUNK:(j + 1) * CHUNK, (h // 2) * LANES:(h // 2 + 1) * LANES]
            qs.append(keep(lo if h % 2 == 0 else hi, qp))
        s_all = _dot_nt(jnp.concatenate(qs, axis=0), kg)
        return s_all, v_lo, v_hi, bias

    def attn_finish(j, g, s_all, v_lo, v_hi, bias):
        ps, rs = [], []
        for i in range(group):
            h = g * group + i
            s = s_all[i * CHUNK:(i + 1) * CHUNK] + bias
            sink = sink_ref[h] * LOG2_E
            m = jnp.maximum(jnp.max(s, axis=-1, keepdims=True), sink)
            e = jnp.exp2(s - m)
            den = jnp.sum(e, axis=-1, keepdims=True) + jnp.exp2(sink - m)
            ps.append(e.astype(BF16))
            rs.append(1.0 / den)
        o_lo = _dot(jnp.concatenate([ps[0], ps[2]], axis=0), v_lo)
        o_hi = _dot(jnp.concatenate([ps[1], ps[3]], axis=0), v_hi)
        for pp in range(group // 2):
            part = slice(pp * CHUNK, (pp + 1) * CHUNK)
            pair = g * (group // 2) + pp
            y_ref[j * CHUNK:(j + 1) * CHUNK, pair * LANES:(pair + 1) * LANES] = (
                o_lo[part] * rs[2 * pp] + o_hi[part] * rs[2 * pp + 1]).astype(BF16)

    def retention_pair(j, p):
        rows = slice(j * CHUNK, (j + 1) * CHUNK)
        sl = slice(p * LANES, (p + 1) * LANES)
        qp, kp, vp = rq_ref[rows, sl], rk_ref[rows, sl], rv_ref[rows, sl]
        s2 = _dot_nt(jnp.concatenate([keep(lo, qp), keep(hi, qp)], axis=0), kp)
        p0 = (s2[:CHUNK] * dm_ref[2 * p]).astype(BF16)
        p1 = (s2[CHUNK:] * dm_ref[2 * p + 1]).astype(BF16)
        qf = (qp.astype(F32) * xif_ref[:, sl]).astype(BF16)
        qb = (qp.astype(F32) * xib_ref[:, sl]).astype(BF16)
        lhs = jnp.concatenate([p0, p1, qf, qb], axis=1)
        rhs = jnp.concatenate([keep(lo, vp), keep(hi, vp), rf_ref[j, p], rb_ref[j, p]], axis=0)
        ret_ref[rows, sl] = _dot(lhs, rhs)

    units = [(j, g) for j in range(n_sub) for g in range(ATTN_KV_HEADS)]
    pairs_per_unit = PAIRS // ATTN_KV_HEADS
    pending = attn_scores(*units[0])
    for u, (j, g) in enumerate(units):
        following = attn_scores(*units[u + 1]) if u + 1 < len(units) else None
        for p in range(g * pairs_per_unit, (g + 1) * pairs_per_unit):
            retention_pair(j, p)
        attn_finish(j, g, *pending)
        pending = following

    ret = ret_ref[...]
    ss = _dot((ret * ret).astype(BF16), bd_ref[...])
    ret_n = ret * lax.rsqrt(ss * (1.0 / HEAD_DIM) + EPS) * rnw_ref[...]
    gate = rg_ref[...].astype(F32)
    y_ref[:, ATTN_W:] = (gate * (1.0 / (1.0 + jnp.exp(-gate))) * ret_n).astype(BF16)
    o_ref[...] = x_ref[...] + _dot(y_ref[...], wo_ref[...])


def _mix_call(sink, proj3, rf, rb, x3, w_out, bias, dmat, xif, xib, rnw, bd):
    batch, seq, d_model = x3.shape
    tm = MIX_TM
    n_sub = tm // CHUNK
    nt = seq // tm
    nc = seq // CHUNK
    wide = lambda col: pl.BlockSpec((None, tm, RET_W), lambda b, t: (b, t, col))
    kv_main = lambda col: pl.BlockSpec((None, tm, KV2_W), lambda b, t: (b, t, col))
    kv_prev = lambda col: pl.BlockSpec(
        (None, CHUNK, KV2_W), lambda b, t: (b, jnp.maximum(t * n_sub - 1, 0), col))
    kv_next = lambda col: pl.BlockSpec(
        (None, CHUNK, KV2_W), lambda b, t: (b, jnp.minimum((t + 1) * n_sub, nc - 1), col))
    st = pl.BlockSpec((None, n_sub, PAIRS, LANES, LANES), lambda b, t: (b, t, 0, 0, 0))
    k2c, v2c = C_K2 // KV2_W, C_V2 // KV2_W
    return pl.pallas_call(
        _mix_kernel,
        grid=(batch, nt),
        in_specs=[
            pl.BlockSpec(memory_space=pltpu.SMEM),
            wide(C_AQ // ATTN_W),
            kv_prev(k2c), kv_main(k2c), kv_next(k2c),
            kv_prev(v2c), kv_main(v2c), kv_next(v2c),
            wide(C_RQ // RET_W), wide(C_RK // RET_W), wide(C_RV // RET_W), wide(C_RG // RET_W),
            st, st,
            pl.BlockSpec((None, tm, d_model), lambda b, t: (b, t, 0)),
            _resident((d_model, d_model)),
            _resident((3, CHUNK, 3 * CHUNK)),
            _resident((RET_HEADS, CHUNK, CHUNK)),
            _resident((CHUNK, RET_W)),
            _resident((CHUNK, RET_W)),
            _resident((1, RET_W)),
            _resident((RET_W, RET_W)),
        ],
        out_specs=pl.BlockSpec((None, tm, d_model), lambda b, t: (b, t, 0)),
        out_shape=jax.ShapeDtypeStruct((batch, seq, d_model), F32),
        scratch_shapes=[pltpu.VMEM((tm, d_model), BF16), pltpu.VMEM((tm, RET_W), F32)],
        compiler_params=pltpu.CompilerParams(
            dimension_semantics=("arbitrary", "arbitrary"), vmem_limit_bytes=VMEM_LIMIT),
        name="mix",
    )(sink, proj3, proj3, proj3, proj3, proj3, proj3, proj3, proj3, proj3, proj3, proj3,
      rf, rb, x3, w_out, bias, dmat, xif, xib, rnw, bd)


def _ffn_kernel(h_ref, fnw_ref, wg_ref, wu_ref, wd_ref, o_ref, a_ref):
    h = h_ref[...]
    ms = jnp.mean(h * h, axis=-1, keepdims=True)
    m = (h * lax.rsqrt(ms + EPS) * fnw_ref[...]).astype(BF16)
    d_ff = wg_ref.shape[1]
    for c0 in range(0, d_ff, FFN_CK):
        c1 = min(c0 + FFN_CK, d_ff)
        g = _dot(m, wg_ref[:, c0:c1])
        u = _dot(m, wu_ref[:, c0:c1])
        a_ref[:, c0:c1] = (g * (1.0 / (1.0 + jnp.exp(-g))) * u).astype(BF16)
    o_ref[...] = h + _dot(a_ref[...], wd_ref[...])


def _ffn_call(h2, fnw, wg, wu, wd):
    tokens, d_model = h2.shape
    d_ff = wg.shape[1]
    tm = FFN_TM
    row = lambda i: (i, 0)
    return pl.pallas_call(
        _ffn_kernel,
        grid=(tokens // tm,),
        in_specs=[
            pl.BlockSpec((tm, d_model), row),
            _resident((1, d_model)),
            _resident((d_model, d_ff)),
            _resident((d_model, d_ff)),
            _resident((d_ff, d_model)),
        ],
        out_specs=pl.BlockSpec((tm, d_model), row),
        out_shape=jax.ShapeDtypeStruct((tokens, d_model), F32),
        scratch_shapes=[pltpu.VMEM((tm, d_ff), BF16)],
        compiler_params=pltpu.CompilerParams(
            dimension_semantics=("arbitrary",), vmem_limit_bytes=VMEM_LIMIT),
        name="ffn",
    )(h2, fnw, wg, wu, wd)


def _rope_tables(seq):
    inv_freq = ROPE_THETA ** (-jnp.arange(0, HEAD_DIM, 2, dtype=F32) / HEAD_DIM)
    ang = jnp.arange(seq, dtype=F32)[:, None] * inv_freq[None, :]
    cos, sin = jnp.cos(ang), jnp.sin(ang)
    zeros = jnp.zeros_like(sin)
    reps = LANES // HEAD_DIM
    cos_t = jnp.tile(jnp.concatenate([cos, cos], -1), (1, reps))
    sina_t = jnp.tile(jnp.concatenate([-sin, zeros], -1), (1, reps))
    sinb_t = jnp.tile(jnp.concatenate([zeros, sin], -1), (1, reps))
    return cos_t, sina_t, sinb_t


def _attn_bias():
    i = np.arange(CHUNK)[:, None]
    j = np.arange(3 * CHUNK)[None, :]
    band = np.abs(i + CHUNK - j) <= CHUNK
    first = band & (j >= CHUNK)
    last = band & (j < 2 * CHUNK)
    tab = np.stack([first, band, last]).astype(np.float32)
    return jnp.asarray(np.where(tab > 0, 0.0, NEG_INF).astype(np.float32))


def _retention_tables(log_f, log_b):
    idx = jnp.arange(CHUNK, dtype=F32)
    diff = idx[:, None] - idx[None, :]
    lf, lb = log_f[:, None, None], log_b[:, None, None]
    dmat = jnp.where(diff[None] >= 0,
                     jnp.exp(lf * jnp.maximum(diff, 0.0)[None]),
                     jnp.exp(lb * jnp.maximum(-diff, 0.0)[None]))
    per_lane = lambda t: jnp.repeat(t.T, HEAD_DIM, axis=1)
    xif = per_lane(jnp.exp(log_f[:, None] * (idx + 1.0)[None]))
    xib = per_lane(jnp.exp(log_b[:, None] * (CHUNK - idx)[None]))
    zf = per_lane(jnp.exp(log_f[:, None] * (CHUNK - 1.0 - idx)[None]))
    zb = per_lane(jnp.exp(log_b[:, None] * idx[None]))
    per_row = lambda g: jnp.broadcast_to(
        jnp.repeat(g.reshape(PAIRS, 2), HEAD_DIM, axis=1)[:, :, None], (PAIRS, LANES, LANES))
    gf = per_row(jnp.exp(log_f * CHUNK))
    gb = per_row(jnp.exp(log_b * CHUNK))
    return dmat, xif, xib, zf, zb, gf, gb


def _block_diag_ones(width, dtype):
    r = np.arange(width) // HEAD_DIM
    return jnp.asarray((r[:, None] == r[None, :]).astype(np.float32), dtype=dtype)


def kernel(x, attn_norm_w, w_in, q_norm_w, k_norm_w, attn_sink, ret_log_decay_fwd,
           ret_log_decay_bwd, ret_norm_w, w_out, ffn_norm_w, w_gate, w_up, w_down):
    batch, seq, d_model = x.shape
    depth = w_in.shape[0]
    assert seq % PROJ_TM == 0 and seq % MIX_TM == 0 and (batch * seq) % FFN_TM == 0
    assert MIX_TM // CHUNK >= 2
    kvw = ATTN_KV_HEADS * HEAD_DIM

    cos_t, sina_t, sinb_t = _rope_tables(seq)
    bias = _attn_bias()
    bd = _block_diag_ones(RET_W, BF16)
    bdm = _block_diag_ones(LANES, F32)

    h = x
    for l in range(depth):
        wq, wk, wv, wrq, wrk, wrv, wrg = jnp.split(
            w_in[l], np.cumsum([ATTN_W, kvw, kvw, RET_W, RET_W, RET_W]).tolist(), axis=1)
        swap = lambda w: jnp.concatenate([w[:, HEAD_DIM:], w[:, :HEAD_DIM]], axis=1)
        w_in2 = jnp.concatenate(
            [wq, wk, swap(wk), wv, swap(wv), wrq, wrk, wrv, wrg], axis=1).astype(BF16)
        qnw = jnp.tile(q_norm_w[l], ATTN_HEADS)[None, :]
        knw = jnp.tile(k_norm_w[l], KV2_W // HEAD_DIM)[None, :]
        log_f = -jnp.abs(ret_log_decay_fwd[l].astype(F32))
        log_b = -jnp.abs(ret_log_decay_bwd[l].astype(F32))
        dmat, xif, xib, zf, zb, gf, gb = _retention_tables(log_f, log_b)

        proj = _proj_call(h.reshape(batch * seq, d_model), attn_norm_w[l][None, :], w_in2,
                          qnw, knw, cos_t, sina_t, sinb_t, bd, seq)
        proj3 = proj.reshape(batch, seq, PROJ_W)
        rf, rb = _state_call(proj3, zf, zb, gf, gb, bdm)
        h = _mix_call(attn_sink[l].astype(F32), proj3, rf, rb, h, w_out[l].astype(BF16),
                      bias, dmat, xif, xib, ret_norm_w[l][None, :], bd)
        h = _ffn_call(h.reshape(batch * seq, d_model), ffn_norm_w[l][None, :],
                      w_gate[l].astype(BF16), w_up[l].astype(BF16),
                      w_down[l].astype(BF16)).reshape(batch, seq, d_model)
    return h
```

```python
import jax
import jax.numpy as jnp
import numpy as np
from jax import lax
from jax.experimental import pallas as pl
from jax.experimental.pallas import tpu as pltpu

HEAD_DIM = 64
ATTN_HEADS = 8
ATTN_KV_HEADS = 2
RET_HEADS = 8
CHUNK = 128
ROPE_THETA = 10000.0
EPS = 1e-6
NEG_INF = -1e30
LOG2_E = 1.4426950408889634

LANES = 128
PAIRS = RET_HEADS // 2
ATTN_W = ATTN_HEADS * HEAD_DIM
RET_W = RET_HEADS * HEAD_DIM
KV2_W = 2 * ATTN_KV_HEADS * HEAD_DIM

C_AQ = 0
C_K2 = C_AQ + ATTN_W
C_V2 = C_K2 + KV2_W
C_RQ = C_V2 + KV2_W
C_RK = C_RQ + RET_W
C_RV = C_RK + RET_W
C_RG = C_RV + RET_W
PROJ_W = C_RG + RET_W

PROJ_TM = 512
MIX_TM = 512
FFN_TM = 512
FFN_CK = 512
VMEM_LIMIT = 56 * 1024 * 1024

BF16 = jnp.bfloat16
F32 = jnp.float32


def _dot(a, b):
    return jnp.dot(a, b, preferred_element_type=F32)


def _dot_nt(a, b):
    return lax.dot_general(a, b, (((1,), (1,)), ((), ())), preferred_element_type=F32)


def _dot_tn(a, b):
    return lax.dot_general(a, b, (((0,), (0,)), ((), ())), preferred_element_type=F32)


def _resident(shape):
    zeros = (0,) * len(shape)
    return pl.BlockSpec(shape, lambda *_: zeros, pipeline_mode=pl.Buffered(1))


def _proj_kernel(x_ref, anw_ref, w_ref, qnw_ref, knw_ref, cos_ref, sina_ref, sinb_ref,
                 bd_ref, o_ref):
    x = x_ref[...]
    ms = jnp.mean(x * x, axis=-1, keepdims=True)
    n = (x * lax.rsqrt(ms + EPS) * anw_ref[...]).astype(BF16)
    cos = cos_ref[...]
    sina = sina_ref[...]
    sinb = sinb_ref[...]

    def proj(c0, width):
        return _dot(n, w_ref[:, c0:c0 + width])

    def head_norm(y, w):
        width = y.shape[-1]
        ss = _dot((y * y).astype(BF16), bd_ref[:width, :width])
        return y * lax.rsqrt(ss * (1.0 / HEAD_DIM) + EPS) * w

    def rope(ys):
        return ys * cos + pltpu.roll(ys, LANES - 32, 1) * sina + pltpu.roll(ys, 32, 1) * sinb

    def rope_store(y, c0, scale):
        for s in range(y.shape[-1] // LANES):
            r = rope(y[:, s * LANES:(s + 1) * LANES])
            if scale != 1.0:
                r = r * scale
            o_ref[:, c0 + s * LANES:c0 + (s + 1) * LANES] = r.astype(BF16)

    def plain_store(y, c0):
        o_ref[:, c0:c0 + y.shape[-1]] = y.astype(BF16)

    def store_both_orders(y, c0):
        o_ref[:, c0:c0 + LANES] = y.astype(BF16)
        o_ref[:, c0 + LANES:c0 + 2 * LANES] = pltpu.roll(y, HEAD_DIM, 1).astype(BF16)

    def kv_finish(y):
        store_both_orders(rope(head_norm(y[:, :LANES], knw_ref[...])), C_K2)
        store_both_orders(y[:, LANES:], C_V2)

    scale = HEAD_DIM ** -0.5
    kvw = ATTN_KV_HEADS * HEAD_DIM
    in_kv = ATTN_W
    in_rq = in_kv + 2 * kvw
    stages = [
        (0, ATTN_W, lambda y: rope_store(head_norm(y, qnw_ref[...]), C_AQ, scale * LOG2_E)),
        (in_kv, 2 * kvw, kv_finish),
        (in_rq, RET_W, lambda y: rope_store(y, C_RQ, 1.0)),
        (in_rq + RET_W, RET_W, lambda y: rope_store(y, C_RK, scale)),
        (in_rq + 2 * RET_W, RET_W, lambda y: plain_store(y, C_RV)),
        (in_rq + 3 * RET_W, RET_W, lambda y: plain_store(y, C_RG)),
    ]
    pending = proj(stages[0][0], stages[0][1])
    for i, (_, _, finish) in enumerate(stages):
        following = proj(stages[i + 1][0], stages[i + 1][1]) if i + 1 < len(stages) else None
        finish(pending)
        pending = following


def _proj_call(x2, anw, w_in, qnw, knw, cos, sina, sinb, bd, seq):
    tokens, d_model = x2.shape
    tm = PROJ_TM
    pos_blocks = seq // tm
    row = lambda i: (i, 0)
    pos = lambda i: (i % pos_blocks, 0)
    return pl.pallas_call(
        _proj_kernel,
        grid=(tokens // tm,),
        in_specs=[
            pl.BlockSpec((tm, d_model), row),
            _resident((1, d_model)),
            _resident(w_in.shape),
            _resident((1, ATTN_W)),
            _resident((1, LANES)),
            pl.BlockSpec((tm, LANES), pos),
            pl.BlockSpec((tm, LANES), pos),
            pl.BlockSpec((tm, LANES), pos),
            _resident((ATTN_W, ATTN_W)),
        ],
        out_specs=pl.BlockSpec((tm, PROJ_W), row),
        out_shape=jax.ShapeDtypeStruct((tokens, PROJ_W), BF16),
        compiler_params=pltpu.CompilerParams(
            dimension_semantics=("arbitrary",), vmem_limit_bytes=VMEM_LIMIT),
        name="proj",
    )(x2, anw, w_in, qnw, knw, cos, sina, sinb, bd)


def _state_kernel(kf_ref, vf_ref, kb_ref, vb_ref, zf_ref, zb_ref, gf_ref, gb_ref, bdm_ref,
                  rf_ref, rb_ref, cf_ref, cb_ref):
    @pl.when(pl.program_id(0) == 0)
    def _():
        cf_ref[...] = jnp.zeros_like(cf_ref)
        cb_ref[...] = jnp.zeros_like(cb_ref)

    bdm = bdm_ref[...]
    batch = kf_ref.shape[0]
    for k_ref, v_ref, z_ref, g_ref, r_ref, c_ref in (
            (kf_ref, vf_ref, zf_ref, gf_ref, rf_ref, cf_ref),
            (kb_ref, vb_ref, zb_ref, gb_ref, rb_ref, cb_ref)):
        for b in range(batch):
            for p in range(PAIRS):
                sl = slice(p * LANES, (p + 1) * LANES)
                kz = (k_ref[b, :, sl].astype(F32) * z_ref[:, sl]).astype(BF16)
                kv = _dot_tn(kz, v_ref[b, :, sl])
                carry = c_ref[b, p]
                r_ref[b, p] = carry.astype(BF16)
                c_ref[b, p] = g_ref[p] * carry + kv * bdm


def _state_call(proj3, zf, zb, gf, gb, bdm):
    batch, seq, _ = proj3.shape
    nc = seq // CHUNK
    kcol, vcol = C_RK // RET_W, C_RV // RET_W
    blk = (batch, CHUNK, RET_W)
    out_blk = (batch, None, PAIRS, LANES, LANES)
    out_sds = jax.ShapeDtypeStruct((batch, nc, PAIRS, LANES, LANES), BF16)
    return pl.pallas_call(
        _state_kernel,
        grid=(nc,),
        in_specs=[
            pl.BlockSpec(blk, lambda s: (0, s, kcol)),
            pl.BlockSpec(blk, lambda s: (0, s, vcol)),
            pl.BlockSpec(blk, lambda s: (0, nc - 1 - s, kcol)),
            pl.BlockSpec(blk, lambda s: (0, nc - 1 - s, vcol)),
            _resident((CHUNK, RET_W)),
            _resident((CHUNK, RET_W)),
            _resident((PAIRS, LANES, LANES)),
            _resident((PAIRS, LANES, LANES)),
            _resident((LANES, LANES)),
        ],
        out_specs=[
            pl.BlockSpec(out_blk, lambda s: (0, s, 0, 0, 0)),
            pl.BlockSpec(out_blk, lambda s: (0, nc - 1 - s, 0, 0, 0)),
        ],
        out_shape=[out_sds, out_sds],
        scratch_shapes=[pltpu.VMEM((batch, PAIRS, LANES, LANES), F32),
                        pltpu.VMEM((batch, PAIRS, LANES, LANES), F32)],
        compiler_params=pltpu.CompilerParams(
            dimension_semantics=("arbitrary",), vmem_limit_bytes=VMEM_LIMIT),
        name="ret_state",
    )(proj3, proj3, proj3, proj3, zf, zb, gf, gb, bdm)


def _mix_kernel(sink_ref, p_ref, prev_ref, next_ref, rf_ref, rb_ref, x_ref, wo_ref,
                bias_ref, dm_ref, xif_ref, xib_ref, rnw_ref, bd_ref, o_ref, y_ref, ret_ref):
    lane = lax.broadcasted_iota(jnp.int32, (CHUNK, LANES), 1).astype(F32).astype(BF16)
    lo = lane < HEAD_DIM
    hi = jnp.logical_not(lo)
    zero = jnp.zeros((), BF16)

    def keep(mask, a):
        reps = a.shape[0] // CHUNK
        m = mask if reps == 1 else jnp.concatenate([mask] * reps, axis=0)
        return jnp.where(m, a, zero)

    tile = pl.program_id(1)
    n_sub = p_ref.shape[0] // CHUNK
    group = ATTN_HEADS // ATTN_KV_HEADS
    k2_cols = slice(C_K2, C_K2 + KV2_W)
    v2_cols = slice(C_V2, C_V2 + KV2_W)

    def kv_window(j):
        if j == 0:
            inner = slice(0, 2 * CHUNK)
            k3 = jnp.concatenate([prev_ref[:, :KV2_W], p_ref[inner, k2_cols]], axis=0)
            v3 = jnp.concatenate([prev_ref[:, KV2_W:], p_ref[inner, v2_cols]], axis=0)
            bias = bias_ref[jnp.where(tile == 0, 0, 1)]
        elif j == n_sub - 1:
            inner = slice((j - 1) * CHUNK, (j + 1) * CHUNK)
            k3 = jnp.concatenate([p_ref[inner, k2_cols], next_ref[:, :KV2_W]], axis=0)
            v3 = jnp.concatenate([p_ref[inner, v2_cols], next_ref[:, KV2_W:]], axis=0)
            bias = bias_ref[jnp.where(tile == pl.num_programs(1) - 1, 2, 1)]
        else:
            inner = slice((j - 1) * CHUNK, (j + 2) * CHUNK)
            k3 = p_ref[inner, k2_cols]
            v3 = p_ref[inner, v2_cols]
            bias = bias_ref[1]
        return k3, v3, bias

    def attn_scores(j, g):
        k3, v3, bias = kv_window(j)
        g_low = slice(g * LANES, (g + 1) * LANES)
        g_high = slice((1 - g) * LANES, (2 - g) * LANES)
        kg = jnp.where(jnp.concatenate([lo] * 3, axis=0), k3[:, g_low], k3[:, g_high])
        v_lo = keep(lo, v3[:, g_low])
        v_hi = keep(hi, v3[:, g_high])
        qs = []
        for i in range(group):
            h = g * group + i
            qp = p_ref[j * CHUNK:(j + 1) * CHUNK,
                       C_AQ + (h // 2) * LANES:C_AQ + (h // 2 + 1) * LANES]
            qs.append(keep(lo if h % 2 == 0 else hi, qp))
        s_all = _dot_nt(jnp.concatenate(qs, axis=0), kg)
        return s_all, v_lo, v_hi, bias

    def attn_finish(j, g, s_all, v_lo, v_hi, bias):
        ps, rs = {}, {}

        def softmax_head(i):
            h = g * group + i
            s = s_all[i * CHUNK:(i + 1) * CHUNK] + bias
            sink = sink_ref[h] * LOG2_E
            m = jnp.maximum(jnp.max(s, axis=-1, keepdims=True), sink)
            e = jnp.exp2(s - m)
            den = jnp.sum(e, axis=-1, keepdims=True) + jnp.exp2(sink - m)
            ps[i] = e.astype(BF16)
            rs[i] = 1.0 / den

        softmax_head(0)
        softmax_head(2)
        o_lo = _dot(jnp.concatenate([ps[0], ps[2]], axis=0), v_lo)
        softmax_head(1)
        softmax_head(3)
        o_hi = _dot(jnp.concatenate([ps[1], ps[3]], axis=0), v_hi)
        for pp in range(group // 2):
            part = slice(pp * CHUNK, (pp + 1) * CHUNK)
            pair = g * (group // 2) + pp
            y_ref[j * CHUNK:(j + 1) * CHUNK, pair * LANES:(pair + 1) * LANES] = (
                o_lo[part] * rs[2 * pp] + o_hi[part] * rs[2 * pp + 1]).astype(BF16)

    def retention_pair(j, p):
        rows = slice(j * CHUNK, (j + 1) * CHUNK)
        sl = slice(p * LANES, (p + 1) * LANES)
        col = lambda c0: slice(c0 + p * LANES, c0 + (p + 1) * LANES)
        qp, kp, vp = p_ref[rows, col(C_RQ)], p_ref[rows, col(C_RK)], p_ref[rows, col(C_RV)]
        s2 = _dot_nt(jnp.concatenate([keep(lo, qp), keep(hi, qp)], axis=0), kp)
        p0 = (s2[:CHUNK] * dm_ref[2 * p]).astype(BF16)
        p1 = (s2[CHUNK:] * dm_ref[2 * p + 1]).astype(BF16)
        qf = (qp.astype(F32) * xif_ref[:, sl]).astype(BF16)
        qb = (qp.astype(F32) * xib_ref[:, sl]).astype(BF16)
        lhs = jnp.concatenate([p0, p1, qf, qb], axis=1)
        rhs = jnp.concatenate([keep(lo, vp), keep(hi, vp), rf_ref[j, p], rb_ref[j, p]], axis=0)
        ret_ref[rows, sl] = _dot(lhs, rhs)

    units = [(j, g) for j in range(n_sub) for g in range(ATTN_KV_HEADS)]
    pairs_per_unit = PAIRS // ATTN_KV_HEADS
    pending = attn_scores(*units[0])
    for u, (j, g) in enumerate(units):
        following = attn_scores(*units[u + 1]) if u + 1 < len(units) else None
        for p in range(g * pairs_per_unit, (g + 1) * pairs_per_unit):
            retention_pair(j, p)
        attn_finish(j, g, *pending)
        pending = following

    ret = ret_ref[...]
    ss = _dot((ret * ret).astype(BF16), bd_ref[...])
    ret_n = ret * lax.rsqrt(ss * (1.0 / HEAD_DIM) + EPS) * rnw_ref[...]
    gate = p_ref[:, C_RG:C_RG + RET_W].astype(F32)
    y_ref[:, ATTN_W:] = (gate * (1.0 / (1.0 + jnp.exp(-gate))) * ret_n).astype(BF16)
    o_ref[...] = x_ref[...] + _dot(y_ref[...], wo_ref[...])


def _mix_call(sink, proj3, rf, rb, x3, w_out, bias, dmat, xif, xib, rnw, bd):
    batch, seq, d_model = x3.shape
    tm = MIX_TM
    n_sub = tm // CHUNK
    nt = seq // tm
    nc = seq // CHUNK
    kv_cols = C_K2 // (2 * KV2_W)
    assert C_K2 % (2 * KV2_W) == 0 and C_V2 == C_K2 + KV2_W
    kv_prev = pl.BlockSpec(
        (None, CHUNK, 2 * KV2_W), lambda b, t: (b, jnp.maximum(t * n_sub - 1, 0), kv_cols))
    kv_next = pl.BlockSpec(
        (None, CHUNK, 2 * KV2_W), lambda b, t: (b, jnp.minimum((t + 1) * n_sub, nc - 1), kv_cols))
    st = pl.BlockSpec((None, n_sub, PAIRS, LANES, LANES), lambda b, t: (b, t, 0, 0, 0))
    return pl.pallas_call(
        _mix_kernel,
        grid=(batch, nt),
        in_specs=[
            pl.BlockSpec(memory_space=pltpu.SMEM),
            pl.BlockSpec((None, tm, PROJ_W), lambda b, t: (b, t, 0)),
            kv_prev, kv_next,
            st, st,
            pl.BlockSpec((None, tm, d_model), lambda b, t: (b, t, 0)),
            _resident((d_model, d_model)),
            _resident((3, CHUNK, 3 * CHUNK)),
            _resident((RET_HEADS, CHUNK, CHUNK)),
            _resident((CHUNK, RET_W)),
            _resident((CHUNK, RET_W)),
            _resident((1, RET_W)),
            _resident((RET_W, RET_W)),
        ],
        out_specs=pl.BlockSpec((None, tm, d_model), lambda b, t: (b, t, 0)),
        out_shape=jax.ShapeDtypeStruct((batch, seq, d_model), F32),
        scratch_shapes=[pltpu.VMEM((tm, d_model), BF16), pltpu.VMEM((tm, RET_W), F32)],
        compiler_params=pltpu.CompilerParams(
            dimension_semantics=("arbitrary", "arbitrary"), vmem_limit_bytes=VMEM_LIMIT),
        name="mix",
    )(sink, proj3, proj3, proj3, rf, rb, x3, w_out, bias, dmat, xif, xib, rnw, bd)


def _ffn_kernel(h_ref, fnw_ref, wg_ref, wu_ref, wd_ref, o_ref, a_ref):
    h = h_ref[...]
    ms = jnp.mean(h * h, axis=-1, keepdims=True)
    m = (h * lax.rsqrt(ms + EPS) * fnw_ref[...]).astype(BF16)
    d_ff = wg_ref.shape[1]
    for c0 in range(0, d_ff, FFN_CK):
        c1 = min(c0 + FFN_CK, d_ff)
        g = _dot(m, wg_ref[:, c0:c1])
        u = _dot(m, wu_ref[:, c0:c1])
        a_ref[:, c0:c1] = (g * (1.0 / (1.0 + jnp.exp(-g))) * u).astype(BF16)
    o_ref[...] = h + _dot(a_ref[...], wd_ref[...])


def _ffn_call(h2, fnw, wg, wu, wd):
    tokens, d_model = h2.shape
    d_ff = wg.shape[1]
    tm = FFN_TM
    row = lambda i: (i, 0)
    return pl.pallas_call(
        _ffn_kernel,
        grid=(tokens // tm,),
        in_specs=[
            pl.BlockSpec((tm, d_model), row),
            _resident((1, d_model)),
            _resident((d_model, d_ff)),
            _resident((d_model, d_ff)),
            _resident((d_ff, d_model)),
        ],
        out_specs=pl.BlockSpec((tm, d_model), row),
        out_shape=jax.ShapeDtypeStruct((tokens, d_model), F32),
        scratch_shapes=[pltpu.VMEM((tm, d_ff), BF16)],
        compiler_params=pltpu.CompilerParams(
            dimension_semantics=("arbitrary",), vmem_limit_bytes=VMEM_LIMIT),
        name="ffn",
    )(h2, fnw, wg, wu, wd)


def _rope_tables(seq):
    inv_freq = ROPE_THETA ** (-jnp.arange(0, HEAD_DIM, 2, dtype=F32) / HEAD_DIM)
    ang = jnp.arange(seq, dtype=F32)[:, None] * inv_freq[None, :]
    cos, sin = jnp.cos(ang), jnp.sin(ang)
    zeros = jnp.zeros_like(sin)
    reps = LANES // HEAD_DIM
    cos_t = jnp.tile(jnp.concatenate([cos, cos], -1), (1, reps))
    sina_t = jnp.tile(jnp.concatenate([-sin, zeros], -1), (1, reps))
    sinb_t = jnp.tile(jnp.concatenate([zeros, sin], -1), (1, reps))
    return cos_t, sina_t, sinb_t


def _attn_bias():
    i = np.arange(CHUNK)[:, None]
    j = np.arange(3 * CHUNK)[None, :]
    band = np.abs(i + CHUNK - j) <= CHUNK
    first = band & (j >= CHUNK)
    last = band & (j < 2 * CHUNK)
    tab = np.stack([first, band, last]).astype(np.float32)
    return jnp.asarray(np.where(tab > 0, 0.0, NEG_INF).astype(np.float32))


def _retention_tables(log_f, log_b):
    idx = jnp.arange(CHUNK, dtype=F32)
    diff = idx[:, None] - idx[None, :]
    lf, lb = log_f[:, None, None], log_b[:, None, None]
    dmat = jnp.where(diff[None] >= 0,
                     jnp.exp(lf * jnp.maximum(diff, 0.0)[None]),
                     jnp.exp(lb * jnp.maximum(-diff, 0.0)[None]))
    per_lane = lambda t: jnp.repeat(t.T, HEAD_DIM, axis=1)
    xif = per_lane(jnp.exp(log_f[:, None] * (idx + 1.0)[None]))
    xib = per_lane(jnp.exp(log_b[:, None] * (CHUNK - idx)[None]))
    zf = per_lane(jnp.exp(log_f[:, None] * (CHUNK - 1.0 - idx)[None]))
    zb = per_lane(jnp.exp(log_b[:, None] * idx[None]))
    per_row = lambda g: jnp.broadcast_to(
        jnp.repeat(g.reshape(PAIRS, 2), HEAD_DIM, axis=1)[:, :, None], (PAIRS, LANES, LANES))
    gf = per_row(jnp.exp(log_f * CHUNK))
    gb = per_row(jnp.exp(log_b * CHUNK))
    return dmat, xif, xib, zf, zb, gf, gb


def _block_diag_ones(width, dtype):
    r = np.arange(width) // HEAD_DIM
    return jnp.asarray((r[:, None] == r[None, :]).astype(np.float32), dtype=dtype)


def kernel(x, attn_norm_w, w_in, q_norm_w, k_norm_w, attn_sink, ret_log_decay_fwd,
           ret_log_decay_bwd, ret_norm_w, w_out, ffn_norm_w, w_gate, w_up, w_down):
    batch, seq, d_model = x.shape
    depth = w_in.shape[0]
    assert seq % PROJ_TM == 0 and seq % MIX_TM == 0 and (batch * seq) % FFN_TM == 0
    assert MIX_TM // CHUNK >= 2

    cos_t, sina_t, sinb_t = _rope_tables(seq)
    bias = _attn_bias()
    bd = _block_diag_ones(RET_W, BF16)
    bdm = _block_diag_ones(LANES, F32)

    h = x
    for l in range(depth):
        qnw = jnp.tile(q_norm_w[l], ATTN_HEADS)[None, :]
        knw = jnp.tile(k_norm_w[l], ATTN_KV_HEADS)[None, :]
        log_f = -jnp.abs(ret_log_decay_fwd[l].astype(F32))
        log_b = -jnp.abs(ret_log_decay_bwd[l].astype(F32))
        dmat, xif, xib, zf, zb, gf, gb = _retention_tables(log_f, log_b)

        proj = _proj_call(h.reshape(batch * seq, d_model), attn_norm_w[l][None, :],
                          w_in[l].astype(BF16), qnw, knw, cos_t, sina_t, sinb_t, bd, seq)
        proj3 = proj.reshape(batch, seq, PROJ_W)
        rf, rb = _state_call(proj3, zf, zb, gf, gb, bdm)
        h = _mix_call(attn_sink[l].astype(F32), proj3, rf, rb, h, w_out[l].astype(BF16),
                      bias, dmat, xif, xib, ret_norm_w[l][None, :], bd)
        h = _ffn_call(h.reshape(batch * seq, d_model), ffn_norm_w[l][None, :],
                      w_gate[l].astype(BF16), w_up[l].astype(BF16),
                      w_down[l].astype(BF16)).reshape(batch, seq, d_model)
    return h
```

```python
import jax
import jax.numpy as jnp
import numpy as np
from jax import lax
from jax.experimental import pallas as pl
from jax.experimental.pallas import tpu as pltpu

HEAD_DIM = 64
ATTN_HEADS = 8
ATTN_KV_HEADS = 2
RET_HEADS = 8
CHUNK = 128
ROPE_THETA = 10000.0
EPS = 1e-6
NEG_INF = -1e30
LOG2_E = 1.4426950408889634

LANES = 128
PAIRS = RET_HEADS // 2
ATTN_W = ATTN_HEADS * HEAD_DIM
RET_W = RET_HEADS * HEAD_DIM
KV2_W = 2 * ATTN_KV_HEADS * HEAD_DIM

C_AQ = 0
C_K2 = C_AQ + ATTN_W
C_V2 = C_K2 + KV2_W
C_RK = C_V2 + KV2_W
C_RV = C_RK + RET_W
C_RQ = C_RV + RET_W
C_RG = C_RQ + RET_W
PROJ_W = C_RG + RET_W

PROJ_TM = 512
MIX_TM = 512
RET_PAIRS_PER_UNIT = 2
STATE_CH = 2
FFN_TM = 1024
FFN_CK = 512
VMEM_LIMIT = 56 * 1024 * 1024

BF16 = jnp.bfloat16
F32 = jnp.float32


def _dot(a, b):
    return jnp.dot(a, b, preferred_element_type=F32)


def _dot_nt(a, b):
    return lax.dot_general(a, b, (((1,), (1,)), ((), ())), preferred_element_type=F32)


def _dot_tn(a, b):
    return lax.dot_general(a, b, (((0,), (0,)), ((), ())), preferred_element_type=F32)


def _resident(shape):
    zeros = (0,) * len(shape)
    return pl.BlockSpec(shape, lambda *_: zeros, pipeline_mode=pl.Buffered(1))


def _proj_kernel(x_ref, anw_ref, w_ref, qnw_ref, knw_ref, cos_ref, sina_ref, sinb_ref,
                 bd_ref, o_ref):
    x = x_ref[...]
    ms = jnp.mean(x * x, axis=-1, keepdims=True)
    n = (x * lax.rsqrt(ms + EPS) * anw_ref[...]).astype(BF16)
    cos = cos_ref[...]
    sina = sina_ref[...]
    sinb = sinb_ref[...]

    def proj(c0, width):
        return _dot(n, w_ref[:, c0:c0 + width])

    def head_norm(y, w):
        width = y.shape[-1]
        ss = _dot((y * y).astype(BF16), bd_ref[:width, :width])
        return y * lax.rsqrt(ss * (1.0 / HEAD_DIM) + EPS) * w

    def rope(ys):
        return ys * cos + pltpu.roll(ys, LANES - 32, 1) * sina + pltpu.roll(ys, 32, 1) * sinb

    def rope_store(y, c0, scale):
        for s in range(y.shape[-1] // LANES):
            r = rope(y[:, s * LANES:(s + 1) * LANES])
            if scale != 1.0:
                r = r * scale
            o_ref[:, c0 + s * LANES:c0 + (s + 1) * LANES] = r.astype(BF16)

    def plain_store(y, c0):
        o_ref[:, c0:c0 + y.shape[-1]] = y.astype(BF16)

    def store_both_orders(y, c0):
        o_ref[:, c0:c0 + LANES] = y.astype(BF16)
        o_ref[:, c0 + LANES:c0 + 2 * LANES] = pltpu.roll(y, HEAD_DIM, 1).astype(BF16)

    def kv_finish(y):
        store_both_orders(rope(head_norm(y[:, :LANES], knw_ref[...])), C_K2)
        store_both_orders(y[:, LANES:], C_V2)

    scale = HEAD_DIM ** -0.5
    kvw = ATTN_KV_HEADS * HEAD_DIM
    in_kv = ATTN_W
    in_rq = in_kv + 2 * kvw
    stages = [
        (0, ATTN_W, lambda y: rope_store(head_norm(y, qnw_ref[...]), C_AQ, scale * LOG2_E)),
        (in_kv, 2 * kvw, kv_finish),
        (in_rq, RET_W, lambda y: rope_store(y, C_RQ, 1.0)),
        (in_rq + RET_W, RET_W, lambda y: rope_store(y, C_RK, scale)),
        (in_rq + 2 * RET_W, RET_W, lambda y: plain_store(y, C_RV)),
        (in_rq + 3 * RET_W, RET_W, lambda y: plain_store(y, C_RG)),
    ]
    pending = proj(stages[0][0], stages[0][1])
    for i, (_, _, finish) in enumerate(stages):
        following = proj(stages[i + 1][0], stages[i + 1][1]) if i + 1 < len(stages) else None
        finish(pending)
        pending = following


def _proj_call(x2, anw, w_in, qnw, knw, cos, sina, sinb, bd, seq):
    tokens, d_model = x2.shape
    tm = PROJ_TM
    pos_blocks = seq // tm
    row = lambda i: (i, 0)
    pos = lambda i: (i % pos_blocks, 0)
    return pl.pallas_call(
        _proj_kernel,
        grid=(tokens // tm,),
        in_specs=[
            pl.BlockSpec((tm, d_model), row),
            _resident((1, d_model)),
            _resident(w_in.shape),
            _resident((1, ATTN_W)),
            _resident((1, LANES)),
            pl.BlockSpec((tm, LANES), pos),
            pl.BlockSpec((tm, LANES), pos),
            pl.BlockSpec((tm, LANES), pos),
            _resident((ATTN_W, ATTN_W)),
        ],
        out_specs=pl.BlockSpec((tm, PROJ_W), row),
        out_shape=jax.ShapeDtypeStruct((tokens, PROJ_W), BF16),
        compiler_params=pltpu.CompilerParams(
            dimension_semantics=("arbitrary",), vmem_limit_bytes=VMEM_LIMIT),
        name="proj",
    )(x2, anw, w_in, qnw, knw, cos, sina, sinb, bd)


def _state_kernel(kvf_ref, kvb_ref, zf_ref, zb_ref, gf_ref, gb_ref, bdm_ref,
                  rf_ref, rb_ref, cf_ref, cb_ref):
    @pl.when(pl.program_id(0) == 0)
    def _():
        cf_ref[...] = jnp.zeros_like(cf_ref)
        cb_ref[...] = jnp.zeros_like(cb_ref)

    bdm = bdm_ref[...]
    batch = kvf_ref.shape[0]
    fwd_order = list(range(STATE_CH))
    for kv_ref, z_ref, g_ref, r_ref, c_ref, order in (
            (kvf_ref, zf_ref, gf_ref, rf_ref, cf_ref, fwd_order),
            (kvb_ref, zb_ref, gb_ref, rb_ref, cb_ref, fwd_order[::-1])):
        for b in range(batch):
            for p in range(PAIRS):
                sl = slice(p * LANES, (p + 1) * LANES)
                vsl = slice(RET_W + p * LANES, RET_W + (p + 1) * LANES)
                carry = c_ref[b, p]
                for c in order:
                    rows = slice(c * CHUNK, (c + 1) * CHUNK)
                    kz = (kv_ref[b, rows, sl].astype(F32) * z_ref[:, sl]).astype(BF16)
                    kv = _dot_tn(kz, kv_ref[b, rows, vsl])
                    r_ref[b, c, p] = carry.astype(BF16)
                    carry = g_ref[p] * carry + kv * bdm
                c_ref[b, p] = carry


def _state_call(proj3, zf, zb, gf, gb, bdm):
    batch, seq, _ = proj3.shape
    ns = seq // (CHUNK * STATE_CH)
    assert C_RK % (2 * RET_W) == 0 and C_RV == C_RK + RET_W
    kv_col = C_RK // (2 * RET_W)
    blk = (batch, CHUNK * STATE_CH, 2 * RET_W)
    out_blk = (batch, STATE_CH, PAIRS, LANES, LANES)
    out_sds = jax.ShapeDtypeStruct((batch, seq // CHUNK, PAIRS, LANES, LANES), BF16)
    return pl.pallas_call(
        _state_kernel,
        grid=(ns,),
        in_specs=[
            pl.BlockSpec(blk, lambda s: (0, s, kv_col)),
            pl.BlockSpec(blk, lambda s: (0, ns - 1 - s, kv_col)),
            _resident((CHUNK, RET_W)),
            _resident((CHUNK, RET_W)),
            _resident((PAIRS, LANES, LANES)),
            _resident((PAIRS, LANES, LANES)),
            _resident((LANES, LANES)),
        ],
        out_specs=[
            pl.BlockSpec(out_blk, lambda s: (0, s, 0, 0, 0)),
            pl.BlockSpec(out_blk, lambda s: (0, ns - 1 - s, 0, 0, 0)),
        ],
        out_shape=[out_sds, out_sds],
        scratch_shapes=[pltpu.VMEM((batch, PAIRS, LANES, LANES), F32),
                        pltpu.VMEM((batch, PAIRS, LANES, LANES), F32)],
        compiler_params=pltpu.CompilerParams(
            dimension_semantics=("arbitrary",), vmem_limit_bytes=VMEM_LIMIT),
        name="ret_state",
    )(proj3, proj3, zf, zb, gf, gb, bdm)


def _mix_kernel(sink_ref, p_ref, prev_ref, next_ref, rf_ref, rb_ref, x_ref, wo_ref,
                bias_ref, dm_ref, xif_ref, xib_ref, rnw_ref, bd_ref, o_ref, y_ref, ret_ref):
    lane = lax.broadcasted_iota(jnp.int32, (CHUNK, LANES), 1).astype(F32).astype(BF16)
    lo = lane < HEAD_DIM
    hi = jnp.logical_not(lo)
    zero = jnp.zeros((), BF16)

    def keep(mask, a):
        reps = a.shape[0] // CHUNK
        m = mask if reps == 1 else jnp.concatenate([mask] * reps, axis=0)
        return jnp.where(m, a, zero)

    tile = pl.program_id(1)
    n_sub = p_ref.shape[0] // CHUNK
    group = ATTN_HEADS // ATTN_KV_HEADS
    k2_cols = slice(C_K2, C_K2 + KV2_W)
    v2_cols = slice(C_V2, C_V2 + KV2_W)

    def kv_window(j):
        if j == 0:
            inner = slice(0, 2 * CHUNK)
            k3 = jnp.concatenate([prev_ref[:, :KV2_W], p_ref[inner, k2_cols]], axis=0)
            v3 = jnp.concatenate([prev_ref[:, KV2_W:], p_ref[inner, v2_cols]], axis=0)
            bias = bias_ref[jnp.where(tile == 0, 0, 1)]
        elif j == n_sub - 1:
            inner = slice((j - 1) * CHUNK, (j + 1) * CHUNK)
            k3 = jnp.concatenate([p_ref[inner, k2_cols], next_ref[:, :KV2_W]], axis=0)
            v3 = jnp.concatenate([p_ref[inner, v2_cols], next_ref[:, KV2_W:]], axis=0)
            bias = bias_ref[jnp.where(tile == pl.num_programs(1) - 1, 2, 1)]
        else:
            inner = slice((j - 1) * CHUNK, (j + 2) * CHUNK)
            k3 = p_ref[inner, k2_cols]
            v3 = p_ref[inner, v2_cols]
            bias = bias_ref[1]
        return k3, v3, bias

    def attn_scores(j, g):
        k3, v3, bias = kv_window(j)
        g_low = slice(g * LANES, (g + 1) * LANES)
        g_high = slice((1 - g) * LANES, (2 - g) * LANES)
        kg = jnp.where(jnp.concatenate([lo] * 3, axis=0), k3[:, g_low], k3[:, g_high])
        v_lo = keep(lo, v3[:, g_low])
        v_hi = keep(hi, v3[:, g_high])
        qs = []
        for i in range(group):
            h = g * group + i
            qp = p_ref[j * CHUNK:(j + 1) * CHUNK,
                       C_AQ + (h // 2) * LANES:C_AQ + (h // 2 + 1) * LANES]
            qs.append(keep(lo if h % 2 == 0 else hi, qp))
        s_all = _dot_nt(jnp.concatenate(qs, axis=0), kg)
        return s_all, v_lo, v_hi, bias

    def attn_finish(j, g, s_all, v_lo, v_hi, bias):
        ps, rs = {}, {}

        def softmax_head(i):
            h = g * group + i
            s = s_all[i * CHUNK:(i + 1) * CHUNK] + bias
            sink = sink_ref[h] * LOG2_E
            m = jnp.maximum(jnp.max(s, axis=-1, keepdims=True), sink)
            e = jnp.exp2(s - m)
            den = jnp.sum(e, axis=-1, keepdims=True) + jnp.exp2(sink - m)
            ps[i] = e.astype(BF16)
            rs[i] = 1.0 / den

        softmax_head(0)
        softmax_head(2)
        o_lo = _dot(jnp.concatenate([ps[0], ps[2]], axis=0), v_lo)
        softmax_head(1)
        softmax_head(3)
        o_hi = _dot(jnp.concatenate([ps[1], ps[3]], axis=0), v_hi)
        for pp in range(group // 2):
            part = slice(pp * CHUNK, (pp + 1) * CHUNK)
            pair = g * (group // 2) + pp
            y_ref[j * CHUNK:(j + 1) * CHUNK, pair * LANES:(pair + 1) * LANES] = (
                o_lo[part] * rs[2 * pp] + o_hi[part] * rs[2 * pp + 1]).astype(BF16)

    def retention_pair(j, p):
        rows = slice(j * CHUNK, (j + 1) * CHUNK)
        sl = slice(p * LANES, (p + 1) * LANES)
        col = lambda c0: slice(c0 + p * LANES, c0 + (p + 1) * LANES)
        qp, kp, vp = p_ref[rows, col(C_RQ)], p_ref[rows, col(C_RK)], p_ref[rows, col(C_RV)]
        s2 = _dot_nt(jnp.concatenate([keep(lo, qp), keep(hi, qp)], axis=0), kp)
        p0 = (s2[:CHUNK] * dm_ref[2 * p]).astype(BF16)
        p1 = (s2[CHUNK:] * dm_ref[2 * p + 1]).astype(BF16)
        qf = (qp.astype(F32) * xif_ref[:, sl]).astype(BF16)
        qb = (qp.astype(F32) * xib_ref[:, sl]).astype(BF16)
        lhs = jnp.concatenate([p0, p1, qf, qb], axis=1)
        rhs = jnp.concatenate([keep(lo, vp), keep(hi, vp), rf_ref[j, p], rb_ref[j, p]], axis=0)
        ret_ref[rows, sl] = _dot(lhs, rhs)

    def retention_out():
        ret = ret_ref[...]
        ss = _dot((ret * ret).astype(BF16), bd_ref[...])
        ret_n = ret * lax.rsqrt(ss * (1.0 / HEAD_DIM) + EPS) * rnw_ref[...]
        gate = p_ref[:, C_RG:C_RG + RET_W].astype(F32)
        y_ref[:, ATTN_W:] = (gate * (1.0 / (1.0 + jnp.exp(-gate))) * ret_n).astype(BF16)

    units = [(j, g) for j in range(n_sub) for g in range(ATTN_KV_HEADS)]
    ret_items = [(j, p) for j in range(n_sub) for p in range(PAIRS)]
    assert len(units) * RET_PAIRS_PER_UNIT >= len(ret_items)
    pending = attn_scores(*units[0])
    for u, (j, g) in enumerate(units):
        following = attn_scores(*units[u + 1]) if u + 1 < len(units) else None
        for item in ret_items[u * RET_PAIRS_PER_UNIT:(u + 1) * RET_PAIRS_PER_UNIT]:
            retention_pair(*item)
        attn_finish(j, g, *pending)
        pending = following
    retention_out()
    o_ref[...] = x_ref[...] + _dot(y_ref[...], wo_ref[...])


def _mix_call(sink, proj3, rf, rb, x3, w_out, bias, dmat, xif, xib, rnw, bd):
    batch, seq, d_model = x3.shape
    tm = MIX_TM
    n_sub = tm // CHUNK
    nt = seq // tm
    nc = seq // CHUNK
    kv_cols = C_K2 // (2 * KV2_W)
    assert C_K2 % (2 * KV2_W) == 0 and C_V2 == C_K2 + KV2_W
    kv_prev = pl.BlockSpec(
        (None, CHUNK, 2 * KV2_W), lambda b, t: (b, jnp.maximum(t * n_sub - 1, 0), kv_cols))
    kv_next = pl.BlockSpec(
        (None, CHUNK, 2 * KV2_W), lambda b, t: (b, jnp.minimum((t + 1) * n_sub, nc - 1), kv_cols))
    st = pl.BlockSpec((None, n_sub, PAIRS, LANES, LANES), lambda b, t: (b, t, 0, 0, 0))
    return pl.pallas_call(
        _mix_kernel,
        grid=(batch, nt),
        in_specs=[
            pl.BlockSpec(memory_space=pltpu.SMEM),
            pl.BlockSpec((None, tm, PROJ_W), lambda b, t: (b, t, 0)),
            kv_prev, kv_next,
            st, st,
            pl.BlockSpec((None, tm, d_model), lambda b, t: (b, t, 0)),
            _resident((d_model, d_model)),
            _resident((3, CHUNK, 3 * CHUNK)),
            _resident((RET_HEADS, CHUNK, CHUNK)),
            _resident((CHUNK, RET_W)),
            _resident((CHUNK, RET_W)),
            _resident((1, RET_W)),
            _resident((RET_W, RET_W)),
        ],
        out_specs=pl.BlockSpec((None, tm, d_model), lambda b, t: (b, t, 0)),
        out_shape=jax.ShapeDtypeStruct((batch, seq, d_model), F32),
        scratch_shapes=[pltpu.VMEM((tm, d_model), BF16), pltpu.VMEM((tm, RET_W), F32)],
        compiler_params=pltpu.CompilerParams(
            dimension_semantics=("arbitrary", "arbitrary"), vmem_limit_bytes=VMEM_LIMIT),
        name="mix",
    )(sink, proj3, proj3, proj3, rf, rb, x3, w_out, bias, dmat, xif, xib, rnw, bd)


def _ffn_kernel(h_ref, fnw_ref, wg_ref, wu_ref, wd_ref, o_ref, a_ref):
    h = h_ref[...]
    ms = jnp.mean(h * h, axis=-1, keepdims=True)
    m = (h * lax.rsqrt(ms + EPS) * fnw_ref[...]).astype(BF16)
    d_ff = wg_ref.shape[1]
    for c0 in range(0, d_ff, FFN_CK):
        c1 = min(c0 + FFN_CK, d_ff)
        g = _dot(m, wg_ref[:, c0:c1])
        u = _dot(m, wu_ref[:, c0:c1])
        a_ref[:, c0:c1] = (g * (1.0 / (1.0 + jnp.exp(-g))) * u).astype(BF16)
    o_ref[...] = h + _dot(a_ref[...], wd_ref[...])


def _ffn_call(h2, fnw, wg, wu, wd):
    tokens, d_model = h2.shape
    d_ff = wg.shape[1]
    tm = FFN_TM
    row = lambda i: (i, 0)
    return pl.pallas_call(
        _ffn_kernel,
        grid=(tokens // tm,),
        in_specs=[
            pl.BlockSpec((tm, d_model), row),
            _resident((1, d_model)),
            _resident((d_model, d_ff)),
            _resident((d_model, d_ff)),
            _resident((d_ff, d_model)),
        ],
        out_specs=pl.BlockSpec((tm, d_model), row),
        out_shape=jax.ShapeDtypeStruct((tokens, d_model), F32),
        scratch_shapes=[pltpu.VMEM((tm, d_ff), BF16)],
        compiler_params=pltpu.CompilerParams(
            dimension_semantics=("arbitrary",), vmem_limit_bytes=VMEM_LIMIT),
        name="ffn",
    )(h2, fnw, wg, wu, wd)


def _rope_tables(seq):
    inv_freq = ROPE_THETA ** (-jnp.arange(0, HEAD_DIM, 2, dtype=F32) / HEAD_DIM)
    ang = jnp.arange(seq, dtype=F32)[:, None] * inv_freq[None, :]
    cos, sin = jnp.cos(ang), jnp.sin(ang)
    zeros = jnp.zeros_like(sin)
    reps = LANES // HEAD_DIM
    cos_t = jnp.tile(jnp.concatenate([cos, cos], -1), (1, reps))
    sina_t = jnp.tile(jnp.concatenate([-sin, zeros], -1), (1, reps))
    sinb_t = jnp.tile(jnp.concatenate([zeros, sin], -1), (1, reps))
    return cos_t, sina_t, sinb_t


def _attn_bias():
    i = np.arange(CHUNK)[:, None]
    j = np.arange(3 * CHUNK)[None, :]
    band = np.abs(i + CHUNK - j) <= CHUNK
    first = band & (j >= CHUNK)
    last = band & (j < 2 * CHUNK)
    tab = np.stack([first, band, last]).astype(np.float32)
    return jnp.asarray(np.where(tab > 0, 0.0, NEG_INF).astype(np.float32))


def _retention_tables(log_f, log_b):
    idx = jnp.arange(CHUNK, dtype=F32)
    diff = idx[:, None] - idx[None, :]
    lf, lb = log_f[:, None, None], log_b[:, None, None]
    dmat = jnp.where(diff[None] >= 0,
                     jnp.exp(lf * jnp.maximum(diff, 0.0)[None]),
                     jnp.exp(lb * jnp.maximum(-diff, 0.0)[None]))
    per_lane = lambda t: jnp.repeat(t.T, HEAD_DIM, axis=1)
    xif = per_lane(jnp.exp(log_f[:, None] * (idx + 1.0)[None]))
    xib = per_lane(jnp.exp(log_b[:, None] * (CHUNK - idx)[None]))
    zf = per_lane(jnp.exp(log_f[:, None] * (CHUNK - 1.0 - idx)[None]))
    zb = per_lane(jnp.exp(log_b[:, None] * idx[None]))
    per_row = lambda g: jnp.broadcast_to(
        jnp.repeat(g.reshape(PAIRS, 2), HEAD_DIM, axis=1)[:, :, None], (PAIRS, LANES, LANES))
    gf = per_row(jnp.exp(log_f * CHUNK))
    gb = per_row(jnp.exp(log_b * CHUNK))
    return dmat, xif, xib, zf, zb, gf, gb


def _block_diag_ones(width, dtype):
    r = np.arange(width) // HEAD_DIM
    return jnp.asarray((r[:, None] == r[None, :]).astype(np.float32), dtype=dtype)


def kernel(x, attn_norm_w, w_in, q_norm_w, k_norm_w, attn_sink, ret_log_decay_fwd,
           ret_log_decay_bwd, ret_norm_w, w_out, ffn_norm_w, w_gate, w_up, w_down):
    batch, seq, d_model = x.shape
    depth = w_in.shape[0]
    assert seq % PROJ_TM == 0 and seq % MIX_TM == 0 and (batch * seq) % FFN_TM == 0
    assert MIX_TM // CHUNK >= 2

    cos_t, sina_t, sinb_t = _rope_tables(seq)
    bias = _attn_bias()
    bd = _block_diag_ones(RET_W, BF16)
    bdm = _block_diag_ones(LANES, F32)

    h = x
    for l in range(depth):
        qnw = jnp.tile(q_norm_w[l], ATTN_HEADS)[None, :]
        knw = jnp.tile(k_norm_w[l], ATTN_KV_HEADS)[None, :]
        log_f = -jnp.abs(ret_log_decay_fwd[l].astype(F32))
        log_b = -jnp.abs(ret_log_decay_bwd[l].astype(F32))
        dmat, xif, xib, zf, zb, gf, gb = _retention_tables(log_f, log_b)

        proj = _proj_call(h.reshape(batch * seq, d_model), attn_norm_w[l][None, :],
                          w_in[l].astype(BF16), qnw, knw, cos_t, sina_t, sinb_t, bd, seq)
        proj3 = proj.reshape(batch, seq, PROJ_W)
        rf, rb = _state_call(proj3, zf, zb, gf, gb, bdm)
        h = _mix_call(attn_sink[l].astype(F32), proj3, rf, rb, h, w_out[l].astype(BF16),
                      bias, dmat, xif, xib, ret_norm_w[l][None, :], bd)
        h = _ffn_call(h.reshape(batch * seq, d_model), ffn_norm_w[l][None, :],
                      w_gate[l].astype(BF16), w_up[l].astype(BF16),
                      w_down[l].astype(BF16)).reshape(batch, seq, d_model)
    return h
```

```python
import jax
import jax.numpy as jnp
import numpy as np
from jax import lax
from jax.experimental import pallas as pl
from jax.experimental.pallas import tpu as pltpu

HEAD_DIM = 64
ATTN_HEADS = 8
ATTN_KV_HEADS = 2
RET_HEADS = 8
CHUNK = 128
ROPE_THETA = 10000.0
EPS = 1e-6
NEG_INF = -1e30
LOG2_E = 1.4426950408889634

LANES = 128
PAIRS = RET_HEADS // 2
ATTN_W = ATTN_HEADS * HEAD_DIM
RET_W = RET_HEADS * HEAD_DIM
KV2_W = 2 * ATTN_KV_HEADS * HEAD_DIM

C_AQ = 0
C_K2 = C_AQ + ATTN_W
C_V2 = C_K2 + KV2_W
C_RK = C_V2 + KV2_W
C_RV = C_RK + RET_W
C_RQ = C_RV + RET_W
C_RG = C_RQ + RET_W
PROJ_W = C_RG + RET_W

PROJ_TM = 512
MIX_TM = 512
RET_PAIRS_PER_UNIT = 2
STATE_CH = 2
FFN_TM = 1024
FFN_CK = 512
VMEM_LIMIT = 56 * 1024 * 1024

BF16 = jnp.bfloat16
F32 = jnp.float32


def _dot(a, b):
    return jnp.dot(a, b, preferred_element_type=F32)


def _dot_nt(a, b):
    return lax.dot_general(a, b, (((1,), (1,)), ((), ())), preferred_element_type=F32)


def _dot_tn(a, b):
    return lax.dot_general(a, b, (((0,), (0,)), ((), ())), preferred_element_type=F32)


def _resident(shape):
    zeros = (0,) * len(shape)
    return pl.BlockSpec(shape, lambda *_: zeros, pipeline_mode=pl.Buffered(1))


def _proj_kernel(x_ref, anw_ref, w_ref, qnw_ref, knw_ref, cos_ref, sina_ref, sinb_ref,
                 bd_ref, o_ref):
    x = x_ref[...]
    ms = jnp.mean(x * x, axis=-1, keepdims=True)
    n = (x * lax.rsqrt(ms + EPS) * anw_ref[...]).astype(BF16)
    cos = cos_ref[...]
    sina = sina_ref[...]
    sinb = sinb_ref[...]

    def proj(c0, width):
        return _dot(n, w_ref[:, c0:c0 + width])

    def head_norm(y, w):
        width = y.shape[-1]
        ss = _dot((y * y).astype(BF16), bd_ref[:width, :width])
        return y * lax.rsqrt(ss * (1.0 / HEAD_DIM) + EPS) * w

    def rope(ys):
        return ys * cos + pltpu.roll(ys, LANES - 32, 1) * sina + pltpu.roll(ys, 32, 1) * sinb

    def rope_store(y, c0, scale):
        for s in range(y.shape[-1] // LANES):
            r = rope(y[:, s * LANES:(s + 1) * LANES])
            if scale != 1.0:
                r = r * scale
            o_ref[:, c0 + s * LANES:c0 + (s + 1) * LANES] = r.astype(BF16)

    def plain_store(y, c0):
        o_ref[:, c0:c0 + y.shape[-1]] = y.astype(BF16)

    def store_both_orders(y, c0):
        o_ref[:, c0:c0 + LANES] = y.astype(BF16)
        o_ref[:, c0 + LANES:c0 + 2 * LANES] = pltpu.roll(y, HEAD_DIM, 1).astype(BF16)

    def kv_finish(y):
        store_both_orders(rope(head_norm(y[:, :LANES], knw_ref[...])), C_K2)
        store_both_orders(y[:, LANES:], C_V2)

    scale = HEAD_DIM ** -0.5
    kvw = ATTN_KV_HEADS * HEAD_DIM
    in_kv = ATTN_W
    in_rq = in_kv + 2 * kvw
    stages = [
        (0, ATTN_W, lambda y: rope_store(head_norm(y, qnw_ref[...]), C_AQ, scale * LOG2_E)),
        (in_kv, 2 * kvw, kv_finish),
        (in_rq, RET_W, lambda y: rope_store(y, C_RQ, 1.0)),
        (in_rq + RET_W, RET_W, lambda y: rope_store(y, C_RK, scale)),
        (in_rq + 2 * RET_W, RET_W, lambda y: plain_store(y, C_RV)),
        (in_rq + 3 * RET_W, RET_W, lambda y: plain_store(y, C_RG)),
    ]
    pending = proj(stages[0][0], stages[0][1])
    for i, (_, _, finish) in enumerate(stages):
        following = proj(stages[i + 1][0], stages[i + 1][1]) if i + 1 < len(stages) else None
        finish(pending)
        pending = following


def _proj_call(x2, anw, w_in, qnw, knw, cos, sina, sinb, bd, seq):
    tokens, d_model = x2.shape
    tm = PROJ_TM
    pos_blocks = seq // tm
    row = lambda i: (i, 0)
    pos = lambda i: (i % pos_blocks, 0)
    return pl.pallas_call(
        _proj_kernel,
        grid=(tokens // tm,),
        in_specs=[
            pl.BlockSpec((tm, d_model), row),
            _resident((1, d_model)),
            _resident(w_in.shape),
            _resident((1, ATTN_W)),
            _resident((1, LANES)),
            pl.BlockSpec((tm, LANES), pos),
            pl.BlockSpec((tm, LANES), pos),
            pl.BlockSpec((tm, LANES), pos),
            _resident((ATTN_W, ATTN_W)),
        ],
        out_specs=pl.BlockSpec((tm, PROJ_W), row),
        out_shape=jax.ShapeDtypeStruct((tokens, PROJ_W), BF16),
        compiler_params=pltpu.CompilerParams(
            dimension_semantics=("arbitrary",), vmem_limit_bytes=VMEM_LIMIT),
        name="proj",
    )(x2, anw, w_in, qnw, knw, cos, sina, sinb, bd)


def _state_kernel(kvf_ref, kvb_ref, zf_ref, zb_ref, gf_ref, gb_ref, bdm_ref,
                  rf_ref, rb_ref, cf_ref, cb_ref):
    @pl.when(pl.program_id(0) == 0)
    def _():
        cf_ref[...] = jnp.zeros_like(cf_ref)
        cb_ref[...] = jnp.zeros_like(cb_ref)

    bdm = bdm_ref[...]
    batch = kvf_ref.shape[0]
    fwd_order = list(range(STATE_CH))
    for kv_ref, z_ref, g_ref, r_ref, c_ref, order in (
            (kvf_ref, zf_ref, gf_ref, rf_ref, cf_ref, fwd_order),
            (kvb_ref, zb_ref, gb_ref, rb_ref, cb_ref, fwd_order[::-1])):
        for b in range(batch):
            for p in range(PAIRS):
                sl = slice(p * LANES, (p + 1) * LANES)
                vsl = slice(RET_W + p * LANES, RET_W + (p + 1) * LANES)
                carry = c_ref[b, p]
                for c in order:
                    rows = slice(c * CHUNK, (c + 1) * CHUNK)
                    kz = (kv_ref[b, rows, sl].astype(F32) * z_ref[:, sl]).astype(BF16)
                    kv = _dot_tn(kz, kv_ref[b, rows, vsl])
                    r_ref[b, c, p] = carry.astype(BF16)
                    carry = g_ref[p] * carry + kv * bdm
                c_ref[b, p] = carry


def _state_call(proj3, zf, zb, gf, gb, bdm):
    batch, seq, _ = proj3.shape
    ns = seq // (CHUNK * STATE_CH)
    assert C_RK % (2 * RET_W) == 0 and C_RV == C_RK + RET_W
    kv_col = C_RK // (2 * RET_W)
    blk = (batch, CHUNK * STATE_CH, 2 * RET_W)
    out_blk = (batch, STATE_CH, PAIRS, LANES, LANES)
    out_sds = jax.ShapeDtypeStruct((batch, seq // CHUNK, PAIRS, LANES, LANES), BF16)
    return pl.pallas_call(
        _state_kernel,
        grid=(ns,),
        in_specs=[
            pl.BlockSpec(blk, lambda s: (0, s, kv_col)),
            pl.BlockSpec(blk, lambda s: (0, ns - 1 - s, kv_col)),
            _resident((CHUNK, RET_W)),
            _resident((CHUNK, RET_W)),
            _resident((PAIRS, LANES, LANES)),
            _resident((PAIRS, LANES, LANES)),
            _resident((LANES, LANES)),
        ],
        out_specs=[
            pl.BlockSpec(out_blk, lambda s: (0, s, 0, 0, 0)),
            pl.BlockSpec(out_blk, lambda s: (0, ns - 1 - s, 0, 0, 0)),
        ],
        out_shape=[out_sds, out_sds],
        scratch_shapes=[pltpu.VMEM((batch, PAIRS, LANES, LANES), F32),
                        pltpu.VMEM((batch, PAIRS, LANES, LANES), F32)],
        compiler_params=pltpu.CompilerParams(
            dimension_semantics=("arbitrary",), vmem_limit_bytes=VMEM_LIMIT),
        name="ret_state",
    )(proj3, proj3, zf, zb, gf, gb, bdm)


def _mix_kernel(sink_ref, p_ref, prev_ref, next_ref, rf_ref, rb_ref, x_ref, wo_ref,
                bias_ref, dm_ref, xif_ref, xib_ref, rnw_ref, o_ref, y_ref):
    lane = lax.broadcasted_iota(jnp.int32, (CHUNK, LANES), 1).astype(F32).astype(BF16)
    lo = lane < HEAD_DIM
    hi = jnp.logical_not(lo)
    lo_f32 = lax.broadcasted_iota(jnp.int32, (CHUNK, LANES), 1) < HEAD_DIM
    zero = jnp.zeros((), BF16)

    def keep(mask, a):
        reps = a.shape[0] // CHUNK
        m = mask if reps == 1 else jnp.concatenate([mask] * reps, axis=0)
        return jnp.where(m, a, zero)

    tile = pl.program_id(1)
    n_sub = p_ref.shape[0] // CHUNK
    group = ATTN_HEADS // ATTN_KV_HEADS
    k2_cols = slice(C_K2, C_K2 + KV2_W)
    v2_cols = slice(C_V2, C_V2 + KV2_W)

    def kv_window(j):
        if j == 0:
            inner = slice(0, 2 * CHUNK)
            k3 = jnp.concatenate([prev_ref[:, :KV2_W], p_ref[inner, k2_cols]], axis=0)
            v3 = jnp.concatenate([prev_ref[:, KV2_W:], p_ref[inner, v2_cols]], axis=0)
            bias = bias_ref[jnp.where(tile == 0, 0, 1)]
        elif j == n_sub - 1:
            inner = slice((j - 1) * CHUNK, (j + 1) * CHUNK)
            k3 = jnp.concatenate([p_ref[inner, k2_cols], next_ref[:, :KV2_W]], axis=0)
            v3 = jnp.concatenate([p_ref[inner, v2_cols], next_ref[:, KV2_W:]], axis=0)
            bias = bias_ref[jnp.where(tile == pl.num_programs(1) - 1, 2, 1)]
        else:
            inner = slice((j - 1) * CHUNK, (j + 2) * CHUNK)
            k3 = p_ref[inner, k2_cols]
            v3 = p_ref[inner, v2_cols]
            bias = bias_ref[1]
        return k3, v3, bias

    def attn_scores(j, g):
        k3, v3, bias = kv_window(j)
        g_low = slice(g * LANES, (g + 1) * LANES)
        g_high = slice((1 - g) * LANES, (2 - g) * LANES)
        kg = jnp.where(jnp.concatenate([lo] * 3, axis=0), k3[:, g_low], k3[:, g_high])
        v_lo = keep(lo, v3[:, g_low])
        v_hi = keep(hi, v3[:, g_high])
        qs = []
        for i in range(group):
            h = g * group + i
            qp = p_ref[j * CHUNK:(j + 1) * CHUNK,
                       C_AQ + (h // 2) * LANES:C_AQ + (h // 2 + 1) * LANES]
            qs.append(keep(lo if h % 2 == 0 else hi, qp))
        s_all = _dot_nt(jnp.concatenate(qs, axis=0), kg)
        return s_all, v_lo, v_hi, bias

    def attn_finish(j, g, s_all, v_lo, v_hi, bias):
        ps, rs = {}, {}

        def softmax_head(i):
            h = g * group + i
            s = s_all[i * CHUNK:(i + 1) * CHUNK] + bias
            sink = sink_ref[h] * LOG2_E
            m = jnp.maximum(jnp.max(s, axis=-1, keepdims=True), sink)
            e = jnp.exp2(s - m)
            den = jnp.sum(e, axis=-1, keepdims=True) + jnp.exp2(sink - m)
            ps[i] = e.astype(BF16)
            rs[i] = 1.0 / den

        softmax_head(0)
        softmax_head(2)
        o_lo = _dot(jnp.concatenate([ps[0], ps[2]], axis=0), v_lo)
        softmax_head(1)
        softmax_head(3)
        o_hi = _dot(jnp.concatenate([ps[1], ps[3]], axis=0), v_hi)
        for pp in range(group // 2):
            part = slice(pp * CHUNK, (pp + 1) * CHUNK)
            pair = g * (group // 2) + pp
            y_ref[j * CHUNK:(j + 1) * CHUNK, pair * LANES:(pair + 1) * LANES] = (
                o_lo[part] * rs[2 * pp] + o_hi[part] * rs[2 * pp + 1]).astype(BF16)

    def retention_pair(j, p):
        rows = slice(j * CHUNK, (j + 1) * CHUNK)
        sl = slice(p * LANES, (p + 1) * LANES)
        col = lambda c0: slice(c0 + p * LANES, c0 + (p + 1) * LANES)
        qp, kp, vp = p_ref[rows, col(C_RQ)], p_ref[rows, col(C_RK)], p_ref[rows, col(C_RV)]
        s2 = _dot_nt(jnp.concatenate([keep(lo, qp), keep(hi, qp)], axis=0), kp)
        p0 = (s2[:CHUNK] * dm_ref[2 * p]).astype(BF16)
        p1 = (s2[CHUNK:] * dm_ref[2 * p + 1]).astype(BF16)
        qf = (qp.astype(F32) * xif_ref[:, sl]).astype(BF16)
        qb = (qp.astype(F32) * xib_ref[:, sl]).astype(BF16)
        lhs = jnp.concatenate([p0, p1, qf, qb], axis=1)
        rhs = jnp.concatenate([keep(lo, vp), keep(hi, vp), rf_ref[j, p], rb_ref[j, p]], axis=0)
        ret = _dot(lhs, rhs)
        sq = ret * ret
        ss_lo = jnp.sum(jnp.where(lo_f32, sq, 0.0), axis=-1, keepdims=True)
        ss_hi = jnp.sum(jnp.where(lo_f32, 0.0, sq), axis=-1, keepdims=True)
        ms = jnp.where(lo_f32, ss_lo, ss_hi) * (1.0 / HEAD_DIM)
        ret_n = ret * lax.rsqrt(ms + EPS) * rnw_ref[:, sl]
        gate = p_ref[rows, col(C_RG)].astype(F32)
        y_ref[rows, ATTN_W + p * LANES:ATTN_W + (p + 1) * LANES] = (
            gate * (1.0 / (1.0 + jnp.exp(-gate))) * ret_n).astype(BF16)

    units = [(j, g) for j in range(n_sub) for g in range(ATTN_KV_HEADS)]
    ret_items = [(j, p) for j in range(n_sub) for p in range(PAIRS)]
    assert len(units) * RET_PAIRS_PER_UNIT >= len(ret_items)
    pending = attn_scores(*units[0])
    for u, (j, g) in enumerate(units):
        following = attn_scores(*units[u + 1]) if u + 1 < len(units) else None
        for item in ret_items[u * RET_PAIRS_PER_UNIT:(u + 1) * RET_PAIRS_PER_UNIT]:
            retention_pair(*item)
        attn_finish(j, g, *pending)
        pending = following
    o_ref[...] = x_ref[...] + _dot(y_ref[...], wo_ref[...])


def _mix_call(sink, proj3, rf, rb, x3, w_out, bias, dmat, xif, xib, rnw):
    batch, seq, d_model = x3.shape
    tm = MIX_TM
    n_sub = tm // CHUNK
    nt = seq // tm
    nc = seq // CHUNK
    kv_cols = C_K2 // (2 * KV2_W)
    assert C_K2 % (2 * KV2_W) == 0 and C_V2 == C_K2 + KV2_W
    kv_prev = pl.BlockSpec(
        (None, CHUNK, 2 * KV2_W), lambda b, t: (b, jnp.maximum(t * n_sub - 1, 0), kv_cols))
    kv_next = pl.BlockSpec(
        (None, CHUNK, 2 * KV2_W), lambda b, t: (b, jnp.minimum((t + 1) * n_sub, nc - 1), kv_cols))
    st = pl.BlockSpec((None, n_sub, PAIRS, LANES, LANES), lambda b, t: (b, t, 0, 0, 0))
    return pl.pallas_call(
        _mix_kernel,
        grid=(batch, nt),
        in_specs=[
            pl.BlockSpec(memory_space=pltpu.SMEM),
            pl.BlockSpec((None, tm, PROJ_W), lambda b, t: (b, t, 0)),
            kv_prev, kv_next,
            st, st,
            pl.BlockSpec((None, tm, d_model), lambda b, t: (b, t, 0)),
            _resident((d_model, d_model)),
            _resident((3, CHUNK, 3 * CHUNK)),
            _resident((RET_HEADS, CHUNK, CHUNK)),
            _resident((CHUNK, RET_W)),
            _resident((CHUNK, RET_W)),
            _resident((1, RET_W)),
        ],
        out_specs=pl.BlockSpec((None, tm, d_model), lambda b, t: (b, t, 0)),
        out_shape=jax.ShapeDtypeStruct((batch, seq, d_model), F32),
        scratch_shapes=[pltpu.VMEM((tm, d_model), BF16)],
        compiler_params=pltpu.CompilerParams(
            dimension_semantics=("arbitrary", "arbitrary"), vmem_limit_bytes=VMEM_LIMIT),
        name="mix",
    )(sink, proj3, proj3, proj3, rf, rb, x3, w_out, bias, dmat, xif, xib, rnw)


def _ffn_kernel(h_ref, fnw_ref, wg_ref, wu_ref, wd_ref, o_ref, a_ref):
    h = h_ref[...]
    ms = jnp.mean(h * h, axis=-1, keepdims=True)
    m = (h * lax.rsqrt(ms + EPS) * fnw_ref[...]).astype(BF16)
    d_ff = wg_ref.shape[1]
    for c0 in range(0, d_ff, FFN_CK):
        c1 = min(c0 + FFN_CK, d_ff)
        g = _dot(m, wg_ref[:, c0:c1])
        u = _dot(m, wu_ref[:, c0:c1])
        a_ref[:, c0:c1] = (g * (1.0 / (1.0 + jnp.exp(-g))) * u).astype(BF16)
    o_ref[...] = h + _dot(a_ref[...], wd_ref[...])


def _ffn_call(h2, fnw, wg, wu, wd):
    tokens, d_model = h2.shape
    d_ff = wg.shape[1]
    tm = FFN_TM
    row = lambda i: (i, 0)
    return pl.pallas_call(
        _ffn_kernel,
        grid=(tokens // tm,),
        in_specs=[
            pl.BlockSpec((tm, d_model), row),
            _resident((1, d_model)),
            _resident((d_model, d_ff)),
            _resident((d_model, d_ff)),
            _resident((d_ff, d_model)),
        ],
        out_specs=pl.BlockSpec((tm, d_model), row),
        out_shape=jax.ShapeDtypeStruct((tokens, d_model), F32),
        scratch_shapes=[pltpu.VMEM((tm, d_ff), BF16)],
        compiler_params=pltpu.CompilerParams(
            dimension_semantics=("arbitrary",), vmem_limit_bytes=VMEM_LIMIT),
        name="ffn",
    )(h2, fnw, wg, wu, wd)


def _rope_tables(seq):
    inv_freq = ROPE_THETA ** (-jnp.arange(0, HEAD_DIM, 2, dtype=F32) / HEAD_DIM)
    ang = jnp.arange(seq, dtype=F32)[:, None] * inv_freq[None, :]
    cos, sin = jnp.cos(ang), jnp.sin(ang)
    zeros = jnp.zeros_like(sin)
    reps = LANES // HEAD_DIM
    cos_t = jnp.tile(jnp.concatenate([cos, cos], -1), (1, reps))
    sina_t = jnp.tile(jnp.concatenate([-sin, zeros], -1), (1, reps))
    sinb_t = jnp.tile(jnp.concatenate([zeros, sin], -1), (1, reps))
    return cos_t, sina_t, sinb_t


def _attn_bias():
    i = np.arange(CHUNK)[:, None]
    j = np.arange(3 * CHUNK)[None, :]
    band = np.abs(i + CHUNK - j) <= CHUNK
    first = band & (j >= CHUNK)
    last = band & (j < 2 * CHUNK)
    tab = np.stack([first, band, last]).astype(np.float32)
    return jnp.asarray(np.where(tab > 0, 0.0, NEG_INF).astype(np.float32))


def _retention_tables(log_f, log_b):
    idx = jnp.arange(CHUNK, dtype=F32)
    diff = idx[:, None] - idx[None, :]
    lf, lb = log_f[:, None, None], log_b[:, None, None]
    dmat = jnp.where(diff[None] >= 0,
                     jnp.exp(lf * jnp.maximum(diff, 0.0)[None]),
                     jnp.exp(lb * jnp.maximum(-diff, 0.0)[None]))
    per_lane = lambda t: jnp.repeat(t.T, HEAD_DIM, axis=1)
    xif = per_lane(jnp.exp(log_f[:, None] * (idx + 1.0)[None]))
    xib = per_lane(jnp.exp(log_b[:, None] * (CHUNK - idx)[None]))
    zf = per_lane(jnp.exp(log_f[:, None] * (CHUNK - 1.0 - idx)[None]))
    zb = per_lane(jnp.exp(log_b[:, None] * idx[None]))
    per_row = lambda g: jnp.broadcast_to(
        jnp.repeat(g.reshape(PAIRS, 2), HEAD_DIM, axis=1)[:, :, None], (PAIRS, LANES, LANES))
    gf = per_row(jnp.exp(log_f * CHUNK))
    gb = per_row(jnp.exp(log_b * CHUNK))
    return dmat, xif, xib, zf, zb, gf, gb


def _block_diag_ones(width, dtype):
    r = np.arange(width) // HEAD_DIM
    return jnp.asarray((r[:, None] == r[None, :]).astype(np.float32), dtype=dtype)


def kernel(x, attn_norm_w, w_in, q_norm_w, k_norm_w, attn_sink, ret_log_decay_fwd,
           ret_log_decay_bwd, ret_norm_w, w_out, ffn_norm_w, w_gate, w_up, w_down):
    batch, seq, d_model = x.shape
    depth = w_in.shape[0]
    assert seq % PROJ_TM == 0 and seq % MIX_TM == 0 and (batch * seq) % FFN_TM == 0
    assert MIX_TM // CHUNK >= 2

    cos_t, sina_t, sinb_t = _rope_tables(seq)
    bias = _attn_bias()
    bd = _block_diag_ones(RET_W, BF16)
    bdm = _block_diag_ones(LANES, F32)

    h = x
    for l in range(depth):
        qnw = jnp.tile(q_norm_w[l], ATTN_HEADS)[None, :]
        knw = jnp.tile(k_norm_w[l], ATTN_KV_HEADS)[None, :]
        log_f = -jnp.abs(ret_log_decay_fwd[l].astype(F32))
        log_b = -jnp.abs(ret_log_decay_bwd[l].astype(F32))
        dmat, xif, xib, zf, zb, gf, gb = _retention_tables(log_f, log_b)

        proj = _proj_call(h.reshape(batch * seq, d_model), attn_norm_w[l][None, :],
                          w_in[l].astype(BF16), qnw, knw, cos_t, sina_t, sinb_t, bd, seq)
        proj3 = proj.reshape(batch, seq, PROJ_W)
        rf, rb = _state_call(proj3, zf, zb, gf, gb, bdm)
        h = _mix_call(attn_sink[l].astype(F32), proj3, rf, rb, h, w_out[l].astype(BF16),
                      bias, dmat, xif, xib, ret_norm_w[l][None, :])
        h = _ffn_call(h.reshape(batch * seq, d_model), ffn_norm_w[l][None, :],
                      w_gate[l].astype(BF16), w_up[l].astype(BF16),
                      w_down[l].astype(BF16)).reshape(batch, seq, d_model)
    return h
```

```python
import jax
import jax.numpy as jnp
import numpy as np
from jax import lax
from jax.experimental import pallas as pl
from jax.experimental.pallas import tpu as pltpu

HEAD_DIM = 64
ATTN_HEADS = 8
ATTN_KV_HEADS = 2
RET_HEADS = 8
CHUNK = 128
ROPE_THETA = 10000.0
EPS = 1e-6
NEG_INF = -1e30
LOG2_E = 1.4426950408889634

LANES = 128
PAIRS = RET_HEADS // 2
ATTN_W = ATTN_HEADS * HEAD_DIM
RET_W = RET_HEADS * HEAD_DIM
KV2_W = 2 * ATTN_KV_HEADS * HEAD_DIM

C_AQ = 0
C_K2 = C_AQ + ATTN_W
C_V2 = C_K2 + KV2_W
C_RK = C_V2 + KV2_W
C_RV = C_RK + RET_W
C_RQ = C_RV + RET_W
C_RG = C_RQ + RET_W
PROJ_W = C_RG + RET_W

PROJ_TM = 512
MIX_TM = 512
RET_PAIRS_PER_UNIT = 2
STATE_CH = 2
FFN_TM = 512
FFN_CK = 512
VMEM_LIMIT = 56 * 1024 * 1024

BF16 = jnp.bfloat16
F32 = jnp.float32


def _dot(a, b):
    return jnp.dot(a, b, preferred_element_type=F32)


def _dot_nt(a, b):
    return lax.dot_general(a, b, (((1,), (1,)), ((), ())), preferred_element_type=F32)


def _dot_tn(a, b):
    return lax.dot_general(a, b, (((0,), (0,)), ((), ())), preferred_element_type=F32)


def _resident(shape):
    zeros = (0,) * len(shape)
    return pl.BlockSpec(shape, lambda *_: zeros, pipeline_mode=pl.Buffered(1))


def _proj_kernel(x_ref, anw_ref, w_ref, qnw_ref, knw_ref, cos_ref, sina_ref, sinb_ref,
                 bd_ref, o_ref):
    x = x_ref[...]
    ms = jnp.mean(x * x, axis=-1, keepdims=True)
    n = x * lax.rsqrt(ms + EPS) * anw_ref[...]
    cos = cos_ref[...]
    sina = sina_ref[...]
    sinb = sinb_ref[...]

    def proj(c0, width):
        return _dot(n, w_ref[:, c0:c0 + width])

    def head_norm(y, w):
        width = y.shape[-1]
        ss = _dot((y * y).astype(BF16), bd_ref[:width, :width])
        return y * lax.rsqrt(ss * (1.0 / HEAD_DIM) + EPS) * w

    def rope(ys):
        return ys * cos + pltpu.roll(ys, LANES - 32, 1) * sina + pltpu.roll(ys, 32, 1) * sinb

    def rope_store(y, c0, scale):
        for s in range(y.shape[-1] // LANES):
            r = rope(y[:, s * LANES:(s + 1) * LANES])
            if scale != 1.0:
                r = r * scale
            o_ref[:, c0 + s * LANES:c0 + (s + 1) * LANES] = r.astype(BF16)

    def plain_store(y, c0):
        o_ref[:, c0:c0 + y.shape[-1]] = y.astype(BF16)

    def store_both_orders(y, c0):
        o_ref[:, c0:c0 + LANES] = y.astype(BF16)
        o_ref[:, c0 + LANES:c0 + 2 * LANES] = pltpu.roll(y, HEAD_DIM, 1).astype(BF16)

    def kv_finish(y):
        store_both_orders(rope(head_norm(y[:, :LANES], knw_ref[...])), C_K2)
        store_both_orders(y[:, LANES:], C_V2)

    scale = HEAD_DIM ** -0.5
    kvw = ATTN_KV_HEADS * HEAD_DIM
    in_kv = ATTN_W
    in_rq = in_kv + 2 * kvw
    stages = [
        (0, ATTN_W, lambda y: rope_store(head_norm(y, qnw_ref[...]), C_AQ, scale * LOG2_E)),
        (in_kv, 2 * kvw, kv_finish),
        (in_rq, RET_W, lambda y: rope_store(y, C_RQ, 1.0)),
        (in_rq + RET_W, RET_W, lambda y: rope_store(y, C_RK, scale)),
        (in_rq + 2 * RET_W, RET_W, lambda y: plain_store(y, C_RV)),
        (in_rq + 3 * RET_W, RET_W, lambda y: plain_store(y, C_RG)),
    ]
    pending = proj(stages[0][0], stages[0][1])
    for i, (_, _, finish) in enumerate(stages):
        following = proj(stages[i + 1][0], stages[i + 1][1]) if i + 1 < len(stages) else None
        finish(pending)
        pending = following


def _proj_call(x2, anw, w_in, qnw, knw, cos, sina, sinb, bd, seq):
    tokens, d_model = x2.shape
    tm = PROJ_TM
    pos_blocks = seq // tm
    row = lambda i: (i, 0)
    pos = lambda i: (i % pos_blocks, 0)
    return pl.pallas_call(
        _proj_kernel,
        grid=(tokens // tm,),
        in_specs=[
            pl.BlockSpec((tm, d_model), row),
            _resident((1, d_model)),
            _resident(w_in.shape),
            _resident((1, ATTN_W)),
            _resident((1, LANES)),
            pl.BlockSpec((tm, LANES), pos),
            pl.BlockSpec((tm, LANES), pos),
            pl.BlockSpec((tm, LANES), pos),
            _resident((ATTN_W, ATTN_W)),
        ],
        out_specs=pl.BlockSpec((tm, PROJ_W), row),
        out_shape=jax.ShapeDtypeStruct((tokens, PROJ_W), BF16),
        compiler_params=pltpu.CompilerParams(
            dimension_semantics=("arbitrary",), vmem_limit_bytes=VMEM_LIMIT),
        name="proj",
    )(x2, anw, w_in, qnw, knw, cos, sina, sinb, bd)


def _state_kernel(kvf_ref, kvb_ref, zf_ref, zb_ref, gf_ref, gb_ref, bdm_ref,
                  rf_ref, rb_ref, cf_ref, cb_ref):
    @pl.when(pl.program_id(0) == 0)
    def _():
        cf_ref[...] = jnp.zeros_like(cf_ref)
        cb_ref[...] = jnp.zeros_like(cb_ref)

    bdm = bdm_ref[...]
    batch = kvf_ref.shape[0]
    fwd_order = list(range(STATE_CH))
    for kv_ref, z_ref, g_ref, r_ref, c_ref, order in (
            (kvf_ref, zf_ref, gf_ref, rf_ref, cf_ref, fwd_order),
            (kvb_ref, zb_ref, gb_ref, rb_ref, cb_ref, fwd_order[::-1])):
        for b in range(batch):
            for p in range(PAIRS):
                sl = slice(p * LANES, (p + 1) * LANES)
                vsl = slice(RET_W + p * LANES, RET_W + (p + 1) * LANES)
                carry = c_ref[b, p]
                for c in order:
                    rows = slice(c * CHUNK, (c + 1) * CHUNK)
                    kz = (kv_ref[b, rows, sl].astype(F32) * z_ref[:, sl]).astype(BF16)
                    kv = _dot_tn(kz, kv_ref[b, rows, vsl])
                    r_ref[b, c, p] = carry.astype(BF16)
                    carry = g_ref[p] * carry + kv * bdm
                c_ref[b, p] = carry


def _state_call(proj3, zf, zb, gf, gb, bdm):
    batch, seq, _ = proj3.shape
    ns = seq // (CHUNK * STATE_CH)
    assert C_RK % (2 * RET_W) == 0 and C_RV == C_RK + RET_W
    kv_col = C_RK // (2 * RET_W)
    blk = (batch, CHUNK * STATE_CH, 2 * RET_W)
    out_blk = (batch, STATE_CH, PAIRS, LANES, LANES)
    out_sds = jax.ShapeDtypeStruct((batch, seq // CHUNK, PAIRS, LANES, LANES), BF16)
    return pl.pallas_call(
        _state_kernel,
        grid=(ns,),
        in_specs=[
            pl.BlockSpec(blk, lambda s: (0, s, kv_col)),
            pl.BlockSpec(blk, lambda s: (0, ns - 1 - s, kv_col)),
            _resident((CHUNK, RET_W)),
            _resident((CHUNK, RET_W)),
            _resident((PAIRS, LANES, LANES)),
            _resident((PAIRS, LANES, LANES)),
            _resident((LANES, LANES)),
        ],
        out_specs=[
            pl.BlockSpec(out_blk, lambda s: (0, s, 0, 0, 0)),
            pl.BlockSpec(out_blk, lambda s: (0, ns - 1 - s, 0, 0, 0)),
        ],
        out_shape=[out_sds, out_sds],
        scratch_shapes=[pltpu.VMEM((batch, PAIRS, LANES, LANES), F32),
                        pltpu.VMEM((batch, PAIRS, LANES, LANES), F32)],
        compiler_params=pltpu.CompilerParams(
            dimension_semantics=("arbitrary",), vmem_limit_bytes=VMEM_LIMIT),
        name="ret_state",
    )(proj3, proj3, zf, zb, gf, gb, bdm)


def _mix_kernel(sink_ref, p_ref, prev_ref, next_ref, rf_ref, rb_ref, x_ref, wo_ref,
                bias_ref, dm_ref, xif_ref, xib_ref, rnw_ref, o_ref, y_ref):
    lane = lax.broadcasted_iota(jnp.int32, (CHUNK, LANES), 1).astype(F32).astype(BF16)
    lo = lane < HEAD_DIM
    hi = jnp.logical_not(lo)
    lo_f32 = lax.broadcasted_iota(jnp.int32, (CHUNK, LANES), 1) < HEAD_DIM
    zero = jnp.zeros((), BF16)

    def keep(mask, a):
        reps = a.shape[0] // CHUNK
        m = mask if reps == 1 else jnp.concatenate([mask] * reps, axis=0)
        return jnp.where(m, a, zero)

    tile = pl.program_id(1)
    n_sub = p_ref.shape[0] // CHUNK
    group = ATTN_HEADS // ATTN_KV_HEADS
    k2_cols = slice(C_K2, C_K2 + KV2_W)
    v2_cols = slice(C_V2, C_V2 + KV2_W)

    def kv_window(j):
        if j == 0:
            inner = slice(0, 2 * CHUNK)
            k3 = jnp.concatenate([prev_ref[:, :KV2_W], p_ref[inner, k2_cols]], axis=0)
            v3 = jnp.concatenate([prev_ref[:, KV2_W:], p_ref[inner, v2_cols]], axis=0)
            bias = bias_ref[jnp.where(tile == 0, 0, 1)]
        elif j == n_sub - 1:
            inner = slice((j - 1) * CHUNK, (j + 1) * CHUNK)
            k3 = jnp.concatenate([p_ref[inner, k2_cols], next_ref[:, :KV2_W]], axis=0)
            v3 = jnp.concatenate([p_ref[inner, v2_cols], next_ref[:, KV2_W:]], axis=0)
            bias = bias_ref[jnp.where(tile == pl.num_programs(1) - 1, 2, 1)]
        else:
            inner = slice((j - 1) * CHUNK, (j + 2) * CHUNK)
            k3 = p_ref[inner, k2_cols]
            v3 = p_ref[inner, v2_cols]
            bias = bias_ref[1]
        return k3, v3, bias

    def attn_scores(j, g):
        k3, v3, bias = kv_window(j)
        g_low = slice(g * LANES, (g + 1) * LANES)
        g_high = slice((1 - g) * LANES, (2 - g) * LANES)
        kg = jnp.where(jnp.concatenate([lo] * 3, axis=0), k3[:, g_low], k3[:, g_high])
        v_lo = keep(lo, v3[:, g_low])
        v_hi = keep(hi, v3[:, g_high])
        qs = []
        for i in range(group):
            h = g * group + i
            qp = p_ref[j * CHUNK:(j + 1) * CHUNK,
                       C_AQ + (h // 2) * LANES:C_AQ + (h // 2 + 1) * LANES]
            qs.append(keep(lo if h % 2 == 0 else hi, qp))
        s_all = _dot_nt(jnp.concatenate(qs, axis=0), kg)
        return s_all, v_lo, v_hi, bias

    def attn_finish(j, g, s_all, v_lo, v_hi, bias):
        ps, rs = {}, {}

        def softmax_head(i):
            h = g * group + i
            s = s_all[i * CHUNK:(i + 1) * CHUNK] + bias
            sink = sink_ref[h] * LOG2_E
            m = jnp.maximum(jnp.max(s, axis=-1, keepdims=True), sink)
            e = jnp.exp2(s - m)
            den = jnp.sum(e, axis=-1, keepdims=True) + jnp.exp2(sink - m)
            ps[i] = e.astype(BF16)
            rs[i] = 1.0 / den

        softmax_head(0)
        softmax_head(2)
        o_lo = _dot(jnp.concatenate([ps[0], ps[2]], axis=0), v_lo)
        softmax_head(1)
        softmax_head(3)
        o_hi = _dot(jnp.concatenate([ps[1], ps[3]], axis=0), v_hi)
        for pp in range(group // 2):
            part = slice(pp * CHUNK, (pp + 1) * CHUNK)
            pair = g * (group // 2) + pp
            y_ref[j * CHUNK:(j + 1) * CHUNK, pair * LANES:(pair + 1) * LANES] = (
                o_lo[part] * rs[2 * pp] + o_hi[part] * rs[2 * pp + 1])

    def retention_pair(j, p):
        rows = slice(j * CHUNK, (j + 1) * CHUNK)
        sl = slice(p * LANES, (p + 1) * LANES)
        col = lambda c0: slice(c0 + p * LANES, c0 + (p + 1) * LANES)
        qp, kp, vp = p_ref[rows, col(C_RQ)], p_ref[rows, col(C_RK)], p_ref[rows, col(C_RV)]
        s2 = _dot_nt(jnp.concatenate([keep(lo, qp), keep(hi, qp)], axis=0), kp)
        p0 = (s2[:CHUNK] * dm_ref[2 * p]).astype(BF16)
        p1 = (s2[CHUNK:] * dm_ref[2 * p + 1]).astype(BF16)
        qf = (qp.astype(F32) * xif_ref[:, sl]).astype(BF16)
        qb = (qp.astype(F32) * xib_ref[:, sl]).astype(BF16)
        lhs = jnp.concatenate([p0, p1, qf, qb], axis=1)
        rhs = jnp.concatenate([keep(lo, vp), keep(hi, vp), rf_ref[j, p], rb_ref[j, p]], axis=0)
        ret = _dot(lhs, rhs)
        sq = ret * ret
        ss_lo = jnp.sum(jnp.where(lo_f32, sq, 0.0), axis=-1, keepdims=True)
        ss_hi = jnp.sum(jnp.where(lo_f32, 0.0, sq), axis=-1, keepdims=True)
        ms = jnp.where(lo_f32, ss_lo, ss_hi) * (1.0 / HEAD_DIM)
        ret_n = ret * lax.rsqrt(ms + EPS) * rnw_ref[:, sl]
        gate = p_ref[rows, col(C_RG)].astype(F32)
        y_ref[rows, ATTN_W + p * LANES:ATTN_W + (p + 1) * LANES] = (
            gate * (1.0 / (1.0 + jnp.exp(-gate))) * ret_n)

    units = [(j, g) for j in range(n_sub) for g in range(ATTN_KV_HEADS)]
    ret_items = [(j, p) for j in range(n_sub) for p in range(PAIRS)]
    assert len(units) * RET_PAIRS_PER_UNIT >= len(ret_items)
    pending = attn_scores(*units[0])
    for u, (j, g) in enumerate(units):
        following = attn_scores(*units[u + 1]) if u + 1 < len(units) else None
        for item in ret_items[u * RET_PAIRS_PER_UNIT:(u + 1) * RET_PAIRS_PER_UNIT]:
            retention_pair(*item)
        attn_finish(j, g, *pending)
        pending = following
    o_ref[...] = x_ref[...] + _dot(y_ref[...], wo_ref[...])


def _mix_call(sink, proj3, rf, rb, x3, w_out, bias, dmat, xif, xib, rnw):
    batch, seq, d_model = x3.shape
    tm = MIX_TM
    n_sub = tm // CHUNK
    nt = seq // tm
    nc = seq // CHUNK
    kv_cols = C_K2 // (2 * KV2_W)
    assert C_K2 % (2 * KV2_W) == 0 and C_V2 == C_K2 + KV2_W
    kv_prev = pl.BlockSpec(
        (None, CHUNK, 2 * KV2_W), lambda b, t: (b, jnp.maximum(t * n_sub - 1, 0), kv_cols))
    kv_next = pl.BlockSpec(
        (None, CHUNK, 2 * KV2_W), lambda b, t: (b, jnp.minimum((t + 1) * n_sub, nc - 1), kv_cols))
    st = pl.BlockSpec((None, n_sub, PAIRS, LANES, LANES), lambda b, t: (b, t, 0, 0, 0))
    return pl.pallas_call(
        _mix_kernel,
        grid=(batch, nt),
        in_specs=[
            pl.BlockSpec(memory_space=pltpu.SMEM),
            pl.BlockSpec((None, tm, PROJ_W), lambda b, t: (b, t, 0)),
            kv_prev, kv_next,
            st, st,
            pl.BlockSpec((None, tm, d_model), lambda b, t: (b, t, 0)),
            _resident((d_model, d_model)),
            _resident((3, CHUNK, 3 * CHUNK)),
            _resident((RET_HEADS, CHUNK, CHUNK)),
            _resident((CHUNK, RET_W)),
            _resident((CHUNK, RET_W)),
            _resident((1, RET_W)),
        ],
        out_specs=pl.BlockSpec((None, tm, d_model), lambda b, t: (b, t, 0)),
        out_shape=jax.ShapeDtypeStruct((batch, seq, d_model), F32),
        scratch_shapes=[pltpu.VMEM((tm, d_model), F32)],
        compiler_params=pltpu.CompilerParams(
            dimension_semantics=("arbitrary", "arbitrary"), vmem_limit_bytes=VMEM_LIMIT),
        name="mix",
    )(sink, proj3, proj3, proj3, rf, rb, x3, w_out, bias, dmat, xif, xib, rnw)


def _ffn_kernel(h_ref, fnw_ref, wg_ref, wu_ref, wd_ref, o_ref, a_ref):
    h = h_ref[...]
    ms = jnp.mean(h * h, axis=-1, keepdims=True)
    m = h * lax.rsqrt(ms + EPS) * fnw_ref[...]
    d_ff = wg_ref.shape[1]
    for c0 in range(0, d_ff, FFN_CK):
        c1 = min(c0 + FFN_CK, d_ff)
        g = _dot(m, wg_ref[:, c0:c1])
        u = _dot(m, wu_ref[:, c0:c1])
        a_ref[:, c0:c1] = g * (1.0 / (1.0 + jnp.exp(-g))) * u
    o_ref[...] = h + _dot(a_ref[...], wd_ref[...])


def _ffn_call(h2, fnw, wg, wu, wd):
    tokens, d_model = h2.shape
    d_ff = wg.shape[1]
    tm = FFN_TM
    row = lambda i: (i, 0)
    return pl.pallas_call(
        _ffn_kernel,
        grid=(tokens // tm,),
        in_specs=[
            pl.BlockSpec((tm, d_model), row),
            _resident((1, d_model)),
            _resident((d_model, d_ff)),
            _resident((d_model, d_ff)),
            _resident((d_ff, d_model)),
        ],
        out_specs=pl.BlockSpec((tm, d_model), row),
        out_shape=jax.ShapeDtypeStruct((tokens, d_model), F32),
        scratch_shapes=[pltpu.VMEM((tm, d_ff), F32)],
        compiler_params=pltpu.CompilerParams(
            dimension_semantics=("arbitrary",), vmem_limit_bytes=VMEM_LIMIT),
        name="ffn",
    )(h2, fnw, wg, wu, wd)


def _rope_tables(seq):
    inv_freq = ROPE_THETA ** (-jnp.arange(0, HEAD_DIM, 2, dtype=F32) / HEAD_DIM)
    ang = jnp.arange(seq, dtype=F32)[:, None] * inv_freq[None, :]
    cos, sin = jnp.cos(ang), jnp.sin(ang)
    zeros = jnp.zeros_like(sin)
    reps = LANES // HEAD_DIM
    cos_t = jnp.tile(jnp.concatenate([cos, cos], -1), (1, reps))
    sina_t = jnp.tile(jnp.concatenate([-sin, zeros], -1), (1, reps))
    sinb_t = jnp.tile(jnp.concatenate([zeros, sin], -1), (1, reps))
    return cos_t, sina_t, sinb_t


def _attn_bias():
    i = np.arange(CHUNK)[:, None]
    j = np.arange(3 * CHUNK)[None, :]
    band = np.abs(i + CHUNK - j) <= CHUNK
    first = band & (j >= CHUNK)
    last = band & (j < 2 * CHUNK)
    tab = np.stack([first, band, last]).astype(np.float32)
    return jnp.asarray(np.where(tab > 0, 0.0, NEG_INF).astype(np.float32))


def _retention_tables(log_f, log_b):
    idx = jnp.arange(CHUNK, dtype=F32)
    diff = idx[:, None] - idx[None, :]
    lf, lb = log_f[:, None, None], log_b[:, None, None]
    dmat = jnp.where(diff[None] >= 0,
                     jnp.exp(lf * jnp.maximum(diff, 0.0)[None]),
                     jnp.exp(lb * jnp.maximum(-diff, 0.0)[None]))
    per_lane = lambda t: jnp.repeat(t.T, HEAD_DIM, axis=1)
    xif = per_lane(jnp.exp(log_f[:, None] * (idx + 1.0)[None]))
    xib = per_lane(jnp.exp(log_b[:, None] * (CHUNK - idx)[None]))
    zf = per_lane(jnp.exp(log_f[:, None] * (CHUNK - 1.0 - idx)[None]))
    zb = per_lane(jnp.exp(log_b[:, None] * idx[None]))
    per_row = lambda g: jnp.broadcast_to(
        jnp.repeat(g.reshape(PAIRS, 2), HEAD_DIM, axis=1)[:, :, None], (PAIRS, LANES, LANES))
    gf = per_row(jnp.exp(log_f * CHUNK))
    gb = per_row(jnp.exp(log_b * CHUNK))
    return dmat, xif, xib, zf, zb, gf, gb


def _block_diag_ones(width, dtype):
    r = np.arange(width) // HEAD_DIM
    return jnp.asarray((r[:, None] == r[None, :]).astype(np.float32), dtype=dtype)


def kernel(x, attn_norm_w, w_in, q_norm_w, k_norm_w, attn_sink, ret_log_decay_fwd,
           ret_log_decay_bwd, ret_norm_w, w_out, ffn_norm_w, w_gate, w_up, w_down):
    batch, seq, d_model = x.shape
    depth = w_in.shape[0]
    assert seq % PROJ_TM == 0 and seq % MIX_TM == 0 and (batch * seq) % FFN_TM == 0
    assert MIX_TM // CHUNK >= 2

    cos_t, sina_t, sinb_t = _rope_tables(seq)
    bias = _attn_bias()
    bd = _block_diag_ones(RET_W, BF16)
    bdm = _block_diag_ones(LANES, F32)

    h = x
    for l in range(depth):
        qnw = jnp.tile(q_norm_w[l], ATTN_HEADS)[None, :]
        knw = jnp.tile(k_norm_w[l], ATTN_KV_HEADS)[None, :]
        log_f = -jnp.abs(ret_log_decay_fwd[l].astype(F32))
        log_b = -jnp.abs(ret_log_decay_bwd[l].astype(F32))
        dmat, xif, xib, zf, zb, gf, gb = _retention_tables(log_f, log_b)

        proj = _proj_call(h.reshape(batch * seq, d_model), attn_norm_w[l][None, :],
                          w_in[l], qnw, knw, cos_t, sina_t, sinb_t, bd, seq)
        proj3 = proj.reshape(batch, seq, PROJ_W)
        rf, rb = _state_call(proj3, zf, zb, gf, gb, bdm)
        h = _mix_call(attn_sink[l].astype(F32), proj3, rf, rb, h, w_out[l],
                      bias, dmat, xif, xib, ret_norm_w[l][None, :])
        h = _ffn_call(h.reshape(batch * seq, d_model), ffn_norm_w[l][None, :],
                      w_gate[l], w_up[l], w_down[l]).reshape(batch, seq, d_model)
    return h
```

```python
import jax
import jax.numpy as jnp
import numpy as np
from jax import lax
from jax.experimental import pallas as pl
from jax.experimental.pallas import tpu as pltpu

HEAD_DIM = 64
ATTN_HEADS = 8
ATTN_KV_HEADS = 2
RET_HEADS = 8
CHUNK = 128
ROPE_THETA = 10000.0
EPS = 1e-6
NEG_INF = -1e30
LOG2_E = 1.4426950408889634

LANES = 128
PAIRS = RET_HEADS // 2
ATTN_W = ATTN_HEADS * HEAD_DIM
RET_W = RET_HEADS * HEAD_DIM
KV2_W = 2 * ATTN_KV_HEADS * HEAD_DIM

C_AQ = 0
C_K2 = C_AQ + ATTN_W
C_V2 = C_K2 + KV2_W
C_RK = C_V2 + KV2_W
C_RV = C_RK + RET_W
C_RQ = C_RV + RET_W
C_RG = C_RQ + RET_W
PROJ_W = C_RG + RET_W

PROJ_TM = 512
MIX_TM = 512
RET_PAIRS_PER_UNIT = 2
STATE_CH = 2
FFN_TM = 512
FFN_CK = 512
VMEM_LIMIT = 56 * 1024 * 1024

BF16 = jnp.bfloat16
F32 = jnp.float32


def _dot(a, b):
    return jnp.dot(a, b, preferred_element_type=F32)


def _dot_nt(a, b):
    return lax.dot_general(a, b, (((1,), (1,)), ((), ())), preferred_element_type=F32)


def _dot_tn(a, b):
    return lax.dot_general(a, b, (((0,), (0,)), ((), ())), preferred_element_type=F32)


def _resident(shape):
    zeros = (0,) * len(shape)
    return pl.BlockSpec(shape, lambda *_: zeros, pipeline_mode=pl.Buffered(1))


def _proj_kernel(x_ref, anw_ref, w_ref, qnw_ref, knw_ref, cos_ref, sina_ref, sinb_ref,
                 bd_ref, o_ref):
    x = x_ref[...]
    ms = jnp.mean(x * x, axis=-1, keepdims=True)
    n = x * lax.rsqrt(ms + EPS) * anw_ref[...]
    cos = cos_ref[...]
    sina = sina_ref[...]
    sinb = sinb_ref[...]

    def proj(c0, width):
        return _dot(n, w_ref[:, c0:c0 + width])

    def head_norm(y, w):
        width = y.shape[-1]
        ss = _dot((y * y).astype(BF16), bd_ref[:width, :width])
        return y * lax.rsqrt(ss * (1.0 / HEAD_DIM) + EPS) * w

    def rope(ys):
        return ys * cos + pltpu.roll(ys, LANES - 32, 1) * sina + pltpu.roll(ys, 32, 1) * sinb

    def rope_store(y, c0, scale):
        for s in range(y.shape[-1] // LANES):
            r = rope(y[:, s * LANES:(s + 1) * LANES])
            if scale != 1.0:
                r = r * scale
            o_ref[:, c0 + s * LANES:c0 + (s + 1) * LANES] = r.astype(BF16)

    def plain_store(y, c0):
        o_ref[:, c0:c0 + y.shape[-1]] = y.astype(BF16)

    def store_both_orders(y, c0):
        o_ref[:, c0:c0 + LANES] = y.astype(BF16)
        o_ref[:, c0 + LANES:c0 + 2 * LANES] = pltpu.roll(y, HEAD_DIM, 1).astype(BF16)

    def kv_finish(y):
        store_both_orders(rope(head_norm(y[:, :LANES], knw_ref[...])), C_K2)
        store_both_orders(y[:, LANES:], C_V2)

    scale = HEAD_DIM ** -0.5
    kvw = ATTN_KV_HEADS * HEAD_DIM
    in_kv = ATTN_W
    in_rq = in_kv + 2 * kvw
    stages = [
        (0, ATTN_W, lambda y: rope_store(head_norm(y, qnw_ref[...]), C_AQ, scale * LOG2_E)),
        (in_kv, 2 * kvw, kv_finish),
        (in_rq, RET_W, lambda y: rope_store(y, C_RQ, 1.0)),
        (in_rq + RET_W, RET_W, lambda y: rope_store(y, C_RK, scale)),
        (in_rq + 2 * RET_W, RET_W, lambda y: plain_store(y, C_RV)),
        (in_rq + 3 * RET_W, RET_W, lambda y: plain_store(y, C_RG)),
    ]
    pending = proj(stages[0][0], stages[0][1])
    for i, (_, _, finish) in enumerate(stages):
        following = proj(stages[i + 1][0], stages[i + 1][1]) if i + 1 < len(stages) else None
        finish(pending)
        pending = following


def _proj_call(x2, anw, w_in, qnw, knw, cos, sina, sinb, bd, seq):
    tokens, d_model = x2.shape
    tm = PROJ_TM
    pos_blocks = seq // tm
    row = lambda i: (i, 0)
    pos = lambda i: (i % pos_blocks, 0)
    return pl.pallas_call(
        _proj_kernel,
        grid=(tokens // tm,),
        in_specs=[
            pl.BlockSpec((tm, d_model), row),
            _resident((1, d_model)),
            _resident(w_in.shape),
            _resident((1, ATTN_W)),
            _resident((1, LANES)),
            pl.BlockSpec((tm, LANES), pos),
            pl.BlockSpec((tm, LANES), pos),
            pl.BlockSpec((tm, LANES), pos),
            _resident((ATTN_W, ATTN_W)),
        ],
        out_specs=pl.BlockSpec((tm, PROJ_W), row),
        out_shape=jax.ShapeDtypeStruct((tokens, PROJ_W), BF16),
        compiler_params=pltpu.CompilerParams(
            dimension_semantics=("arbitrary",), vmem_limit_bytes=VMEM_LIMIT),
        name="proj",
    )(x2, anw, w_in, qnw, knw, cos, sina, sinb, bd)


def _state_kernel(kvf_ref, kvb_ref, zf_ref, zb_ref, gf_ref, gb_ref, bdm_ref,
                  rf_ref, rb_ref, cf_ref, cb_ref):
    @pl.when(pl.program_id(0) == 0)
    def _():
        cf_ref[...] = jnp.zeros_like(cf_ref)
        cb_ref[...] = jnp.zeros_like(cb_ref)

    bdm = bdm_ref[...]
    batch = kvf_ref.shape[0]
    fwd_order = list(range(STATE_CH))
    for kv_ref, z_ref, g_ref, r_ref, c_ref, order in (
            (kvf_ref, zf_ref, gf_ref, rf_ref, cf_ref, fwd_order),
            (kvb_ref, zb_ref, gb_ref, rb_ref, cb_ref, fwd_order[::-1])):
        for b in range(batch):
            for p in range(PAIRS):
                sl = slice(p * LANES, (p + 1) * LANES)
                vsl = slice(RET_W + p * LANES, RET_W + (p + 1) * LANES)
                carry = c_ref[b, p]
                for c in order:
                    rows = slice(c * CHUNK, (c + 1) * CHUNK)
                    kz = (kv_ref[b, rows, sl].astype(F32) * z_ref[:, sl]).astype(BF16)
                    kv = _dot_tn(kz, kv_ref[b, rows, vsl])
                    r_ref[b, c, p] = carry.astype(BF16)
                    carry = g_ref[p] * carry + kv * bdm
                c_ref[b, p] = carry


def _state_call(proj3, zf, zb, gf, gb, bdm):
    batch, seq, _ = proj3.shape
    ns = seq // (CHUNK * STATE_CH)
    assert C_RK % (2 * RET_W) == 0 and C_RV == C_RK + RET_W
    kv_col = C_RK // (2 * RET_W)
    blk = (batch, CHUNK * STATE_CH, 2 * RET_W)
    out_blk = (batch, STATE_CH, PAIRS, LANES, LANES)
    out_sds = jax.ShapeDtypeStruct((batch, seq // CHUNK, PAIRS, LANES, LANES), BF16)
    return pl.pallas_call(
        _state_kernel,
        grid=(ns,),
        in_specs=[
            pl.BlockSpec(blk, lambda s: (0, s, kv_col)),
            pl.BlockSpec(blk, lambda s: (0, ns - 1 - s, kv_col)),
            _resident((CHUNK, RET_W)),
            _resident((CHUNK, RET_W)),
            _resident((PAIRS, LANES, LANES)),
            _resident((PAIRS, LANES, LANES)),
            _resident((LANES, LANES)),
        ],
        out_specs=[
            pl.BlockSpec(out_blk, lambda s: (0, s, 0, 0, 0)),
            pl.BlockSpec(out_blk, lambda s: (0, ns - 1 - s, 0, 0, 0)),
        ],
        out_shape=[out_sds, out_sds],
        scratch_shapes=[pltpu.VMEM((batch, PAIRS, LANES, LANES), F32),
                        pltpu.VMEM((batch, PAIRS, LANES, LANES), F32)],
        compiler_params=pltpu.CompilerParams(
            dimension_semantics=("arbitrary",), vmem_limit_bytes=VMEM_LIMIT),
        name="ret_state",
    )(proj3, proj3, zf, zb, gf, gb, bdm)


def _mix_kernel(sink_ref, p_ref, prev_ref, next_ref, rf_ref, rb_ref, x_ref, wo_ref,
                bias_ref, dm_ref, xif_ref, xib_ref, rnw_ref, o_ref, y_ref):
    lane = lax.broadcasted_iota(jnp.int32, (CHUNK, LANES), 1).astype(F32).astype(BF16)
    lo = lane < HEAD_DIM
    hi = jnp.logical_not(lo)
    lo_f32 = lax.broadcasted_iota(jnp.int32, (CHUNK, LANES), 1) < HEAD_DIM
    zero = jnp.zeros((), BF16)

    def keep(mask, a):
        reps = a.shape[0] // CHUNK
        m = mask if reps == 1 else jnp.concatenate([mask] * reps, axis=0)
        return jnp.where(m, a, zero)

    tile = pl.program_id(1)
    n_sub = p_ref.shape[0] // CHUNK
    group = ATTN_HEADS // ATTN_KV_HEADS
    k2_cols = slice(C_K2, C_K2 + KV2_W)
    v2_cols = slice(C_V2, C_V2 + KV2_W)

    def kv_window(j):
        if j == 0:
            inner = slice(0, 2 * CHUNK)
            k3 = jnp.concatenate([prev_ref[:, :KV2_W], p_ref[inner, k2_cols]], axis=0)
            v3 = jnp.concatenate([prev_ref[:, KV2_W:], p_ref[inner, v2_cols]], axis=0)
            bias = bias_ref[jnp.where(tile == 0, 0, 1)]
        elif j == n_sub - 1:
            inner = slice((j - 1) * CHUNK, (j + 1) * CHUNK)
            k3 = jnp.concatenate([p_ref[inner, k2_cols], next_ref[:, :KV2_W]], axis=0)
            v3 = jnp.concatenate([p_ref[inner, v2_cols], next_ref[:, KV2_W:]], axis=0)
            bias = bias_ref[jnp.where(tile == pl.num_programs(1) - 1, 2, 1)]
        else:
            inner = slice((j - 1) * CHUNK, (j + 2) * CHUNK)
            k3 = p_ref[inner, k2_cols]
            v3 = p_ref[inner, v2_cols]
            bias = bias_ref[1]
        return k3, v3, bias

    def attn_scores(j, g):
        k3, v3, bias_t = kv_window(j)
        g_low = slice(g * LANES, (g + 1) * LANES)
        g_high = slice((1 - g) * LANES, (2 - g) * LANES)
        kg = jnp.where(jnp.concatenate([lo] * 3, axis=0), k3[:, g_low], k3[:, g_high])
        qs = []
        for i in range(group):
            h = g * group + i
            qp = p_ref[j * CHUNK:(j + 1) * CHUNK,
                       C_AQ + (h // 2) * LANES:C_AQ + (h // 2 + 1) * LANES]
            qs.append(keep(lo if h % 2 == 0 else hi, qp))
        s_t = _dot_nt(kg, jnp.concatenate(qs, axis=0))
        return s_t, v3[:, :LANES], bias_t

    def attn_softmax(g, s_t, v_ab, bias_t):
        ps, rs = [], []
        for i in range(group):
            h = g * group + i
            s = s_t[:, i * CHUNK:(i + 1) * CHUNK] + bias_t
            m = jnp.max(s, axis=0, keepdims=True)
            e = jnp.exp2(s - m)
            den = jnp.sum(e, axis=0, keepdims=True) + jnp.exp2(sink_ref[h] * LOG2_E - m)
            ps.append(e.astype(BF16))
            rs.append(1.0 / den)
        return ps, rs, v_ab

    def attn_output(j, g, ps, rs, v_ab):
        for pp in range(group // 2):
            o_t = _dot_tn(v_ab, jnp.concatenate(ps[2 * pp:2 * pp + 2], axis=1))
            o_t = o_t[g * HEAD_DIM:(g + 1) * HEAD_DIM]
            halves = [o_t[:, i * CHUNK:(i + 1) * CHUNK] * rs[2 * pp + i] for i in range(2)]
            pair = g * (group // 2) + pp
            y_ref[j * CHUNK:(j + 1) * CHUNK, pair * LANES:(pair + 1) * LANES] = (
                jnp.concatenate(halves, axis=0).T)

    def retention_pair(j, p):
        rows = slice(j * CHUNK, (j + 1) * CHUNK)
        sl = slice(p * LANES, (p + 1) * LANES)
        col = lambda c0: slice(c0 + p * LANES, c0 + (p + 1) * LANES)
        qp, kp, vp = p_ref[rows, col(C_RQ)], p_ref[rows, col(C_RK)], p_ref[rows, col(C_RV)]
        s2 = _dot_nt(jnp.concatenate([keep(lo, qp), keep(hi, qp)], axis=0), kp)
        p0 = (s2[:CHUNK] * dm_ref[2 * p]).astype(BF16)
        p1 = (s2[CHUNK:] * dm_ref[2 * p + 1]).astype(BF16)
        qf = (qp.astype(F32) * xif_ref[:, sl]).astype(BF16)
        qb = (qp.astype(F32) * xib_ref[:, sl]).astype(BF16)
        lhs = jnp.concatenate([p0, p1, qf, qb], axis=1)
        rhs = jnp.concatenate([keep(lo, vp), keep(hi, vp), rf_ref[j, p], rb_ref[j, p]], axis=0)
        ret = _dot(lhs, rhs)
        sq = ret * ret
        ss_lo = jnp.sum(jnp.where(lo_f32, sq, 0.0), axis=-1, keepdims=True)
        ss_hi = jnp.sum(jnp.where(lo_f32, 0.0, sq), axis=-1, keepdims=True)
        ms = jnp.where(lo_f32, ss_lo, ss_hi) * (1.0 / HEAD_DIM)
        ret_n = ret * lax.rsqrt(ms + EPS) * rnw_ref[:, sl]
        gate = p_ref[rows, col(C_RG)].astype(F32)
        y_ref[rows, ATTN_W + p * LANES:ATTN_W + (p + 1) * LANES] = (
            gate * (1.0 / (1.0 + jnp.exp(-gate))) * ret_n)

    units = [(j, g) for j in range(n_sub) for g in range(ATTN_KV_HEADS)]
    ret_items = [(j, p) for j in range(n_sub) for p in range(PAIRS)]
    assert len(units) * RET_PAIRS_PER_UNIT >= len(ret_items)
    n_units = len(units)
    scores = {u: attn_scores(*units[u]) for u in range(min(2, n_units))}
    probs = {0: attn_softmax(units[0][1], *scores.pop(0))}
    for u, (j, g) in enumerate(units):
        if u + 2 < n_units:
            scores[u + 2] = attn_scores(*units[u + 2])
        for item in ret_items[u * RET_PAIRS_PER_UNIT:(u + 1) * RET_PAIRS_PER_UNIT]:
            retention_pair(*item)
        if u + 1 < n_units:
            probs[u + 1] = attn_softmax(units[u + 1][1], *scores.pop(u + 1))
        attn_output(j, g, *probs.pop(u))
    o_ref[...] = x_ref[...] + _dot(y_ref[...], wo_ref[...])


def _mix_call(sink, proj3, rf, rb, x3, w_out, bias, dmat, xif, xib, rnw):
    batch, seq, d_model = x3.shape
    tm = MIX_TM
    n_sub = tm // CHUNK
    nt = seq // tm
    nc = seq // CHUNK
    kv_cols = C_K2 // (2 * KV2_W)
    assert C_K2 % (2 * KV2_W) == 0 and C_V2 == C_K2 + KV2_W
    kv_prev = pl.BlockSpec(
        (None, CHUNK, 2 * KV2_W), lambda b, t: (b, jnp.maximum(t * n_sub - 1, 0), kv_cols))
    kv_next = pl.BlockSpec(
        (None, CHUNK, 2 * KV2_W), lambda b, t: (b, jnp.minimum((t + 1) * n_sub, nc - 1), kv_cols))
    st = pl.BlockSpec((None, n_sub, PAIRS, LANES, LANES), lambda b, t: (b, t, 0, 0, 0))
    return pl.pallas_call(
        _mix_kernel,
        grid=(batch, nt),
        in_specs=[
            pl.BlockSpec(memory_space=pltpu.SMEM),
            pl.BlockSpec((None, tm, PROJ_W), lambda b, t: (b, t, 0)),
            kv_prev, kv_next,
            st, st,
            pl.BlockSpec((None, tm, d_model), lambda b, t: (b, t, 0)),
            _resident((d_model, d_model)),
            _resident((3, 3 * CHUNK, CHUNK)),
            _resident((RET_HEADS, CHUNK, CHUNK)),
            _resident((CHUNK, RET_W)),
            _resident((CHUNK, RET_W)),
            _resident((1, RET_W)),
        ],
        out_specs=pl.BlockSpec((None, tm, d_model), lambda b, t: (b, t, 0)),
        out_shape=jax.ShapeDtypeStruct((batch, seq, d_model), F32),
        scratch_shapes=[pltpu.VMEM((tm, d_model), F32)],
        compiler_params=pltpu.CompilerParams(
            dimension_semantics=("arbitrary", "arbitrary"), vmem_limit_bytes=VMEM_LIMIT),
        name="mix",
    )(sink, proj3, proj3, proj3, rf, rb, x3, w_out, bias, dmat, xif, xib, rnw)


def _ffn_kernel(h_ref, fnw_ref, wg_ref, wu_ref, wd_ref, o_ref, a_ref):
    h = h_ref[...]
    ms = jnp.mean(h * h, axis=-1, keepdims=True)
    m = h * lax.rsqrt(ms + EPS) * fnw_ref[...]
    d_ff = wg_ref.shape[1]
    for c0 in range(0, d_ff, FFN_CK):
        c1 = min(c0 + FFN_CK, d_ff)
        g = _dot(m, wg_ref[:, c0:c1])
        u = _dot(m, wu_ref[:, c0:c1])
        a_ref[:, c0:c1] = g * (1.0 / (1.0 + jnp.exp(-g))) * u
    o_ref[...] = h + _dot(a_ref[...], wd_ref[...])


def _ffn_call(h2, fnw, wg, wu, wd):
    tokens, d_model = h2.shape
    d_ff = wg.shape[1]
    tm = FFN_TM
    row = lambda i: (i, 0)
    return pl.pallas_call(
        _ffn_kernel,
        grid=(tokens // tm,),
        in_specs=[
            pl.BlockSpec((tm, d_model), row),
            _resident((1, d_model)),
            _resident((d_model, d_ff)),
            _resident((d_model, d_ff)),
            _resident((d_ff, d_model)),
        ],
        out_specs=pl.BlockSpec((tm, d_model), row),
        out_shape=jax.ShapeDtypeStruct((tokens, d_model), F32),
        scratch_shapes=[pltpu.VMEM((tm, d_ff), F32)],
        compiler_params=pltpu.CompilerParams(
            dimension_semantics=("arbitrary",), vmem_limit_bytes=VMEM_LIMIT),
        name="ffn",
    )(h2, fnw, wg, wu, wd)


def _rope_tables(seq):
    inv_freq = ROPE_THETA ** (-np.arange(0, HEAD_DIM, 2, dtype=np.float64) / HEAD_DIM)
    ang = np.arange(seq, dtype=np.float64)[:, None] * inv_freq[None, :]
    cos, sin = np.cos(ang), np.sin(ang)
    zeros = np.zeros_like(sin)
    reps = LANES // HEAD_DIM
    cos_t = np.tile(np.concatenate([cos, cos], -1), (1, reps))
    sina_t = np.tile(np.concatenate([-sin, zeros], -1), (1, reps))
    sinb_t = np.tile(np.concatenate([zeros, sin], -1), (1, reps))
    return tuple(jnp.asarray(t.astype(np.float32)) for t in (cos_t, sina_t, sinb_t))


def _attn_bias():
    i = np.arange(CHUNK)[:, None]
    j = np.arange(3 * CHUNK)[None, :]
    band = np.abs(i + CHUNK - j) <= CHUNK
    first = band & (j >= CHUNK)
    last = band & (j < 2 * CHUNK)
    tab = np.stack([first, band, last]).transpose(0, 2, 1)
    return jnp.asarray(np.where(tab, 0.0, NEG_INF).astype(np.float32))


def _retention_tables(log_f, log_b):
    idx = jnp.arange(CHUNK, dtype=F32)
    diff = idx[:, None] - idx[None, :]
    lf, lb = log_f[:, None, None], log_b[:, None, None]
    dmat = jnp.where(diff[None] >= 0,
                     jnp.exp(lf * jnp.maximum(diff, 0.0)[None]),
                     jnp.exp(lb * jnp.maximum(-diff, 0.0)[None]))
    per_lane = lambda t: jnp.repeat(t.T, HEAD_DIM, axis=1)
    xif = per_lane(jnp.exp(log_f[:, None] * (idx + 1.0)[None]))
    xib = per_lane(jnp.exp(log_b[:, None] * (CHUNK - idx)[None]))
    zf = per_lane(jnp.exp(log_f[:, None] * (CHUNK - 1.0 - idx)[None]))
    zb = per_lane(jnp.exp(log_b[:, None] * idx[None]))
    per_row = lambda g: jnp.broadcast_to(
        jnp.repeat(g.reshape(PAIRS, 2), HEAD_DIM, axis=1)[:, :, None], (PAIRS, LANES, LANES))
    gf = per_row(jnp.exp(log_f * CHUNK))
    gb = per_row(jnp.exp(log_b * CHUNK))
    return dmat, xif, xib, zf, zb, gf, gb


def _block_diag_ones(width, dtype):
    r = np.arange(width) // HEAD_DIM
    return jnp.asarray((r[:, None] == r[None, :]).astype(np.float32), dtype=dtype)


def kernel(x, attn_norm_w, w_in, q_norm_w, k_norm_w, attn_sink, ret_log_decay_fwd,
           ret_log_decay_bwd, ret_norm_w, w_out, ffn_norm_w, w_gate, w_up, w_down):
    batch, seq, d_model = x.shape
    depth = w_in.shape[0]
    assert seq % PROJ_TM == 0 and seq % MIX_TM == 0 and (batch * seq) % FFN_TM == 0
    assert MIX_TM // CHUNK >= 2

    cos_t, sina_t, sinb_t = _rope_tables(seq)
    bias = _attn_bias()
    bd = _block_diag_ones(RET_W, BF16)
    bdm = _block_diag_ones(LANES, F32)

    h = x
    for l in range(depth):
        qnw = jnp.tile(q_norm_w[l], ATTN_HEADS)[None, :]
        knw = jnp.tile(k_norm_w[l], ATTN_KV_HEADS)[None, :]
        log_f = -jnp.abs(ret_log_decay_fwd[l].astype(F32))
        log_b = -jnp.abs(ret_log_decay_bwd[l].astype(F32))
        dmat, xif, xib, zf, zb, gf, gb = _retention_tables(log_f, log_b)

        proj = _proj_call(h.reshape(batch * seq, d_model), attn_norm_w[l][None, :],
                          w_in[l], qnw, knw, cos_t, sina_t, sinb_t, bd, seq)
        proj3 = proj.reshape(batch, seq, PROJ_W)
        rf, rb = _state_call(proj3, zf, zb, gf, gb, bdm)
        h = _mix_call(attn_sink[l].astype(F32), proj3, rf, rb, h, w_out[l],
                      bias, dmat, xif, xib, ret_norm_w[l][None, :])
        h = _ffn_call(h.reshape(batch * seq, d_model), ffn_norm_w[l][None, :],
                      w_gate[l], w_up[l], w_down[l]).reshape(batch, seq, d_model)
    return h
```

```python
import jax
import jax.numpy as jnp
import numpy as np
from jax import lax
from jax.experimental import pallas as pl
from jax.experimental.pallas import tpu as pltpu

HEAD_DIM = 64
ATTN_HEADS = 8
ATTN_KV_HEADS = 2
RET_HEADS = 8
CHUNK = 128
ROPE_THETA = 10000.0
EPS = 1e-6
NEG_INF = -1e30
LOG2_E = 1.4426950408889634

LANES = 128
PAIRS = RET_HEADS // 2
ATTN_W = ATTN_HEADS * HEAD_DIM
RET_W = RET_HEADS * HEAD_DIM
KV2_W = 2 * ATTN_KV_HEADS * HEAD_DIM

C_AQ = 0
C_K2 = C_AQ + ATTN_W
C_V2 = C_K2 + KV2_W
C_RK = C_V2 + KV2_W
C_RV = C_RK + RET_W
C_RQ = C_RV + RET_W
C_RG = C_RQ + RET_W
PROJ_W = C_RG + RET_W

PROJ_TM = 512
MIX_TM = 512
RET_PAIRS_PER_UNIT = 2
STATE_CH = 2
FFN_TM = 512
FFN_CK = 512
VMEM_LIMIT = 56 * 1024 * 1024

BF16 = jnp.bfloat16
F32 = jnp.float32


def _dot(a, b):
    return jnp.dot(a, b, preferred_element_type=F32)


def _dot_nt(a, b):
    return lax.dot_general(a, b, (((1,), (1,)), ((), ())), preferred_element_type=F32)


def _dot_tn(a, b):
    return lax.dot_general(a, b, (((0,), (0,)), ((), ())), preferred_element_type=F32)


def _resident(shape):
    zeros = (0,) * len(shape)
    return pl.BlockSpec(shape, lambda *_: zeros, pipeline_mode=pl.Buffered(1))


def _proj_kernel(x_ref, anw_ref, w_ref, qnw_ref, knw_ref, cos_ref, sina_ref, sinb_ref,
                 bd_ref, o_ref):
    x = x_ref[...]
    ms = jnp.mean(x * x, axis=-1, keepdims=True)
    n = x * lax.rsqrt(ms + EPS) * anw_ref[...]
    cos = cos_ref[...]
    sina = sina_ref[...]
    sinb = sinb_ref[...]

    def proj(c0, width):
        return _dot(n, w_ref[:, c0:c0 + width])

    def head_norm(y, w):
        width = y.shape[-1]
        ss = _dot((y * y).astype(BF16), bd_ref[:width, :width])
        return y * lax.rsqrt(ss * (1.0 / HEAD_DIM) + EPS) * w

    def rope(ys):
        return ys * cos + pltpu.roll(ys, LANES - 32, 1) * sina + pltpu.roll(ys, 32, 1) * sinb

    def rope_store(y, c0, scale):
        for s in range(y.shape[-1] // LANES):
            r = rope(y[:, s * LANES:(s + 1) * LANES])
            if scale != 1.0:
                r = r * scale
            o_ref[:, c0 + s * LANES:c0 + (s + 1) * LANES] = r.astype(BF16)

    def plain_store(y, c0):
        o_ref[:, c0:c0 + y.shape[-1]] = y.astype(BF16)

    def store_both_orders(y, c0):
        o_ref[:, c0:c0 + LANES] = y.astype(BF16)
        o_ref[:, c0 + LANES:c0 + 2 * LANES] = pltpu.roll(y, HEAD_DIM, 1).astype(BF16)

    def kv_finish(y):
        store_both_orders(rope(head_norm(y[:, :LANES], knw_ref[...])), C_K2)
        store_both_orders(y[:, LANES:], C_V2)

    scale = HEAD_DIM ** -0.5
    kvw = ATTN_KV_HEADS * HEAD_DIM
    in_kv = ATTN_W
    in_rq = in_kv + 2 * kvw
    stages = [
        (0, ATTN_W, lambda y: rope_store(head_norm(y, qnw_ref[...]), C_AQ, scale * LOG2_E)),
        (in_kv, 2 * kvw, kv_finish),
        (in_rq, RET_W, lambda y: rope_store(y, C_RQ, 1.0)),
        (in_rq + RET_W, RET_W, lambda y: rope_store(y, C_RK, scale)),
        (in_rq + 2 * RET_W, RET_W, lambda y: plain_store(y, C_RV)),
        (in_rq + 3 * RET_W, RET_W, lambda y: plain_store(y, C_RG)),
    ]
    pending = proj(stages[0][0], stages[0][1])
    for i, (_, _, finish) in enumerate(stages):
        following = proj(stages[i + 1][0], stages[i + 1][1]) if i + 1 < len(stages) else None
        finish(pending)
        pending = following


def _proj_call(x2, anw, w_in, qnw, knw, cos, sina, sinb, bd, seq):
    tokens, d_model = x2.shape
    tm = PROJ_TM
    pos_blocks = seq // tm
    row = lambda i: (i, 0)
    pos = lambda i: (i % pos_blocks, 0)
    return pl.pallas_call(
        _proj_kernel,
        grid=(tokens // tm,),
        in_specs=[
            pl.BlockSpec((tm, d_model), row),
            _resident((1, d_model)),
            _resident(w_in.shape),
            _resident((1, ATTN_W)),
            _resident((1, LANES)),
            pl.BlockSpec((tm, LANES), pos),
            pl.BlockSpec((tm, LANES), pos),
            pl.BlockSpec((tm, LANES), pos),
            _resident((ATTN_W, ATTN_W)),
        ],
        out_specs=pl.BlockSpec((tm, PROJ_W), row),
        out_shape=jax.ShapeDtypeStruct((tokens, PROJ_W), BF16),
        compiler_params=pltpu.CompilerParams(
            dimension_semantics=("arbitrary",), vmem_limit_bytes=VMEM_LIMIT),
        name="proj",
    )(x2, anw, w_in, qnw, knw, cos, sina, sinb, bd)


def _state_kernel(kvf_ref, kvb_ref, zf_ref, zb_ref, gf_ref, gb_ref, bdm_ref,
                  rf_ref, rb_ref, cf_ref, cb_ref):
    @pl.when(pl.program_id(0) == 0)
    def _():
        cf_ref[...] = jnp.zeros_like(cf_ref)
        cb_ref[...] = jnp.zeros_like(cb_ref)

    bdm = bdm_ref[...]
    batch = kvf_ref.shape[0]
    fwd_order = list(range(STATE_CH))
    for kv_ref, z_ref, g_ref, r_ref, c_ref, order in (
            (kvf_ref, zf_ref, gf_ref, rf_ref, cf_ref, fwd_order),
            (kvb_ref, zb_ref, gb_ref, rb_ref, cb_ref, fwd_order[::-1])):
        for b in range(batch):
            for p in range(PAIRS):
                sl = slice(p * LANES, (p + 1) * LANES)
                vsl = slice(RET_W + p * LANES, RET_W + (p + 1) * LANES)
                carry = c_ref[b, p]
                for c in order:
                    rows = slice(c * CHUNK, (c + 1) * CHUNK)
                    kz = (kv_ref[b, rows, sl].astype(F32) * z_ref[:, sl]).astype(BF16)
                    kv = _dot_tn(kz, kv_ref[b, rows, vsl])
                    r_ref[b, c, p] = carry.astype(BF16)
                    carry = g_ref[p] * carry + kv * bdm
                c_ref[b, p] = carry


def _state_call(proj3, zf, zb, gf, gb, bdm):
    batch, seq, _ = proj3.shape
    ns = seq // (CHUNK * STATE_CH)
    assert C_RK % (2 * RET_W) == 0 and C_RV == C_RK + RET_W
    kv_col = C_RK // (2 * RET_W)
    blk = (batch, CHUNK * STATE_CH, 2 * RET_W)
    out_blk = (batch, STATE_CH, PAIRS, LANES, LANES)
    out_sds = jax.ShapeDtypeStruct((batch, seq // CHUNK, PAIRS, LANES, LANES), BF16)
    return pl.pallas_call(
        _state_kernel,
        grid=(ns,),
        in_specs=[
            pl.BlockSpec(blk, lambda s: (0, s, kv_col)),
            pl.BlockSpec(blk, lambda s: (0, ns - 1 - s, kv_col)),
            _resident((CHUNK, RET_W)),
            _resident((CHUNK, RET_W)),
            _resident((PAIRS, LANES, LANES)),
            _resident((PAIRS, LANES, LANES)),
            _resident((LANES, LANES)),
        ],
        out_specs=[
            pl.BlockSpec(out_blk, lambda s: (0, s, 0, 0, 0)),
            pl.BlockSpec(out_blk, lambda s: (0, ns - 1 - s, 0, 0, 0)),
        ],
        out_shape=[out_sds, out_sds],
        scratch_shapes=[pltpu.VMEM((batch, PAIRS, LANES, LANES), F32),
                        pltpu.VMEM((batch, PAIRS, LANES, LANES), F32)],
        compiler_params=pltpu.CompilerParams(
            dimension_semantics=("arbitrary",), vmem_limit_bytes=VMEM_LIMIT),
        name="ret_state",
    )(proj3, proj3, zf, zb, gf, gb, bdm)


def _mix_kernel(sink_ref, p_ref, prev_ref, next_ref, rf_ref, rb_ref, x_ref, wo_ref,
                bias_ref, dm_ref, xif_ref, xib_ref, rnw_ref, o_ref, y_ref):
    lane = lax.broadcasted_iota(jnp.int32, (CHUNK, LANES), 1).astype(F32).astype(BF16)
    lo = lane < HEAD_DIM
    hi = jnp.logical_not(lo)
    lo_f32 = lax.broadcasted_iota(jnp.int32, (CHUNK, LANES), 1) < HEAD_DIM
    zero = jnp.zeros((), BF16)

    def keep(mask, a):
        reps = a.shape[0] // CHUNK
        m = mask if reps == 1 else jnp.concatenate([mask] * reps, axis=0)
        return jnp.where(m, a, zero)

    tile = pl.program_id(1)
    n_sub = p_ref.shape[0] // CHUNK
    group = ATTN_HEADS // ATTN_KV_HEADS
    k2_cols = slice(C_K2, C_K2 + KV2_W)
    v2_cols = slice(C_V2, C_V2 + KV2_W)

    def kv_window(j):
        if j == 0:
            inner = slice(0, 2 * CHUNK)
            k3 = jnp.concatenate([prev_ref[:, :KV2_W], p_ref[inner, k2_cols]], axis=0)
            v3 = jnp.concatenate([prev_ref[:, KV2_W:], p_ref[inner, v2_cols]], axis=0)
            bias = bias_ref[jnp.where(tile == 0, 0, 1)]
        elif j == n_sub - 1:
            inner = slice((j - 1) * CHUNK, (j + 1) * CHUNK)
            k3 = jnp.concatenate([p_ref[inner, k2_cols], next_ref[:, :KV2_W]], axis=0)
            v3 = jnp.concatenate([p_ref[inner, v2_cols], next_ref[:, KV2_W:]], axis=0)
            bias = bias_ref[jnp.where(tile == pl.num_programs(1) - 1, 2, 1)]
        else:
            inner = slice((j - 1) * CHUNK, (j + 2) * CHUNK)
            k3 = p_ref[inner, k2_cols]
            v3 = p_ref[inner, v2_cols]
            bias = bias_ref[1]
        return k3, v3, bias

    def attn_scores(j, g):
        k3, v3, bias = kv_window(j)
        g_low = slice(g * LANES, (g + 1) * LANES)
        g_high = slice((1 - g) * LANES, (2 - g) * LANES)
        kg = jnp.where(jnp.concatenate([lo] * 3, axis=0), k3[:, g_low], k3[:, g_high])
        v_lo = keep(lo, v3[:, g_low])
        v_hi = keep(hi, v3[:, g_high])
        qs = []
        for i in range(group):
            h = g * group + i
            qp = p_ref[j * CHUNK:(j + 1) * CHUNK,
                       C_AQ + (h // 2) * LANES:C_AQ + (h // 2 + 1) * LANES]
            qs.append(keep(lo if h % 2 == 0 else hi, qp))
        s_all = _dot_nt(jnp.concatenate(qs, axis=0), kg)
        return s_all, v_lo, v_hi, bias

    def attn_finish(j, g, s_all, v_lo, v_hi, bias):
        ps, rs = {}, {}

        def softmax_head(i):
            h = g * group + i
            s = s_all[i * CHUNK:(i + 1) * CHUNK] + bias
            sink = sink_ref[h] * LOG2_E
            m = jnp.maximum(jnp.max(s, axis=-1, keepdims=True), sink)
            e = jnp.exp2(s - m)
            den = jnp.sum(e, axis=-1, keepdims=True) + jnp.exp2(sink - m)
            ps[i] = e.astype(BF16)
            rs[i] = 1.0 / den

        softmax_head(0)
        softmax_head(2)
        o_lo = _dot(jnp.concatenate([ps[0], ps[2]], axis=0), v_lo)
        softmax_head(1)
        softmax_head(3)
        o_hi = _dot(jnp.concatenate([ps[1], ps[3]], axis=0), v_hi)
        for pp in range(group // 2):
            part = slice(pp * CHUNK, (pp + 1) * CHUNK)
            pair = g * (group // 2) + pp
            y_ref[j * CHUNK:(j + 1) * CHUNK, pair * LANES:(pair + 1) * LANES] = (
                o_lo[part] * rs[2 * pp] + o_hi[part] * rs[2 * pp + 1])

    def retention_pair(j, p):
        rows = slice(j * CHUNK, (j + 1) * CHUNK)
        sl = slice(p * LANES, (p + 1) * LANES)
        col = lambda c0: slice(c0 + p * LANES, c0 + (p + 1) * LANES)
        qp, kp, vp = p_ref[rows, col(C_RQ)], p_ref[rows, col(C_RK)], p_ref[rows, col(C_RV)]
        s2 = _dot_nt(jnp.concatenate([keep(lo, qp), keep(hi, qp)], axis=0), kp)
        p0 = (s2[:CHUNK] * dm_ref[2 * p]).astype(BF16)
        p1 = (s2[CHUNK:] * dm_ref[2 * p + 1]).astype(BF16)
        qf = (qp.astype(F32) * xif_ref[:, sl]).astype(BF16)
        qb = (qp.astype(F32) * xib_ref[:, sl]).astype(BF16)
        lhs = jnp.concatenate([p0, p1, qf, qb], axis=1)
        rhs = jnp.concatenate([keep(lo, vp), keep(hi, vp), rf_ref[j, p], rb_ref[j, p]], axis=0)
        ret = _dot(lhs, rhs)
        sq = ret * ret
        ss_lo = jnp.sum(jnp.where(lo_f32, sq, 0.0), axis=-1, keepdims=True)
        ss_hi = jnp.sum(jnp.where(lo_f32, 0.0, sq), axis=-1, keepdims=True)
        ms = jnp.where(lo_f32, ss_lo, ss_hi) * (1.0 / HEAD_DIM)
        ret_n = ret * lax.rsqrt(ms + EPS) * rnw_ref[:, sl]
        gate = p_ref[rows, col(C_RG)].astype(F32)
        y_ref[rows, ATTN_W + p * LANES:ATTN_W + (p + 1) * LANES] = (
            gate * (1.0 / (1.0 + jnp.exp(-gate))) * ret_n)

    units = [(j, g) for j in range(n_sub) for g in range(ATTN_KV_HEADS)]
    ret_items = [(j, p) for j in range(n_sub) for p in range(PAIRS)]
    assert len(units) * RET_PAIRS_PER_UNIT >= len(ret_items)
    pending = attn_scores(*units[0])
    for u, (j, g) in enumerate(units):
        following = attn_scores(*units[u + 1]) if u + 1 < len(units) else None
        for item in ret_items[u * RET_PAIRS_PER_UNIT:(u + 1) * RET_PAIRS_PER_UNIT]:
            retention_pair(*item)
        attn_finish(j, g, *pending)
        pending = following
    o_ref[...] = x_ref[...] + _dot(y_ref[...], wo_ref[...])


def _mix_call(sink, proj3, rf, rb, x3, w_out, bias, dmat, xif, xib, rnw):
    batch, seq, d_model = x3.shape
    tm = MIX_TM
    n_sub = tm // CHUNK
    nt = seq // tm
    nc = seq // CHUNK
    kv_cols = C_K2 // (2 * KV2_W)
    assert C_K2 % (2 * KV2_W) == 0 and C_V2 == C_K2 + KV2_W
    kv_prev = pl.BlockSpec(
        (None, CHUNK, 2 * KV2_W), lambda b, t: (b, jnp.maximum(t * n_sub - 1, 0), kv_cols))
    kv_next = pl.BlockSpec(
        (None, CHUNK, 2 * KV2_W), lambda b, t: (b, jnp.minimum((t + 1) * n_sub, nc - 1), kv_cols))
    st = pl.BlockSpec((None, n_sub, PAIRS, LANES, LANES), lambda b, t: (b, t, 0, 0, 0))
    return pl.pallas_call(
        _mix_kernel,
        grid=(batch, nt),
        in_specs=[
            pl.BlockSpec(memory_space=pltpu.SMEM),
            pl.BlockSpec((None, tm, PROJ_W), lambda b, t: (b, t, 0)),
            kv_prev, kv_next,
            st, st,
            pl.BlockSpec((None, tm, d_model), lambda b, t: (b, t, 0)),
            _resident((d_model, d_model)),
            _resident((3, CHUNK, 3 * CHUNK)),
            _resident((RET_HEADS, CHUNK, CHUNK)),
            _resident((CHUNK, RET_W)),
            _resident((CHUNK, RET_W)),
            _resident((1, RET_W)),
        ],
        out_specs=pl.BlockSpec((None, tm, d_model), lambda b, t: (b, t, 0)),
        out_shape=jax.ShapeDtypeStruct((batch, seq, d_model), F32),
        scratch_shapes=[pltpu.VMEM((tm, d_model), F32)],
        compiler_params=pltpu.CompilerParams(
            dimension_semantics=("arbitrary", "arbitrary"), vmem_limit_bytes=VMEM_LIMIT),
        name="mix",
    )(sink, proj3, proj3, proj3, rf, rb, x3, w_out, bias, dmat, xif, xib, rnw)


def _ffn_kernel(h_ref, fnw_ref, wg_ref, wu_ref, wd_ref, o_ref, a_ref):
    h = h_ref[...]
    ms = jnp.mean(h * h, axis=-1, keepdims=True)
    m = h * lax.rsqrt(ms + EPS) * fnw_ref[...]
    d_ff = wg_ref.shape[1]
    for c0 in range(0, d_ff, FFN_CK):
        c1 = min(c0 + FFN_CK, d_ff)
        g = _dot(m, wg_ref[:, c0:c1])
        u = _dot(m, wu_ref[:, c0:c1])
        a_ref[:, c0:c1] = g * (1.0 / (1.0 + jnp.exp(-g))) * u
    o_ref[...] = h + _dot(a_ref[...], wd_ref[...])


def _ffn_call(h2, fnw, wg, wu, wd):
    tokens, d_model = h2.shape
    d_ff = wg.shape[1]
    tm = FFN_TM
    row = lambda i: (i, 0)
    return pl.pallas_call(
        _ffn_kernel,
        grid=(tokens // tm,),
        in_specs=[
            pl.BlockSpec((tm, d_model), row),
            _resident((1, d_model)),
            _resident((d_model, d_ff)),
            _resident((d_model, d_ff)),
            _resident((d_ff, d_model)),
        ],
        out_specs=pl.BlockSpec((tm, d_model), row),
        out_shape=jax.ShapeDtypeStruct((tokens, d_model), F32),
        scratch_shapes=[pltpu.VMEM((tm, d_ff), F32)],
        compiler_params=pltpu.CompilerParams(
            dimension_semantics=("arbitrary",), vmem_limit_bytes=VMEM_LIMIT),
        name="ffn",
    )(h2, fnw, wg, wu, wd)


def _rope_tables(seq):
    inv_freq = ROPE_THETA ** (-np.arange(0, HEAD_DIM, 2, dtype=np.float64) / HEAD_DIM)
    ang = np.arange(seq, dtype=np.float64)[:, None] * inv_freq[None, :]
    cos, sin = np.cos(ang), np.sin(ang)
    zeros = np.zeros_like(sin)
    reps = LANES // HEAD_DIM
    cos_t = np.tile(np.concatenate([cos, cos], -1), (1, reps))
    sina_t = np.tile(np.concatenate([-sin, zeros], -1), (1, reps))
    sinb_t = np.tile(np.concatenate([zeros, sin], -1), (1, reps))
    return tuple(jnp.asarray(t.astype(np.float32)) for t in (cos_t, sina_t, sinb_t))


def _attn_bias():
    i = np.arange(CHUNK)[:, None]
    j = np.arange(3 * CHUNK)[None, :]
    band = np.abs(i + CHUNK - j) <= CHUNK
    first = band & (j >= CHUNK)
    last = band & (j < 2 * CHUNK)
    tab = np.stack([first, band, last])
    return jnp.asarray(np.where(tab, 0.0, NEG_INF).astype(np.float32))


def _retention_tables(log_f, log_b):
    idx = jnp.arange(CHUNK, dtype=F32)
    diff = idx[:, None] - idx[None, :]
    lf, lb = log_f[:, None, None], log_b[:, None, None]
    dmat = jnp.where(diff[None] >= 0,
                     jnp.exp(lf * jnp.maximum(diff, 0.0)[None]),
                     jnp.exp(lb * jnp.maximum(-diff, 0.0)[None]))
    per_lane = lambda t: jnp.repeat(t.T, HEAD_DIM, axis=1)
    xif = per_lane(jnp.exp(log_f[:, None] * (idx + 1.0)[None]))
    xib = per_lane(jnp.exp(log_b[:, None] * (CHUNK - idx)[None]))
    zf = per_lane(jnp.exp(log_f[:, None] * (CHUNK - 1.0 - idx)[None]))
    zb = per_lane(jnp.exp(log_b[:, None] * idx[None]))
    per_row = lambda g: jnp.broadcast_to(
        jnp.repeat(g.reshape(PAIRS, 2), HEAD_DIM, axis=1)[:, :, None], (PAIRS, LANES, LANES))
    gf = per_row(jnp.exp(log_f * CHUNK))
    gb = per_row(jnp.exp(log_b * CHUNK))
    return dmat, xif, xib, zf, zb, gf, gb


def _block_diag_ones(width, dtype):
    r = np.arange(width) // HEAD_DIM
    return jnp.asarray((r[:, None] == r[None, :]).astype(np.float32), dtype=dtype)


def kernel(x, attn_norm_w, w_in, q_norm_w, k_norm_w, attn_sink, ret_log_decay_fwd,
           ret_log_decay_bwd, ret_norm_w, w_out, ffn_norm_w, w_gate, w_up, w_down):
    batch, seq, d_model = x.shape
    depth = w_in.shape[0]
    assert seq % PROJ_TM == 0 and seq % MIX_TM == 0 and (batch * seq) % FFN_TM == 0
    assert MIX_TM // CHUNK >= 2

    cos_t, sina_t, sinb_t = _rope_tables(seq)
    bias = _attn_bias()
    bd = _block_diag_ones(RET_W, BF16)
    bdm = _block_diag_ones(LANES, F32)

    h = x
    for l in range(depth):
        qnw = jnp.tile(q_norm_w[l], ATTN_HEADS)[None, :]
        knw = jnp.tile(k_norm_w[l], ATTN_KV_HEADS)[None, :]
        log_f = -jnp.abs(ret_log_decay_fwd[l].astype(F32))
        log_b = -jnp.abs(ret_log_decay_bwd[l].astype(F32))
        dmat, xif, xib, zf, zb, gf, gb = _retention_tables(log_f, log_b)

        proj = _proj_call(h.reshape(batch * seq, d_model), attn_norm_w[l][None, :],
                          w_in[l], qnw, knw, cos_t, sina_t, sinb_t, bd, seq)
        proj3 = proj.reshape(batch, seq, PROJ_W)
        rf, rb = _state_call(proj3, zf, zb, gf, gb, bdm)
        h = _mix_call(attn_sink[l].astype(F32), proj3, rf, rb, h, w_out[l],
                      bias, dmat, xif, xib, ret_norm_w[l][None, :])
        h = _ffn_call(h.reshape(batch * seq, d_model), ffn_norm_w[l][None, :],
                      w_gate[l], w_up[l], w_down[l]).reshape(batch, seq, d_model)
    return h
```

```python
import jax
import jax.numpy as jnp
import numpy as np
from jax import lax
from jax.experimental import pallas as pl
from jax.experimental.pallas import tpu as pltpu

HEAD_DIM = 64
ATTN_HEADS = 8
ATTN_KV_HEADS = 2
RET_HEADS = 8
CHUNK = 128
ROPE_THETA = 10000.0
EPS = 1e-6
NEG_INF = -1e30
LOG2_E = 1.4426950408889634

LANES = 128
BF16_SUBLANES = 16
PAIRS = RET_HEADS // 2
ATTN_W = ATTN_HEADS * HEAD_DIM
RET_W = RET_HEADS * HEAD_DIM
KV2_W = 2 * ATTN_KV_HEADS * HEAD_DIM

C_AQ = 0
C_K2 = C_AQ + ATTN_W
C_V2 = C_K2 + KV2_W
C_RK = C_V2 + KV2_W
C_RV = C_RK + RET_W
C_RQ = C_RV + RET_W
C_RG = C_RQ + RET_W
PROJ_W = C_RG + RET_W

PROJ_TM = 512
MIX_TM = 512
RET_PAIRS_PER_UNIT = 2
STATE_CH = 2
FFN_TM = 1024
FFN_CK = 512
VMEM_LIMIT = 56 * 1024 * 1024

BF16 = jnp.bfloat16
F32 = jnp.float32


def _dot(a, b):
    return jnp.dot(a, b, preferred_element_type=F32)


def _dot_nt(a, b):
    return lax.dot_general(a, b, (((1,), (1,)), ((), ())), preferred_element_type=F32)


def _dot_tn(a, b):
    return lax.dot_general(a, b, (((0,), (0,)), ((), ())), preferred_element_type=F32)


def _resident(shape):
    zeros = (0,) * len(shape)
    return pl.BlockSpec(shape, lambda *_: zeros, pipeline_mode=pl.Buffered(1))


def _proj_kernel(x_ref, anw_ref, w_ref, qnw_ref, knw_ref, cos_ref, sina_ref, sinb_ref,
                 bd_ref, wg_ref, wu_ref, wd_ref, o_ref, wg_bf_ref, wu_bf_ref, wd_bf_ref):
    x = x_ref[...]
    ms = jnp.mean(x * x, axis=-1, keepdims=True)
    n = x * lax.rsqrt(ms + EPS) * anw_ref[...]
    cos = cos_ref[...]
    sina = sina_ref[...]
    sinb = sinb_ref[...]

    def proj(c0, width):
        return _dot(n, w_ref[:, c0:c0 + width])

    def head_norm(y, w):
        width = y.shape[-1]
        ss = _dot((y * y).astype(BF16), bd_ref[:width, :width])
        return y * lax.rsqrt(ss * (1.0 / HEAD_DIM) + EPS) * w

    def rope(ys):
        return ys * cos + pltpu.roll(ys, LANES - 32, 1) * sina + pltpu.roll(ys, 32, 1) * sinb

    def rope_store(y, c0, scale):
        for s in range(y.shape[-1] // LANES):
            r = rope(y[:, s * LANES:(s + 1) * LANES])
            if scale != 1.0:
                r = r * scale
            o_ref[:, c0 + s * LANES:c0 + (s + 1) * LANES] = r.astype(BF16)

    def plain_store(y, c0):
        o_ref[:, c0:c0 + y.shape[-1]] = y.astype(BF16)

    def store_both_orders(y, c0):
        o_ref[:, c0:c0 + LANES] = y.astype(BF16)
        o_ref[:, c0 + LANES:c0 + 2 * LANES] = pltpu.roll(y, HEAD_DIM, 1).astype(BF16)

    def kv_finish(y):
        store_both_orders(rope(head_norm(y[:, :LANES], knw_ref[...])), C_K2)
        store_both_orders(y[:, LANES:], C_V2)

    scale = HEAD_DIM ** -0.5
    kvw = ATTN_KV_HEADS * HEAD_DIM
    in_kv = ATTN_W
    in_rq = in_kv + 2 * kvw
    stages = [
        (0, ATTN_W, lambda y: rope_store(head_norm(y, qnw_ref[...]), C_AQ, scale * LOG2_E)),
        (in_kv, 2 * kvw, kv_finish),
        (in_rq, RET_W, lambda y: rope_store(y, C_RQ, 1.0)),
        (in_rq + RET_W, RET_W, lambda y: rope_store(y, C_RK, scale)),
        (in_rq + 2 * RET_W, RET_W, lambda y: plain_store(y, C_RV)),
        (in_rq + 3 * RET_W, RET_W, lambda y: plain_store(y, C_RG)),
    ]
    pending = proj(stages[0][0], stages[0][1])
    wg_bf_ref[...] = wg_ref[...].astype(BF16)
    wu_bf_ref[...] = wu_ref[...].astype(BF16)
    wd_bf_ref[...] = wd_ref[...].astype(BF16)
    for i, (_, _, finish) in enumerate(stages):
        following = proj(stages[i + 1][0], stages[i + 1][1]) if i + 1 < len(stages) else None
        finish(pending)
        pending = following


def _slab_spec(weight, n_steps):
    rows, cols = weight.shape
    n_slabs = n_steps
    while rows % n_slabs or (rows // n_slabs) % BF16_SUBLANES or n_steps % n_slabs:
        n_slabs -= 1
    hold = n_steps // n_slabs
    return pl.BlockSpec((rows // n_slabs, cols), lambda i: (i // hold, 0))


def _proj_call(x2, anw, w_in, qnw, knw, cos, sina, sinb, bd, seq, ffn_weights):
    tokens, d_model = x2.shape
    tm = PROJ_TM
    n_steps = tokens // tm
    pos_blocks = seq // tm
    row = lambda i: (i, 0)
    pos = lambda i: (i % pos_blocks, 0)
    slabs = [_slab_spec(w, n_steps) for w in ffn_weights]
    return pl.pallas_call(
        _proj_kernel,
        grid=(n_steps,),
        in_specs=[
            pl.BlockSpec((tm, d_model), row),
            _resident((1, d_model)),
            _resident(w_in.shape),
            _resident((1, ATTN_W)),
            _resident((1, LANES)),
            pl.BlockSpec((tm, LANES), pos),
            pl.BlockSpec((tm, LANES), pos),
            pl.BlockSpec((tm, LANES), pos),
            _resident((ATTN_W, ATTN_W)),
            *slabs,
        ],
        out_specs=[pl.BlockSpec((tm, PROJ_W), row), *slabs],
        out_shape=[jax.ShapeDtypeStruct((tokens, PROJ_W), BF16),
                   *[jax.ShapeDtypeStruct(w.shape, BF16) for w in ffn_weights]],
        compiler_params=pltpu.CompilerParams(
            dimension_semantics=("arbitrary",), vmem_limit_bytes=VMEM_LIMIT),
        name="proj",
    )(x2, anw, w_in, qnw, knw, cos, sina, sinb, bd, *ffn_weights)


def _state_kernel(kvf_ref, kvb_ref, zf_ref, zb_ref, gf_ref, gb_ref, bdm_ref,
                  rf_ref, rb_ref, cf_ref, cb_ref):
    @pl.when(pl.program_id(0) == 0)
    def _():
        cf_ref[...] = jnp.zeros_like(cf_ref)
        cb_ref[...] = jnp.zeros_like(cb_ref)

    bdm = bdm_ref[...]
    batch = kvf_ref.shape[0]
    fwd_order = list(range(STATE_CH))
    for kv_ref, z_ref, g_ref, r_ref, c_ref, order in (
            (kvf_ref, zf_ref, gf_ref, rf_ref, cf_ref, fwd_order),
            (kvb_ref, zb_ref, gb_ref, rb_ref, cb_ref, fwd_order[::-1])):
        for b in range(batch):
            for p in range(PAIRS):
                sl = slice(p * LANES, (p + 1) * LANES)
                vsl = slice(RET_W + p * LANES, RET_W + (p + 1) * LANES)
                carry = c_ref[b, p]
                for c in order:
                    rows = slice(c * CHUNK, (c + 1) * CHUNK)
                    kz = (kv_ref[b, rows, sl].astype(F32) * z_ref[:, sl]).astype(BF16)
                    kv = _dot_tn(kz, kv_ref[b, rows, vsl])
                    r_ref[b, c, p] = carry.astype(BF16)
                    carry = g_ref[p] * carry + kv * bdm
                c_ref[b, p] = carry


def _state_call(proj3, zf, zb, gf, gb, bdm):
    batch, seq, _ = proj3.shape
    ns = seq // (CHUNK * STATE_CH)
    assert C_RK % (2 * RET_W) == 0 and C_RV == C_RK + RET_W
    kv_col = C_RK // (2 * RET_W)
    blk = (batch, CHUNK * STATE_CH, 2 * RET_W)
    out_blk = (batch, STATE_CH, PAIRS, LANES, LANES)
    out_sds = jax.ShapeDtypeStruct((batch, seq // CHUNK, PAIRS, LANES, LANES), BF16)
    return pl.pallas_call(
        _state_kernel,
        grid=(ns,),
        in_specs=[
            pl.BlockSpec(blk, lambda s: (0, s, kv_col)),
            pl.BlockSpec(blk, lambda s: (0, ns - 1 - s, kv_col)),
            _resident((CHUNK, RET_W)),
            _resident((CHUNK, RET_W)),
            _resident((PAIRS, LANES, LANES)),
            _resident((PAIRS, LANES, LANES)),
            _resident((LANES, LANES)),
        ],
        out_specs=[
            pl.BlockSpec(out_blk, lambda s: (0, s, 0, 0, 0)),
            pl.BlockSpec(out_blk, lambda s: (0, ns - 1 - s, 0, 0, 0)),
        ],
        out_shape=[out_sds, out_sds],
        scratch_shapes=[pltpu.VMEM((batch, PAIRS, LANES, LANES), F32),
                        pltpu.VMEM((batch, PAIRS, LANES, LANES), F32)],
        compiler_params=pltpu.CompilerParams(
            dimension_semantics=("arbitrary",), vmem_limit_bytes=VMEM_LIMIT),
        name="ret_state",
    )(proj3, proj3, zf, zb, gf, gb, bdm)


def _mix_kernel(sink_ref, p_ref, prev_ref, next_ref, rf_ref, rb_ref, x_ref, wo_ref,
                bias_ref, dm_ref, xif_ref, xib_ref, rnw_ref, o_ref, y_ref):
    lane = lax.broadcasted_iota(jnp.int32, (CHUNK, LANES), 1).astype(F32).astype(BF16)
    lo = lane < HEAD_DIM
    hi = jnp.logical_not(lo)
    lo_f32 = lax.broadcasted_iota(jnp.int32, (CHUNK, LANES), 1) < HEAD_DIM
    zero = jnp.zeros((), BF16)

    def keep(mask, a):
        reps = a.shape[0] // CHUNK
        m = mask if reps == 1 else jnp.concatenate([mask] * reps, axis=0)
        return jnp.where(m, a, zero)

    tile = pl.program_id(1)
    n_sub = p_ref.shape[0] // CHUNK
    group = ATTN_HEADS // ATTN_KV_HEADS
    k2_cols = slice(C_K2, C_K2 + KV2_W)
    v2_cols = slice(C_V2, C_V2 + KV2_W)

    def kv_window(j):
        if j == 0:
            inner = slice(0, 2 * CHUNK)
            k3 = jnp.concatenate([prev_ref[:, :KV2_W], p_ref[inner, k2_cols]], axis=0)
            v3 = jnp.concatenate([prev_ref[:, KV2_W:], p_ref[inner, v2_cols]], axis=0)
            bias = bias_ref[jnp.where(tile == 0, 0, 1)]
        elif j == n_sub - 1:
            inner = slice((j - 1) * CHUNK, (j + 1) * CHUNK)
            k3 = jnp.concatenate([p_ref[inner, k2_cols], next_ref[:, :KV2_W]], axis=0)
            v3 = jnp.concatenate([p_ref[inner, v2_cols], next_ref[:, KV2_W:]], axis=0)
            bias = bias_ref[jnp.where(tile == pl.num_programs(1) - 1, 2, 1)]
        else:
            inner = slice((j - 1) * CHUNK, (j + 2) * CHUNK)
            k3 = p_ref[inner, k2_cols]
            v3 = p_ref[inner, v2_cols]
            bias = bias_ref[1]
        return k3, v3, bias

    def attn_scores(j, g):
        k3, v3, bias = kv_window(j)
        g_low = slice(g * LANES, (g + 1) * LANES)
        g_high = slice((1 - g) * LANES, (2 - g) * LANES)
        kg = jnp.where(jnp.concatenate([lo] * 3, axis=0), k3[:, g_low], k3[:, g_high])
        v_lo = keep(lo, v3[:, g_low])
        v_hi = keep(hi, v3[:, g_high])
        qs = []
        for i in range(group):
            h = g * group + i
            qp = p_ref[j * CHUNK:(j + 1) * CHUNK,
                       C_AQ + (h // 2) * LANES:C_AQ + (h // 2 + 1) * LANES]
            qs.append(keep(lo if h % 2 == 0 else hi, qp))
        s_all = _dot_nt(jnp.concatenate(qs, axis=0), kg)
        return s_all, v_lo, v_hi, bias

    def attn_finish(j, g, s_all, v_lo, v_hi, bias):
        ps, rs = {}, {}

        def softmax_head(i):
            h = g * group + i
            s = s_all[i * CHUNK:(i + 1) * CHUNK] + bias
            sink = sink_ref[h] * LOG2_E
            m = jnp.maximum(jnp.max(s, axis=-1, keepdims=True), sink)
            e = jnp.exp2(s - m)
            den = jnp.sum(e, axis=-1, keepdims=True) + jnp.exp2(sink - m)
            ps[i] = e.astype(BF16)
            rs[i] = 1.0 / den

        softmax_head(0)
        softmax_head(2)
        o_lo = _dot(jnp.concatenate([ps[0], ps[2]], axis=0), v_lo)
        softmax_head(1)
        softmax_head(3)
        o_hi = _dot(jnp.concatenate([ps[1], ps[3]], axis=0), v_hi)
        for pp in range(group // 2):
            part = slice(pp * CHUNK, (pp + 1) * CHUNK)
            pair = g * (group // 2) + pp
            y_ref[j * CHUNK:(j + 1) * CHUNK, pair * LANES:(pair + 1) * LANES] = (
                o_lo[part] * rs[2 * pp] + o_hi[part] * rs[2 * pp + 1])

    def retention_pair(j, p):
        rows = slice(j * CHUNK, (j + 1) * CHUNK)
        sl = slice(p * LANES, (p + 1) * LANES)
        col = lambda c0: slice(c0 + p * LANES, c0 + (p + 1) * LANES)
        qp, kp, vp = p_ref[rows, col(C_RQ)], p_ref[rows, col(C_RK)], p_ref[rows, col(C_RV)]
        s2 = _dot_nt(jnp.concatenate([keep(lo, qp), keep(hi, qp)], axis=0), kp)
        p0 = (s2[:CHUNK] * dm_ref[2 * p]).astype(BF16)
        p1 = (s2[CHUNK:] * dm_ref[2 * p + 1]).astype(BF16)
        qf = (qp.astype(F32) * xif_ref[:, sl]).astype(BF16)
        qb = (qp.astype(F32) * xib_ref[:, sl]).astype(BF16)
        lhs = jnp.concatenate([p0, p1, qf, qb], axis=1)
        rhs = jnp.concatenate([keep(lo, vp), keep(hi, vp), rf_ref[j, p], rb_ref[j, p]], axis=0)
        ret = _dot(lhs, rhs)
        sq = ret * ret
        ss_lo = jnp.sum(jnp.where(lo_f32, sq, 0.0), axis=-1, keepdims=True)
        ss_hi = jnp.sum(jnp.where(lo_f32, 0.0, sq), axis=-1, keepdims=True)
        ms = jnp.where(lo_f32, ss_lo, ss_hi) * (1.0 / HEAD_DIM)
        ret_n = ret * lax.rsqrt(ms + EPS) * rnw_ref[:, sl]
        gate = p_ref[rows, col(C_RG)].astype(F32)
        y_ref[rows, ATTN_W + p * LANES:ATTN_W + (p + 1) * LANES] = (
            gate * (1.0 / (1.0 + jnp.exp(-gate))) * ret_n)

    units = [(j, g) for j in range(n_sub) for g in range(ATTN_KV_HEADS)]
    ret_items = [(j, p) for j in range(n_sub) for p in range(PAIRS)]
    assert len(units) * RET_PAIRS_PER_UNIT >= len(ret_items)
    pending = attn_scores(*units[0])
    for u, (j, g) in enumerate(units):
        following = attn_scores(*units[u + 1]) if u + 1 < len(units) else None
        for item in ret_items[u * RET_PAIRS_PER_UNIT:(u + 1) * RET_PAIRS_PER_UNIT]:
            retention_pair(*item)
        attn_finish(j, g, *pending)
        pending = following
    o_ref[...] = x_ref[...] + _dot(y_ref[...], wo_ref[...])


def _mix_call(sink, proj3, rf, rb, x3, w_out, bias, dmat, xif, xib, rnw):
    batch, seq, d_model = x3.shape
    tm = MIX_TM
    n_sub = tm // CHUNK
    nt = seq // tm
    nc = seq // CHUNK
    kv_cols = C_K2 // (2 * KV2_W)
    assert C_K2 % (2 * KV2_W) == 0 and C_V2 == C_K2 + KV2_W
    kv_prev = pl.BlockSpec(
        (None, CHUNK, 2 * KV2_W), lambda b, t: (b, jnp.maximum(t * n_sub - 1, 0), kv_cols))
    kv_next = pl.BlockSpec(
        (None, CHUNK, 2 * KV2_W), lambda b, t: (b, jnp.minimum((t + 1) * n_sub, nc - 1), kv_cols))
    st = pl.BlockSpec((None, n_sub, PAIRS, LANES, LANES), lambda b, t: (b, t, 0, 0, 0))
    return pl.pallas_call(
        _mix_kernel,
        grid=(batch, nt),
        in_specs=[
            pl.BlockSpec(memory_space=pltpu.SMEM),
            pl.BlockSpec((None, tm, PROJ_W), lambda b, t: (b, t, 0)),
            kv_prev, kv_next,
            st, st,
            pl.BlockSpec((None, tm, d_model), lambda b, t: (b, t, 0)),
            _resident((d_model, d_model)),
            _resident((3, CHUNK, 3 * CHUNK)),
            _resident((RET_HEADS, CHUNK, CHUNK)),
            _resident((CHUNK, RET_W)),
            _resident((CHUNK, RET_W)),
            _resident((1, RET_W)),
        ],
        out_specs=pl.BlockSpec((None, tm, d_model), lambda b, t: (b, t, 0)),
        out_shape=jax.ShapeDtypeStruct((batch, seq, d_model), F32),
        scratch_shapes=[pltpu.VMEM((tm, d_model), F32)],
        compiler_params=pltpu.CompilerParams(
            dimension_semantics=("arbitrary", "arbitrary"), vmem_limit_bytes=VMEM_LIMIT),
        name="mix",
    )(sink, proj3, proj3, proj3, rf, rb, x3, w_out, bias, dmat, xif, xib, rnw)


def _ffn_kernel(h_ref, fnw_ref, wg_ref, wu_ref, wd_ref, o_ref, a_ref):
    h = h_ref[...]
    ms = jnp.mean(h * h, axis=-1, keepdims=True)
    m = (h * lax.rsqrt(ms + EPS) * fnw_ref[...]).astype(BF16)
    d_ff = wg_ref.shape[1]
    for c0 in range(0, d_ff, FFN_CK):
        c1 = min(c0 + FFN_CK, d_ff)
        g = _dot(m, wg_ref[:, c0:c1])
        u = _dot(m, wu_ref[:, c0:c1])
        a_ref[:, c0:c1] = (g * (1.0 / (1.0 + jnp.exp(-g))) * u).astype(BF16)
    o_ref[...] = h + _dot(a_ref[...], wd_ref[...])


def _ffn_call(h2, fnw, wg, wu, wd):
    tokens, d_model = h2.shape
    d_ff = wg.shape[1]
    tm = FFN_TM
    row = lambda i: (i, 0)
    return pl.pallas_call(
        _ffn_kernel,
        grid=(tokens // tm,),
        in_specs=[
            pl.BlockSpec((tm, d_model), row),
            _resident((1, d_model)),
            _resident((d_model, d_ff)),
            _resident((d_model, d_ff)),
            _resident((d_ff, d_model)),
        ],
        out_specs=pl.BlockSpec((tm, d_model), row),
        out_shape=jax.ShapeDtypeStruct((tokens, d_model), F32),
        scratch_shapes=[pltpu.VMEM((tm, d_ff), BF16)],
        compiler_params=pltpu.CompilerParams(
            dimension_semantics=("arbitrary",), vmem_limit_bytes=VMEM_LIMIT),
        name="ffn",
    )(h2, fnw, wg, wu, wd)


def _rope_tables(seq):
    inv_freq = ROPE_THETA ** (-np.arange(0, HEAD_DIM, 2, dtype=np.float64) / HEAD_DIM)
    ang = np.arange(seq, dtype=np.float64)[:, None] * inv_freq[None, :]
    cos, sin = np.cos(ang), np.sin(ang)
    zeros = np.zeros_like(sin)
    reps = LANES // HEAD_DIM
    cos_t = np.tile(np.concatenate([cos, cos], -1), (1, reps))
    sina_t = np.tile(np.concatenate([-sin, zeros], -1), (1, reps))
    sinb_t = np.tile(np.concatenate([zeros, sin], -1), (1, reps))
    return tuple(jnp.asarray(t.astype(np.float32)) for t in (cos_t, sina_t, sinb_t))


def _attn_bias():
    i = np.arange(CHUNK)[:, None]
    j = np.arange(3 * CHUNK)[None, :]
    band = np.abs(i + CHUNK - j) <= CHUNK
    first = band & (j >= CHUNK)
    last = band & (j < 2 * CHUNK)
    tab = np.stack([first, band, last])
    return jnp.asarray(np.where(tab, 0.0, NEG_INF).astype(np.float32))


def _retention_tables(log_f, log_b):
    idx = jnp.arange(CHUNK, dtype=F32)
    diff = idx[:, None] - idx[None, :]
    lf, lb = log_f[:, None, None], log_b[:, None, None]
    dmat = jnp.where(diff[None] >= 0,
                     jnp.exp(lf * jnp.maximum(diff, 0.0)[None]),
                     jnp.exp(lb * jnp.maximum(-diff, 0.0)[None]))
    per_lane = lambda t: jnp.repeat(t.T, HEAD_DIM, axis=1)
    xif = per_lane(jnp.exp(log_f[:, None] * (idx + 1.0)[None]))
    xib = per_lane(jnp.exp(log_b[:, None] * (CHUNK - idx)[None]))
    zf = per_lane(jnp.exp(log_f[:, None] * (CHUNK - 1.0 - idx)[None]))
    zb = per_lane(jnp.exp(log_b[:, None] * idx[None]))
    per_row = lambda g: jnp.broadcast_to(
        jnp.repeat(g.reshape(PAIRS, 2), HEAD_DIM, axis=1)[:, :, None], (PAIRS, LANES, LANES))
    gf = per_row(jnp.exp(log_f * CHUNK))
    gb = per_row(jnp.exp(log_b * CHUNK))
    return dmat, xif, xib, zf, zb, gf, gb


def _block_diag_ones(width, dtype):
    r = np.arange(width) // HEAD_DIM
    return jnp.asarray((r[:, None] == r[None, :]).astype(np.float32), dtype=dtype)


def kernel(x, attn_norm_w, w_in, q_norm_w, k_norm_w, attn_sink, ret_log_decay_fwd,
           ret_log_decay_bwd, ret_norm_w, w_out, ffn_norm_w, w_gate, w_up, w_down):
    batch, seq, d_model = x.shape
    depth = w_in.shape[0]
    assert seq % PROJ_TM == 0 and seq % MIX_TM == 0 and (batch * seq) % FFN_TM == 0
    assert MIX_TM // CHUNK >= 2

    cos_t, sina_t, sinb_t = _rope_tables(seq)
    bias = _attn_bias()
    bd = _block_diag_ones(RET_W, BF16)
    bdm = _block_diag_ones(LANES, F32)

    h = x
    for l in range(depth):
        qnw = jnp.tile(q_norm_w[l], ATTN_HEADS)[None, :]
        knw = jnp.tile(k_norm_w[l], ATTN_KV_HEADS)[None, :]
        log_f = -jnp.abs(ret_log_decay_fwd[l].astype(F32))
        log_b = -jnp.abs(ret_log_decay_bwd[l].astype(F32))
        dmat, xif, xib, zf, zb, gf, gb = _retention_tables(log_f, log_b)

        proj, wg_bf, wu_bf, wd_bf = _proj_call(
            h.reshape(batch * seq, d_model), attn_norm_w[l][None, :], w_in[l], qnw, knw,
            cos_t, sina_t, sinb_t, bd, seq, (w_gate[l], w_up[l], w_down[l]))
        proj3 = proj.reshape(batch, seq, PROJ_W)
        rf, rb = _state_call(proj3, zf, zb, gf, gb, bdm)
        h = _mix_call(attn_sink[l].astype(F32), proj3, rf, rb, h, w_out[l],
                      bias, dmat, xif, xib, ret_norm_w[l][None, :])
        h = _ffn_call(h.reshape(batch * seq, d_model), ffn_norm_w[l][None, :],
                      wg_bf, wu_bf, wd_bf).reshape(batch, seq, d_model)
    return h
```

```python
import functools

import jax
import jax.numpy as jnp
import numpy as np
from jax import lax
from jax.experimental import pallas as pl
from jax.experimental.pallas import tpu as pltpu

HEAD_DIM = 64
ATTN_HEADS = 8
ATTN_KV_HEADS = 2
RET_HEADS = 8
CHUNK = 128
ROPE_THETA = 10000.0
EPS = 1e-6
NEG_INF = -1e30
LOG2_E = 1.4426950408889634

LANES = 128
BF16_SUBLANES = 16
PAIRS = RET_HEADS // 2
ATTN_W = ATTN_HEADS * HEAD_DIM
RET_W = RET_HEADS * HEAD_DIM
KV2_W = 2 * ATTN_KV_HEADS * HEAD_DIM

C_AQ = 0
C_K2 = C_AQ + ATTN_W
C_V2 = C_K2 + KV2_W
C_RK = C_V2 + KV2_W
C_RV = C_RK + RET_W
C_RQ = C_RV + RET_W
C_RG = C_RQ + RET_W
PROJ_W = C_RG + RET_W

PROJ_TM = 512
MIX_TM = 512
RET_PAIRS_PER_UNIT = 2
FFN_TM = 1024
FFN_CK = 512
VMEM_LIMIT = 56 * 1024 * 1024

BF16 = jnp.bfloat16
F32 = jnp.float32


def _dot(a, b):
    return jnp.dot(a, b, preferred_element_type=F32)


def _dot_nt(a, b):
    return lax.dot_general(a, b, (((1,), (1,)), ((), ())), preferred_element_type=F32)


def _dot_tn(a, b):
    return lax.dot_general(a, b, (((0,), (0,)), ((), ())), preferred_element_type=F32)


def _resident(shape):
    zeros = (0,) * len(shape)
    return pl.BlockSpec(shape, lambda *_: zeros, pipeline_mode=pl.Buffered(1))


def _proj_kernel(tiles_per_seq, x_ref, anw_ref, w_ref, qnw_ref, knw_ref, cos_ref, sina_ref,
                 sinb_ref, bd_ref, wg_ref, wu_ref, wd_ref, zb_ref, gb_ref, bdm_ref,
                 o_ref, wg_bf_ref, wu_bf_ref, wd_bf_ref, rb_ref, carry_ref):
    @pl.when(pl.program_id(0) % tiles_per_seq == 0)
    def _():
        carry_ref[...] = jnp.zeros_like(carry_ref)

    x = x_ref[...]
    ms = jnp.mean(x * x, axis=-1, keepdims=True)
    n = x * lax.rsqrt(ms + EPS) * anw_ref[...]
    cos = cos_ref[...]
    sina = sina_ref[...]
    sinb = sinb_ref[...]

    def proj(c0, width):
        return _dot(n, w_ref[:, c0:c0 + width])

    def head_norm(y, w):
        width = y.shape[-1]
        ss = _dot((y * y).astype(BF16), bd_ref[:width, :width])
        return y * lax.rsqrt(ss * (1.0 / HEAD_DIM) + EPS) * w

    def rope(ys):
        return ys * cos + pltpu.roll(ys, LANES - 32, 1) * sina + pltpu.roll(ys, 32, 1) * sinb

    def rope_store(y, c0, scale):
        for s in range(y.shape[-1] // LANES):
            r = rope(y[:, s * LANES:(s + 1) * LANES])
            if scale != 1.0:
                r = r * scale
            o_ref[:, c0 + s * LANES:c0 + (s + 1) * LANES] = r.astype(BF16)

    def plain_store(y, c0):
        o_ref[:, c0:c0 + y.shape[-1]] = y.astype(BF16)

    def store_both_orders(y, c0):
        o_ref[:, c0:c0 + LANES] = y.astype(BF16)
        o_ref[:, c0 + LANES:c0 + 2 * LANES] = pltpu.roll(y, HEAD_DIM, 1).astype(BF16)

    def kv_finish(y):
        store_both_orders(rope(head_norm(y[:, :LANES], knw_ref[...])), C_K2)
        store_both_orders(y[:, LANES:], C_V2)

    scale = HEAD_DIM ** -0.5
    kvw = ATTN_KV_HEADS * HEAD_DIM
    in_kv = ATTN_W
    in_rq = in_kv + 2 * kvw
    def backward_states():
        bdm = bdm_ref[...]
        for p in range(PAIRS):
            sl = slice(p * LANES, (p + 1) * LANES)
            carry = carry_ref[p]
            for c in reversed(range(o_ref.shape[0] // CHUNK)):
                rows = slice(c * CHUNK, (c + 1) * CHUNK)
                k = o_ref[rows, C_RK + p * LANES:C_RK + (p + 1) * LANES].astype(F32) * zb_ref[:, sl]
                kv = _dot_tn(k.astype(BF16), o_ref[rows, C_RV + p * LANES:C_RV + (p + 1) * LANES])
                rb_ref[c, p] = carry.astype(BF16)
                carry = gb_ref[p] * carry + kv * bdm
            carry_ref[p] = carry

    def rq_finish(y):
        rope_store(y, C_RQ, 1.0)
        backward_states()

    stages = [
        (0, ATTN_W, lambda y: rope_store(head_norm(y, qnw_ref[...]), C_AQ, scale * LOG2_E)),
        (in_rq + RET_W, RET_W, lambda y: rope_store(y, C_RK, scale)),
        (in_rq + 2 * RET_W, RET_W, lambda y: plain_store(y, C_RV)),
        (in_kv, 2 * kvw, kv_finish),
        (in_rq, RET_W, rq_finish),
        (in_rq + 3 * RET_W, RET_W, lambda y: plain_store(y, C_RG)),
    ]
    pending = proj(stages[0][0], stages[0][1])
    wg_bf_ref[...] = wg_ref[...].astype(BF16)
    wu_bf_ref[...] = wu_ref[...].astype(BF16)
    wd_bf_ref[...] = wd_ref[...].astype(BF16)
    for i, (_, _, finish) in enumerate(stages):
        following = proj(stages[i + 1][0], stages[i + 1][1]) if i + 1 < len(stages) else None
        finish(pending)
        pending = following


def _slab_spec(weight, n_steps):
    rows, cols = weight.shape
    n_slabs = n_steps
    while rows % n_slabs or (rows // n_slabs) % BF16_SUBLANES or n_steps % n_slabs:
        n_slabs -= 1
    hold = n_steps // n_slabs
    return pl.BlockSpec((rows // n_slabs, cols), lambda i: (i // hold, 0))


def _proj_call(x2, anw, w_in, qnw, knw, cos, sina, sinb, bd, seq, ffn_weights, zb, gb, bdm):
    tokens, d_model = x2.shape
    tm = PROJ_TM
    n_steps = tokens // tm
    tiles_per_seq = seq // tm
    n_sub = tm // CHUNK
    row = lambda i: (n_steps - 1 - i, 0)
    pos = lambda i: ((n_steps - 1 - i) % tiles_per_seq, 0)
    slabs = [_slab_spec(w, n_steps) for w in ffn_weights]
    state_blk = (n_sub, PAIRS, LANES, LANES)
    return pl.pallas_call(
        functools.partial(_proj_kernel, tiles_per_seq),
        grid=(n_steps,),
        in_specs=[
            pl.BlockSpec((tm, d_model), row),
            _resident((1, d_model)),
            _resident(w_in.shape),
            _resident((1, ATTN_W)),
            _resident((1, LANES)),
            pl.BlockSpec((tm, LANES), pos),
            pl.BlockSpec((tm, LANES), pos),
            pl.BlockSpec((tm, LANES), pos),
            _resident((ATTN_W, ATTN_W)),
            *slabs,
            _resident((CHUNK, RET_W)),
            _resident((PAIRS, LANES, LANES)),
            _resident((LANES, LANES)),
        ],
        out_specs=[pl.BlockSpec((tm, PROJ_W), row), *slabs,
                   pl.BlockSpec(state_blk, lambda i: (n_steps - 1 - i, 0, 0, 0))],
        out_shape=[jax.ShapeDtypeStruct((tokens, PROJ_W), BF16),
                   *[jax.ShapeDtypeStruct(w.shape, BF16) for w in ffn_weights],
                   jax.ShapeDtypeStruct((tokens // CHUNK, PAIRS, LANES, LANES), BF16)],
        scratch_shapes=[pltpu.VMEM((PAIRS, LANES, LANES), F32)],
        compiler_params=pltpu.CompilerParams(
            dimension_semantics=("arbitrary",), vmem_limit_bytes=VMEM_LIMIT),
        name="proj",
    )(x2, anw, w_in, qnw, knw, cos, sina, sinb, bd, *ffn_weights, zb, gb, bdm)


def _mix_kernel(sink_ref, p_ref, prev_ref, next_ref, rb_ref, x_ref, wo_ref, bias_ref, dm_ref,
                xif_ref, xib_ref, rnw_ref, zf_ref, gf_ref, bdm_ref, o_ref, y_ref, carry_ref):
    lane = lax.broadcasted_iota(jnp.int32, (CHUNK, LANES), 1).astype(F32).astype(BF16)
    lo = lane < HEAD_DIM
    hi = jnp.logical_not(lo)
    lo_f32 = lax.broadcasted_iota(jnp.int32, (CHUNK, LANES), 1) < HEAD_DIM
    zero = jnp.zeros((), BF16)

    def keep(mask, a):
        reps = a.shape[0] // CHUNK
        m = mask if reps == 1 else jnp.concatenate([mask] * reps, axis=0)
        return jnp.where(m, a, zero)

    tile = pl.program_id(1)

    @pl.when(tile == 0)
    def _():
        carry_ref[...] = jnp.zeros_like(carry_ref)

    n_sub = p_ref.shape[0] // CHUNK
    group = ATTN_HEADS // ATTN_KV_HEADS
    k2_cols = slice(C_K2, C_K2 + KV2_W)
    v2_cols = slice(C_V2, C_V2 + KV2_W)

    def kv_window(j):
        if j == 0:
            inner = slice(0, 2 * CHUNK)
            k3 = jnp.concatenate([prev_ref[:, :KV2_W], p_ref[inner, k2_cols]], axis=0)
            v3 = jnp.concatenate([prev_ref[:, KV2_W:], p_ref[inner, v2_cols]], axis=0)
            bias = bias_ref[jnp.where(tile == 0, 0, 1)]
        elif j == n_sub - 1:
            inner = slice((j - 1) * CHUNK, (j + 1) * CHUNK)
            k3 = jnp.concatenate([p_ref[inner, k2_cols], next_ref[:, :KV2_W]], axis=0)
            v3 = jnp.concatenate([p_ref[inner, v2_cols], next_ref[:, KV2_W:]], axis=0)
            bias = bias_ref[jnp.where(tile == pl.num_programs(1) - 1, 2, 1)]
        else:
            inner = slice((j - 1) * CHUNK, (j + 2) * CHUNK)
            k3 = p_ref[inner, k2_cols]
            v3 = p_ref[inner, v2_cols]
            bias = bias_ref[1]
        return k3, v3, bias

    def attn_scores(j, g):
        k3, v3, bias = kv_window(j)
        g_low = slice(g * LANES, (g + 1) * LANES)
        g_high = slice((1 - g) * LANES, (2 - g) * LANES)
        kg = jnp.where(jnp.concatenate([lo] * 3, axis=0), k3[:, g_low], k3[:, g_high])
        v_lo = keep(lo, v3[:, g_low])
        v_hi = keep(hi, v3[:, g_high])
        qs = []
        for i in range(group):
            h = g * group + i
            qp = p_ref[j * CHUNK:(j + 1) * CHUNK,
                       C_AQ + (h // 2) * LANES:C_AQ + (h // 2 + 1) * LANES]
            qs.append(keep(lo if h % 2 == 0 else hi, qp))
        s_all = _dot_nt(jnp.concatenate(qs, axis=0), kg)
        return s_all, v_lo, v_hi, bias

    def attn_finish(j, g, s_all, v_lo, v_hi, bias):
        ps, rs = {}, {}

        def softmax_head(i):
            h = g * group + i
            s = s_all[i * CHUNK:(i + 1) * CHUNK] + bias
            sink = sink_ref[h] * LOG2_E
            m = jnp.maximum(jnp.max(s, axis=-1, keepdims=True), sink)
            e = jnp.exp2(s - m)
            den = jnp.sum(e, axis=-1, keepdims=True) + jnp.exp2(sink - m)
            ps[i] = e.astype(BF16)
            rs[i] = 1.0 / den

        softmax_head(0)
        softmax_head(2)
        o_lo = _dot(jnp.concatenate([ps[0], ps[2]], axis=0), v_lo)
        softmax_head(1)
        softmax_head(3)
        o_hi = _dot(jnp.concatenate([ps[1], ps[3]], axis=0), v_hi)
        for pp in range(group // 2):
            part = slice(pp * CHUNK, (pp + 1) * CHUNK)
            pair = g * (group // 2) + pp
            y_ref[j * CHUNK:(j + 1) * CHUNK, pair * LANES:(pair + 1) * LANES] = (
                o_lo[part] * rs[2 * pp] + o_hi[part] * rs[2 * pp + 1])

    def retention_pair(j, p):
        rows = slice(j * CHUNK, (j + 1) * CHUNK)
        sl = slice(p * LANES, (p + 1) * LANES)
        col = lambda c0: slice(c0 + p * LANES, c0 + (p + 1) * LANES)
        qp, kp, vp = p_ref[rows, col(C_RQ)], p_ref[rows, col(C_RK)], p_ref[rows, col(C_RV)]
        s2 = _dot_nt(jnp.concatenate([keep(lo, qp), keep(hi, qp)], axis=0), kp)
        p0 = (s2[:CHUNK] * dm_ref[2 * p]).astype(BF16)
        p1 = (s2[CHUNK:] * dm_ref[2 * p + 1]).astype(BF16)
        qf = (qp.astype(F32) * xif_ref[:, sl]).astype(BF16)
        qb = (qp.astype(F32) * xib_ref[:, sl]).astype(BF16)
        lhs = jnp.concatenate([p0, p1, qf, qb], axis=1)
        state_f = carry_ref[p]
        rhs = jnp.concatenate(
            [keep(lo, vp), keep(hi, vp), state_f.astype(BF16), rb_ref[j, p]], axis=0)
        ret = _dot(lhs, rhs)
        kz = (kp.astype(F32) * zf_ref[:, sl]).astype(BF16)
        carry_ref[p] = gf_ref[p] * state_f + _dot_tn(kz, vp) * bdm_ref[...]
        sq = ret * ret
        ss_lo = jnp.sum(jnp.where(lo_f32, sq, 0.0), axis=-1, keepdims=True)
        ss_hi = jnp.sum(jnp.where(lo_f32, 0.0, sq), axis=-1, keepdims=True)
        ms = jnp.where(lo_f32, ss_lo, ss_hi) * (1.0 / HEAD_DIM)
        ret_n = ret * lax.rsqrt(ms + EPS) * rnw_ref[:, sl]
        gate = p_ref[rows, col(C_RG)].astype(F32)
        y_ref[rows, ATTN_W + p * LANES:ATTN_W + (p + 1) * LANES] = (
            gate * (1.0 / (1.0 + jnp.exp(-gate))) * ret_n)

    units = [(j, g) for j in range(n_sub) for g in range(ATTN_KV_HEADS)]
    ret_items = [(j, p) for j in range(n_sub) for p in range(PAIRS)]
    assert len(units) * RET_PAIRS_PER_UNIT >= len(ret_items)
    pending = attn_scores(*units[0])
    for u, (j, g) in enumerate(units):
        following = attn_scores(*units[u + 1]) if u + 1 < len(units) else None
        for item in ret_items[u * RET_PAIRS_PER_UNIT:(u + 1) * RET_PAIRS_PER_UNIT]:
            retention_pair(*item)
        attn_finish(j, g, *pending)
        pending = following
    o_ref[...] = x_ref[...] + _dot(y_ref[...], wo_ref[...])


def _mix_call(sink, proj3, rb, x3, w_out, bias, dmat, xif, xib, rnw, zf, gf, bdm):
    batch, seq, d_model = x3.shape
    tm = MIX_TM
    n_sub = tm // CHUNK
    nt = seq // tm
    nc = seq // CHUNK
    kv_cols = C_K2 // (2 * KV2_W)
    assert C_K2 % (2 * KV2_W) == 0 and C_V2 == C_K2 + KV2_W
    kv_prev = pl.BlockSpec(
        (None, CHUNK, 2 * KV2_W), lambda b, t: (b, jnp.maximum(t * n_sub - 1, 0), kv_cols))
    kv_next = pl.BlockSpec(
        (None, CHUNK, 2 * KV2_W), lambda b, t: (b, jnp.minimum((t + 1) * n_sub, nc - 1), kv_cols))
    st = pl.BlockSpec((None, n_sub, PAIRS, LANES, LANES), lambda b, t: (b, t, 0, 0, 0))
    return pl.pallas_call(
        _mix_kernel,
        grid=(batch, nt),
        in_specs=[
            pl.BlockSpec(memory_space=pltpu.SMEM),
            pl.BlockSpec((None, tm, PROJ_W), lambda b, t: (b, t, 0)),
            kv_prev, kv_next,
            st,
            pl.BlockSpec((None, tm, d_model), lambda b, t: (b, t, 0)),
            _resident((d_model, d_model)),
            _resident((3, CHUNK, 3 * CHUNK)),
            _resident((RET_HEADS, CHUNK, CHUNK)),
            _resident((CHUNK, RET_W)),
            _resident((CHUNK, RET_W)),
            _resident((1, RET_W)),
            _resident((CHUNK, RET_W)),
            _resident((PAIRS, LANES, LANES)),
            _resident((LANES, LANES)),
        ],
        out_specs=pl.BlockSpec((None, tm, d_model), lambda b, t: (b, t, 0)),
        out_shape=jax.ShapeDtypeStruct((batch, seq, d_model), F32),
        scratch_shapes=[pltpu.VMEM((tm, d_model), F32), pltpu.VMEM((PAIRS, LANES, LANES), F32)],
        compiler_params=pltpu.CompilerParams(
            dimension_semantics=("arbitrary", "arbitrary"), vmem_limit_bytes=VMEM_LIMIT),
        name="mix",
    )(sink, proj3, proj3, proj3, rb, x3, w_out, bias, dmat, xif, xib, rnw, zf, gf, bdm)


def _ffn_kernel(h_ref, fnw_ref, wg_ref, wu_ref, wd_ref, o_ref, a_ref):
    h = h_ref[...]
    ms = jnp.mean(h * h, axis=-1, keepdims=True)
    m = (h * lax.rsqrt(ms + EPS) * fnw_ref[...]).astype(BF16)
    d_ff = wg_ref.shape[1]
    for c0 in range(0, d_ff, FFN_CK):
        c1 = min(c0 + FFN_CK, d_ff)
        g = _dot(m, wg_ref[:, c0:c1])
        u = _dot(m, wu_ref[:, c0:c1])
        a_ref[:, c0:c1] = (g * (1.0 / (1.0 + jnp.exp(-g))) * u).astype(BF16)
    o_ref[...] = h + _dot(a_ref[...], wd_ref[...])


def _ffn_call(h2, fnw, wg, wu, wd):
    tokens, d_model = h2.shape
    d_ff = wg.shape[1]
    tm = FFN_TM
    row = lambda i: (i, 0)
    return pl.pallas_call(
        _ffn_kernel,
        grid=(tokens // tm,),
        in_specs=[
            pl.BlockSpec((tm, d_model), row),
            _resident((1, d_model)),
            _resident((d_model, d_ff)),
            _resident((d_model, d_ff)),
            _resident((d_ff, d_model)),
        ],
        out_specs=pl.BlockSpec((tm, d_model), row),
        out_shape=jax.ShapeDtypeStruct((tokens, d_model), F32),
        scratch_shapes=[pltpu.VMEM((tm, d_ff), BF16)],
        compiler_params=pltpu.CompilerParams(
            dimension_semantics=("arbitrary",), vmem_limit_bytes=VMEM_LIMIT),
        name="ffn",
    )(h2, fnw, wg, wu, wd)


def _rope_tables(seq):
    inv_freq = ROPE_THETA ** (-np.arange(0, HEAD_DIM, 2, dtype=np.float64) / HEAD_DIM)
    ang = np.arange(seq, dtype=np.float64)[:, None] * inv_freq[None, :]
    cos, sin = np.cos(ang), np.sin(ang)
    zeros = np.zeros_like(sin)
    reps = LANES // HEAD_DIM
    cos_t = np.tile(np.concatenate([cos, cos], -1), (1, reps))
    sina_t = np.tile(np.concatenate([-sin, zeros], -1), (1, reps))
    sinb_t = np.tile(np.concatenate([zeros, sin], -1), (1, reps))
    return tuple(jnp.asarray(t.astype(np.float32)) for t in (cos_t, sina_t, sinb_t))


def _attn_bias():
    i = np.arange(CHUNK)[:, None]
    j = np.arange(3 * CHUNK)[None, :]
    band = np.abs(i + CHUNK - j) <= CHUNK
    first = band & (j >= CHUNK)
    last = band & (j < 2 * CHUNK)
    tab = np.stack([first, band, last])
    return jnp.asarray(np.where(tab, 0.0, NEG_INF).astype(np.float32))


def _retention_tables(log_f, log_b):
    idx = jnp.arange(CHUNK, dtype=F32)
    diff = idx[:, None] - idx[None, :]
    lf, lb = log_f[:, None, None], log_b[:, None, None]
    dmat = jnp.where(diff[None] >= 0,
                     jnp.exp(lf * jnp.maximum(diff, 0.0)[None]),
                     jnp.exp(lb * jnp.maximum(-diff, 0.0)[None]))
    per_lane = lambda t: jnp.repeat(t.T, HEAD_DIM, axis=1)
    xif = per_lane(jnp.exp(log_f[:, None] * (idx + 1.0)[None]))
    xib = per_lane(jnp.exp(log_b[:, None] * (CHUNK - idx)[None]))
    zf = per_lane(jnp.exp(log_f[:, None] * (CHUNK - 1.0 - idx)[None]))
    zb = per_lane(jnp.exp(log_b[:, None] * idx[None]))
    per_row = lambda g: jnp.broadcast_to(
        jnp.repeat(g.reshape(PAIRS, 2), HEAD_DIM, axis=1)[:, :, None], (PAIRS, LANES, LANES))
    gf = per_row(jnp.exp(log_f * CHUNK))
    gb = per_row(jnp.exp(log_b * CHUNK))
    return dmat, xif, xib, zf, zb, gf, gb


def _block_diag_ones(width, dtype):
    r = np.arange(width) // HEAD_DIM
    return jnp.asarray((r[:, None] == r[None, :]).astype(np.float32), dtype=dtype)


def kernel(x, attn_norm_w, w_in, q_norm_w, k_norm_w, attn_sink, ret_log_decay_fwd,
           ret_log_decay_bwd, ret_norm_w, w_out, ffn_norm_w, w_gate, w_up, w_down):
    batch, seq, d_model = x.shape
    depth = w_in.shape[0]
    assert seq % PROJ_TM == 0 and seq % MIX_TM == 0 and (batch * seq) % FFN_TM == 0
    assert MIX_TM // CHUNK >= 2

    cos_t, sina_t, sinb_t = _rope_tables(seq)
    bias = _attn_bias()
    bd = _block_diag_ones(RET_W, BF16)
    bdm = _block_diag_ones(LANES, F32)

    h = x
    for l in range(depth):
        qnw = jnp.tile(q_norm_w[l], ATTN_HEADS)[None, :]
        knw = jnp.tile(k_norm_w[l], ATTN_KV_HEADS)[None, :]
        log_f = -jnp.abs(ret_log_decay_fwd[l].astype(F32))
        log_b = -jnp.abs(ret_log_decay_bwd[l].astype(F32))
        dmat, xif, xib, zf, zb, gf, gb = _retention_tables(log_f, log_b)

        proj, wg_bf, wu_bf, wd_bf, rb = _proj_call(
            h.reshape(batch * seq, d_model), attn_norm_w[l][None, :], w_in[l], qnw, knw,
            cos_t, sina_t, sinb_t, bd, seq, (w_gate[l], w_up[l], w_down[l]), zb, gb, bdm)
        proj3 = proj.reshape(batch, seq, PROJ_W)
        rb = rb.reshape(batch, seq // CHUNK, PAIRS, LANES, LANES)
        h = _mix_call(attn_sink[l].astype(F32), proj3, rb, h, w_out[l],
                      bias, dmat, xif, xib, ret_norm_w[l][None, :], zf, gf, bdm)
        h = _ffn_call(h.reshape(batch * seq, d_model), ffn_norm_w[l][None, :],
                      wg_bf, wu_bf, wd_bf).reshape(batch, seq, d_model)
    return h
```

```python
import functools

import jax
import jax.numpy as jnp
import numpy as np
from jax import lax
from jax.experimental import pallas as pl
from jax.experimental.pallas import tpu as pltpu

HEAD_DIM = 64
ATTN_HEADS = 8
ATTN_KV_HEADS = 2
RET_HEADS = 8
CHUNK = 128
ROPE_THETA = 10000.0
EPS = 1e-6
NEG_INF = -1e30
LOG2_E = 1.4426950408889634

LANES = 128
BF16_SUBLANES = 16
PAIRS = RET_HEADS // 2
ATTN_W = ATTN_HEADS * HEAD_DIM
RET_W = RET_HEADS * HEAD_DIM
KV2_W = 2 * ATTN_KV_HEADS * HEAD_DIM

C_AQ = 0
C_K2 = C_AQ + ATTN_W
C_V2 = C_K2 + KV2_W
C_RK = C_V2 + KV2_W
C_RV = C_RK + RET_W
C_RQ = C_RV + RET_W
C_RG = C_RQ + RET_W
PROJ_W = C_RG + RET_W

PROJ_TM = 1024
MIX_TM = 1024
RET_PAIRS_PER_UNIT = 2
FFN_TM = 1024
FFN_CK = 512
VMEM_LIMIT = 56 * 1024 * 1024

BF16 = jnp.bfloat16
F32 = jnp.float32


def _dot(a, b):
    return jnp.dot(a, b, preferred_element_type=F32)


def _dot_nt(a, b):
    return lax.dot_general(a, b, (((1,), (1,)), ((), ())), preferred_element_type=F32)


def _dot_tn(a, b):
    return lax.dot_general(a, b, (((0,), (0,)), ((), ())), preferred_element_type=F32)


def _resident(shape):
    zeros = (0,) * len(shape)
    return pl.BlockSpec(shape, lambda *_: zeros, pipeline_mode=pl.Buffered(1))


def _proj_kernel(tiles_per_seq, x_ref, anw_ref, w_ref, qnw_ref, knw_ref, cos_ref, sina_ref,
                 sinb_ref, bd_ref, wg_ref, wu_ref, wd_ref, zb_ref, gb_ref, bdm_ref,
                 o_ref, wg_bf_ref, wu_bf_ref, wd_bf_ref, rb_ref, carry_ref):
    @pl.when(pl.program_id(0) % tiles_per_seq == 0)
    def _():
        carry_ref[...] = jnp.zeros_like(carry_ref)

    x = x_ref[...]
    ms = jnp.mean(x * x, axis=-1, keepdims=True)
    n = x * lax.rsqrt(ms + EPS) * anw_ref[...]
    cos = cos_ref[...]
    sina = sina_ref[...]
    sinb = sinb_ref[...]

    def proj(c0, width):
        return _dot(n, w_ref[:, c0:c0 + width])

    def head_norm(y, w):
        width = y.shape[-1]
        ss = _dot((y * y).astype(BF16), bd_ref[:width, :width])
        return y * lax.rsqrt(ss * (1.0 / HEAD_DIM) + EPS) * w

    def rope(ys):
        return ys * cos + pltpu.roll(ys, LANES - 32, 1) * sina + pltpu.roll(ys, 32, 1) * sinb

    def rope_store(y, c0, scale):
        for s in range(y.shape[-1] // LANES):
            r = rope(y[:, s * LANES:(s + 1) * LANES])
            if scale != 1.0:
                r = r * scale
            o_ref[:, c0 + s * LANES:c0 + (s + 1) * LANES] = r.astype(BF16)

    def plain_store(y, c0):
        o_ref[:, c0:c0 + y.shape[-1]] = y.astype(BF16)

    def store_both_orders(y, c0):
        o_ref[:, c0:c0 + LANES] = y.astype(BF16)
        o_ref[:, c0 + LANES:c0 + 2 * LANES] = pltpu.roll(y, HEAD_DIM, 1).astype(BF16)

    def kv_finish(y):
        store_both_orders(rope(head_norm(y[:, :LANES], knw_ref[...])), C_K2)
        store_both_orders(y[:, LANES:], C_V2)

    scale = HEAD_DIM ** -0.5
    kvw = ATTN_KV_HEADS * HEAD_DIM
    in_kv = ATTN_W
    in_rq = in_kv + 2 * kvw
    def backward_states():
        bdm = bdm_ref[...]
        for p in range(PAIRS):
            sl = slice(p * LANES, (p + 1) * LANES)
            carry = carry_ref[p]
            for c in reversed(range(o_ref.shape[0] // CHUNK)):
                rows = slice(c * CHUNK, (c + 1) * CHUNK)
                k = o_ref[rows, C_RK + p * LANES:C_RK + (p + 1) * LANES].astype(F32) * zb_ref[:, sl]
                kv = _dot_tn(k.astype(BF16), o_ref[rows, C_RV + p * LANES:C_RV + (p + 1) * LANES])
                rb_ref[c, p] = carry.astype(BF16)
                carry = gb_ref[p] * carry + kv * bdm
            carry_ref[p] = carry

    def rq_finish(y):
        rope_store(y, C_RQ, 1.0)
        backward_states()

    stages = [
        (0, ATTN_W, lambda y: rope_store(head_norm(y, qnw_ref[...]), C_AQ, scale * LOG2_E)),
        (in_rq + RET_W, RET_W, lambda y: rope_store(y, C_RK, scale)),
        (in_rq + 2 * RET_W, RET_W, lambda y: plain_store(y, C_RV)),
        (in_kv, 2 * kvw, kv_finish),
        (in_rq, RET_W, rq_finish),
        (in_rq + 3 * RET_W, RET_W, lambda y: plain_store(y, C_RG)),
    ]
    pending = proj(stages[0][0], stages[0][1])
    wg_bf_ref[...] = wg_ref[...].astype(BF16)
    wu_bf_ref[...] = wu_ref[...].astype(BF16)
    wd_bf_ref[...] = wd_ref[...].astype(BF16)
    for i, (_, _, finish) in enumerate(stages):
        following = proj(stages[i + 1][0], stages[i + 1][1]) if i + 1 < len(stages) else None
        finish(pending)
        pending = following


def _slab_spec(weight, n_steps):
    rows, cols = weight.shape
    n_slabs = n_steps
    while rows % n_slabs or (rows // n_slabs) % BF16_SUBLANES or n_steps % n_slabs:
        n_slabs -= 1
    hold = n_steps // n_slabs
    return pl.BlockSpec((rows // n_slabs, cols), lambda i: (i // hold, 0))


def _proj_call(x2, anw, w_in, qnw, knw, cos, sina, sinb, bd, seq, ffn_weights, zb, gb, bdm):
    tokens, d_model = x2.shape
    tm = PROJ_TM
    n_steps = tokens // tm
    tiles_per_seq = seq // tm
    n_sub = tm // CHUNK
    row = lambda i: (n_steps - 1 - i, 0)
    pos = lambda i: ((n_steps - 1 - i) % tiles_per_seq, 0)
    slabs = [_slab_spec(w, n_steps) for w in ffn_weights]
    state_blk = (n_sub, PAIRS, LANES, LANES)
    return pl.pallas_call(
        functools.partial(_proj_kernel, tiles_per_seq),
        grid=(n_steps,),
        in_specs=[
            pl.BlockSpec((tm, d_model), row),
            _resident((1, d_model)),
            _resident(w_in.shape),
            _resident((1, ATTN_W)),
            _resident((1, LANES)),
            pl.BlockSpec((tm, LANES), pos),
            pl.BlockSpec((tm, LANES), pos),
            pl.BlockSpec((tm, LANES), pos),
            _resident((ATTN_W, ATTN_W)),
            *slabs,
            _resident((CHUNK, RET_W)),
            _resident((PAIRS, LANES, LANES)),
            _resident((LANES, LANES)),
        ],
        out_specs=[pl.BlockSpec((tm, PROJ_W), row), *slabs,
                   pl.BlockSpec(state_blk, lambda i: (n_steps - 1 - i, 0, 0, 0))],
        out_shape=[jax.ShapeDtypeStruct((tokens, PROJ_W), BF16),
                   *[jax.ShapeDtypeStruct(w.shape, BF16) for w in ffn_weights],
                   jax.ShapeDtypeStruct((tokens // CHUNK, PAIRS, LANES, LANES), BF16)],
        scratch_shapes=[pltpu.VMEM((PAIRS, LANES, LANES), F32)],
        compiler_params=pltpu.CompilerParams(
            dimension_semantics=("arbitrary",), vmem_limit_bytes=VMEM_LIMIT),
        name="proj",
    )(x2, anw, w_in, qnw, knw, cos, sina, sinb, bd, *ffn_weights, zb, gb, bdm)


def _mix_kernel(sink_ref, p_ref, prev_ref, next_ref, rb_ref, x_ref, wo_ref, bias_ref, dm_ref,
                xif_ref, xib_ref, rnw_ref, zf_ref, gf_ref, bdm_ref, o_ref, y_ref, carry_ref):
    lane = lax.broadcasted_iota(jnp.int32, (CHUNK, LANES), 1).astype(F32).astype(BF16)
    lo = lane < HEAD_DIM
    hi = jnp.logical_not(lo)
    lo_f32 = lax.broadcasted_iota(jnp.int32, (CHUNK, LANES), 1) < HEAD_DIM
    zero = jnp.zeros((), BF16)

    def keep(mask, a):
        reps = a.shape[0] // CHUNK
        m = mask if reps == 1 else jnp.concatenate([mask] * reps, axis=0)
        return jnp.where(m, a, zero)

    tile = pl.program_id(1)

    @pl.when(tile == 0)
    def _():
        carry_ref[...] = jnp.zeros_like(carry_ref)

    n_sub = p_ref.shape[0] // CHUNK
    group = ATTN_HEADS // ATTN_KV_HEADS
    k2_cols = slice(C_K2, C_K2 + KV2_W)
    v2_cols = slice(C_V2, C_V2 + KV2_W)

    def kv_window(j):
        if j == 0:
            inner = slice(0, 2 * CHUNK)
            k3 = jnp.concatenate([prev_ref[:, :KV2_W], p_ref[inner, k2_cols]], axis=0)
            v3 = jnp.concatenate([prev_ref[:, KV2_W:], p_ref[inner, v2_cols]], axis=0)
            bias = bias_ref[jnp.where(tile == 0, 0, 1)]
        elif j == n_sub - 1:
            inner = slice((j - 1) * CHUNK, (j + 1) * CHUNK)
            k3 = jnp.concatenate([p_ref[inner, k2_cols], next_ref[:, :KV2_W]], axis=0)
            v3 = jnp.concatenate([p_ref[inner, v2_cols], next_ref[:, KV2_W:]], axis=0)
            bias = bias_ref[jnp.where(tile == pl.num_programs(1) - 1, 2, 1)]
        else:
            inner = slice((j - 1) * CHUNK, (j + 2) * CHUNK)
            k3 = p_ref[inner, k2_cols]
            v3 = p_ref[inner, v2_cols]
            bias = bias_ref[1]
        return k3, v3, bias

    def attn_scores(j, g):
        k3, v3, bias = kv_window(j)
        g_low = slice(g * LANES, (g + 1) * LANES)
        g_high = slice((1 - g) * LANES, (2 - g) * LANES)
        kg = jnp.where(jnp.concatenate([lo] * 3, axis=0), k3[:, g_low], k3[:, g_high])
        v_lo = keep(lo, v3[:, g_low])
        v_hi = keep(hi, v3[:, g_high])
        qs = []
        for i in range(group):
            h = g * group + i
            qp = p_ref[j * CHUNK:(j + 1) * CHUNK,
                       C_AQ + (h // 2) * LANES:C_AQ + (h // 2 + 1) * LANES]
            qs.append(keep(lo if h % 2 == 0 else hi, qp))
        s_all = _dot_nt(jnp.concatenate(qs, axis=0), kg)
        return s_all, v_lo, v_hi, bias

    def attn_finish(j, g, s_all, v_lo, v_hi, bias):
        ps, rs = {}, {}

        def softmax_head(i):
            h = g * group + i
            s = s_all[i * CHUNK:(i + 1) * CHUNK] + bias
            sink = sink_ref[h] * LOG2_E
            m = jnp.maximum(jnp.max(s, axis=-1, keepdims=True), sink)
            e = jnp.exp2(s - m)
            den = jnp.sum(e, axis=-1, keepdims=True) + jnp.exp2(sink - m)
            ps[i] = e.astype(BF16)
            rs[i] = 1.0 / den

        softmax_head(0)
        softmax_head(2)
        o_lo = _dot(jnp.concatenate([ps[0], ps[2]], axis=0), v_lo)
        softmax_head(1)
        softmax_head(3)
        o_hi = _dot(jnp.concatenate([ps[1], ps[3]], axis=0), v_hi)
        for pp in range(group // 2):
            part = slice(pp * CHUNK, (pp + 1) * CHUNK)
            pair = g * (group // 2) + pp
            y_ref[j * CHUNK:(j + 1) * CHUNK, pair * LANES:(pair + 1) * LANES] = (
                o_lo[part] * rs[2 * pp] + o_hi[part] * rs[2 * pp + 1])

    def retention_pair(j, p):
        rows = slice(j * CHUNK, (j + 1) * CHUNK)
        sl = slice(p * LANES, (p + 1) * LANES)
        col = lambda c0: slice(c0 + p * LANES, c0 + (p + 1) * LANES)
        qp, kp, vp = p_ref[rows, col(C_RQ)], p_ref[rows, col(C_RK)], p_ref[rows, col(C_RV)]
        s2 = _dot_nt(jnp.concatenate([keep(lo, qp), keep(hi, qp)], axis=0), kp)
        p0 = (s2[:CHUNK] * dm_ref[2 * p]).astype(BF16)
        p1 = (s2[CHUNK:] * dm_ref[2 * p + 1]).astype(BF16)
        qf = (qp.astype(F32) * xif_ref[:, sl]).astype(BF16)
        qb = (qp.astype(F32) * xib_ref[:, sl]).astype(BF16)
        lhs = jnp.concatenate([p0, p1, qf, qb], axis=1)
        state_f = carry_ref[p]
        rhs = jnp.concatenate(
            [keep(lo, vp), keep(hi, vp), state_f.astype(BF16), rb_ref[j, p]], axis=0)
        ret = _dot(lhs, rhs)
        kz = (kp.astype(F32) * zf_ref[:, sl]).astype(BF16)
        carry_ref[p] = gf_ref[p] * state_f + _dot_tn(kz, vp) * bdm_ref[...]
        sq = ret * ret
        ss_lo = jnp.sum(jnp.where(lo_f32, sq, 0.0), axis=-1, keepdims=True)
        ss_hi = jnp.sum(jnp.where(lo_f32, 0.0, sq), axis=-1, keepdims=True)
        ms = jnp.where(lo_f32, ss_lo, ss_hi) * (1.0 / HEAD_DIM)
        ret_n = ret * lax.rsqrt(ms + EPS) * rnw_ref[:, sl]
        gate = p_ref[rows, col(C_RG)].astype(F32)
        y_ref[rows, ATTN_W + p * LANES:ATTN_W + (p + 1) * LANES] = (
            gate * (1.0 / (1.0 + jnp.exp(-gate))) * ret_n)

    units = [(j, g) for j in range(n_sub) for g in range(ATTN_KV_HEADS)]
    ret_items = [(j, p) for j in range(n_sub) for p in range(PAIRS)]
    assert len(units) * RET_PAIRS_PER_UNIT >= len(ret_items)
    pending = attn_scores(*units[0])
    for u, (j, g) in enumerate(units):
        following = attn_scores(*units[u + 1]) if u + 1 < len(units) else None
        for item in ret_items[u * RET_PAIRS_PER_UNIT:(u + 1) * RET_PAIRS_PER_UNIT]:
            retention_pair(*item)
        attn_finish(j, g, *pending)
        pending = following
    o_ref[...] = x_ref[...] + _dot(y_ref[...], wo_ref[...])


def _mix_call(sink, proj3, rb, x3, w_out, bias, dmat, xif, xib, rnw, zf, gf, bdm):
    batch, seq, d_model = x3.shape
    tm = MIX_TM
    n_sub = tm // CHUNK
    nt = seq // tm
    nc = seq // CHUNK
    kv_cols = C_K2 // (2 * KV2_W)
    assert C_K2 % (2 * KV2_W) == 0 and C_V2 == C_K2 + KV2_W
    kv_prev = pl.BlockSpec(
        (None, CHUNK, 2 * KV2_W), lambda b, t: (b, jnp.maximum(t * n_sub - 1, 0), kv_cols))
    kv_next = pl.BlockSpec(
        (None, CHUNK, 2 * KV2_W), lambda b, t: (b, jnp.minimum((t + 1) * n_sub, nc - 1), kv_cols))
    st = pl.BlockSpec((None, n_sub, PAIRS, LANES, LANES), lambda b, t: (b, t, 0, 0, 0))
    return pl.pallas_call(
        _mix_kernel,
        grid=(batch, nt),
        in_specs=[
            pl.BlockSpec(memory_space=pltpu.SMEM),
            pl.BlockSpec((None, tm, PROJ_W), lambda b, t: (b, t, 0)),
            kv_prev, kv_next,
            st,
            pl.BlockSpec((None, tm, d_model), lambda b, t: (b, t, 0)),
            _resident((d_model, d_model)),
            _resident((3, CHUNK, 3 * CHUNK)),
            _resident((RET_HEADS, CHUNK, CHUNK)),
            _resident((CHUNK, RET_W)),
            _resident((CHUNK, RET_W)),
            _resident((1, RET_W)),
            _resident((CHUNK, RET_W)),
            _resident((PAIRS, LANES, LANES)),
            _resident((LANES, LANES)),
        ],
        out_specs=pl.BlockSpec((None, tm, d_model), lambda b, t: (b, t, 0)),
        out_shape=jax.ShapeDtypeStruct((batch, seq, d_model), F32),
        scratch_shapes=[pltpu.VMEM((tm, d_model), F32), pltpu.VMEM((PAIRS, LANES, LANES), F32)],
        compiler_params=pltpu.CompilerParams(
            dimension_semantics=("arbitrary", "arbitrary"), vmem_limit_bytes=VMEM_LIMIT),
        name="mix",
    )(sink, proj3, proj3, proj3, rb, x3, w_out, bias, dmat, xif, xib, rnw, zf, gf, bdm)


def _ffn_kernel(h_ref, fnw_ref, wg_ref, wu_ref, wd_ref, o_ref, a_ref):
    h = h_ref[...]
    ms = jnp.mean(h * h, axis=-1, keepdims=True)
    m = (h * lax.rsqrt(ms + EPS) * fnw_ref[...]).astype(BF16)
    d_ff = wg_ref.shape[1]
    for c0 in range(0, d_ff, FFN_CK):
        c1 = min(c0 + FFN_CK, d_ff)
        g = _dot(m, wg_ref[:, c0:c1])
        u = _dot(m, wu_ref[:, c0:c1])
        a_ref[:, c0:c1] = (g * (1.0 / (1.0 + jnp.exp(-g))) * u).astype(BF16)
    o_ref[...] = h + _dot(a_ref[...], wd_ref[...])


def _ffn_call(h2, fnw, wg, wu, wd):
    tokens, d_model = h2.shape
    d_ff = wg.shape[1]
    tm = FFN_TM
    row = lambda i: (i, 0)
    return pl.pallas_call(
        _ffn_kernel,
        grid=(tokens // tm,),
        in_specs=[
            pl.BlockSpec((tm, d_model), row),
            _resident((1, d_model)),
            _resident((d_model, d_ff)),
            _resident((d_model, d_ff)),
            _resident((d_ff, d_model)),
        ],
        out_specs=pl.BlockSpec((tm, d_model), row),
        out_shape=jax.ShapeDtypeStruct((tokens, d_model), F32),
        scratch_shapes=[pltpu.VMEM((tm, d_ff), BF16)],
        compiler_params=pltpu.CompilerParams(
            dimension_semantics=("arbitrary",), vmem_limit_bytes=VMEM_LIMIT),
        name="ffn",
    )(h2, fnw, wg, wu, wd)


def _rope_tables(seq):
    inv_freq = ROPE_THETA ** (-np.arange(0, HEAD_DIM, 2, dtype=np.float64) / HEAD_DIM)
    ang = np.arange(seq, dtype=np.float64)[:, None] * inv_freq[None, :]
    cos, sin = np.cos(ang), np.sin(ang)
    zeros = np.zeros_like(sin)
    reps = LANES // HEAD_DIM
    cos_t = np.tile(np.concatenate([cos, cos], -1), (1, reps))
    sina_t = np.tile(np.concatenate([-sin, zeros], -1), (1, reps))
    sinb_t = np.tile(np.concatenate([zeros, sin], -1), (1, reps))
    return tuple(jnp.asarray(t.astype(np.float32)) for t in (cos_t, sina_t, sinb_t))


def _attn_bias():
    i = np.arange(CHUNK)[:, None]
    j = np.arange(3 * CHUNK)[None, :]
    band = np.abs(i + CHUNK - j) <= CHUNK
    first = band & (j >= CHUNK)
    last = band & (j < 2 * CHUNK)
    tab = np.stack([first, band, last])
    return jnp.asarray(np.where(tab, 0.0, NEG_INF).astype(np.float32))


def _retention_tables(log_f, log_b):
    idx = jnp.arange(CHUNK, dtype=F32)
    diff = idx[:, None] - idx[None, :]
    lf, lb = log_f[:, None, None], log_b[:, None, None]
    dmat = jnp.where(diff[None] >= 0,
                     jnp.exp(lf * jnp.maximum(diff, 0.0)[None]),
                     jnp.exp(lb * jnp.maximum(-diff, 0.0)[None]))
    pos = np.arange(CHUNK, dtype=np.float32)[:, None]
    lane_f = jnp.repeat(log_f, HEAD_DIM)[None, :]
    lane_b = jnp.repeat(log_b, HEAD_DIM)[None, :]
    xif = jnp.exp(lane_f * (pos + 1.0))
    xib = jnp.exp(lane_b * (CHUNK - pos))
    zf = jnp.exp(lane_f * (CHUNK - 1.0 - pos))
    zb = jnp.exp(lane_b * pos)
    per_row = lambda lg: jnp.broadcast_to(
        jnp.exp(jnp.repeat(lg.reshape(PAIRS, 2), HEAD_DIM, axis=1) * CHUNK)[:, :, None],
        (PAIRS, LANES, LANES))
    return dmat, xif, xib, zf, zb, per_row(log_f), per_row(log_b)


def _block_diag_ones(width, dtype):
    r = np.arange(width) // HEAD_DIM
    return jnp.asarray((r[:, None] == r[None, :]).astype(np.float32), dtype=dtype)


def kernel(x, attn_norm_w, w_in, q_norm_w, k_norm_w, attn_sink, ret_log_decay_fwd,
           ret_log_decay_bwd, ret_norm_w, w_out, ffn_norm_w, w_gate, w_up, w_down):
    batch, seq, d_model = x.shape
    depth = w_in.shape[0]
    assert seq % PROJ_TM == 0 and seq % MIX_TM == 0 and (batch * seq) % FFN_TM == 0
    assert MIX_TM // CHUNK >= 2

    cos_t, sina_t, sinb_t = _rope_tables(seq)
    bias = _attn_bias()
    bd = _block_diag_ones(RET_W, BF16)
    bdm = _block_diag_ones(LANES, F32)

    h = x
    for l in range(depth):
        qnw = jnp.tile(q_norm_w[l], ATTN_HEADS)[None, :]
        knw = jnp.tile(k_norm_w[l], ATTN_KV_HEADS)[None, :]
        log_f = -jnp.abs(ret_log_decay_fwd[l].astype(F32))
        log_b = -jnp.abs(ret_log_decay_bwd[l].astype(F32))
        dmat, xif, xib, zf, zb, gf, gb = _retention_tables(log_f, log_b)

        proj, wg_bf, wu_bf, wd_bf, rb = _proj_call(
            h.reshape(batch * seq, d_model), attn_norm_w[l][None, :], w_in[l], qnw, knw,
            cos_t, sina_t, sinb_t, bd, seq, (w_gate[l], w_up[l], w_down[l]), zb, gb, bdm)
        proj3 = proj.reshape(batch, seq, PROJ_W)
        rb = rb.reshape(batch, seq // CHUNK, PAIRS, LANES, LANES)
        h = _mix_call(attn_sink[l].astype(F32), proj3, rb, h, w_out[l],
                      bias, dmat, xif, xib, ret_norm_w[l][None, :], zf, gf, bdm)
        h = _ffn_call(h.reshape(batch * seq, d_model), ffn_norm_w[l][None, :],
                      wg_bf, wu_bf, wd_bf).reshape(batch, seq, d_model)
    return h
```

```python
import functools

import jax
import jax.numpy as jnp
import numpy as np
from jax import lax
from jax.experimental import pallas as pl
from jax.experimental.pallas import tpu as pltpu

HEAD_DIM = 64
ATTN_HEADS = 8
ATTN_KV_HEADS = 2
RET_HEADS = 8
CHUNK = 128
ROPE_THETA = 10000.0
EPS = 1e-6
NEG_INF = -1e30
LOG2_E = 1.4426950408889634

LANES = 128
BF16_SUBLANES = 16
PAIRS = RET_HEADS // 2
ATTN_W = ATTN_HEADS * HEAD_DIM
RET_W = RET_HEADS * HEAD_DIM
KV2_W = 2 * ATTN_KV_HEADS * HEAD_DIM

C_AQ = 0
C_K2 = C_AQ + ATTN_W
C_V2 = C_K2 + KV2_W
C_RK = C_V2 + KV2_W
C_RV = C_RK + RET_W
C_RQ = C_RV + RET_W
C_RG = C_RQ + RET_W
PROJ_W = C_RG + RET_W

PROJ_TM = 1024
MIX_TM = 1024
RET_PAIRS_PER_UNIT = 2
FFN_TM = 1024
FFN_CK = 512
VMEM_LIMIT = 56 * 1024 * 1024

BF16 = jnp.bfloat16
F32 = jnp.float32


def _dot(a, b):
    return jnp.dot(a, b, preferred_element_type=F32)


def _dot_nt(a, b):
    return lax.dot_general(a, b, (((1,), (1,)), ((), ())), preferred_element_type=F32)


def _dot_tn(a, b):
    return lax.dot_general(a, b, (((0,), (0,)), ((), ())), preferred_element_type=F32)


def _resident(shape):
    zeros = (0,) * len(shape)
    return pl.BlockSpec(shape, lambda *_: zeros, pipeline_mode=pl.Buffered(1))


def _proj_kernel(tiles_per_seq, x_ref, anw_ref, w_ref, qnw_ref, knw_ref, cos_ref, sina_ref,
                 sinb_ref, bd_ref, wg_ref, wu_ref, wd_ref, wo_ref, zb_ref, gb_ref, bdm_ref,
                 o_ref, wg_bf_ref, wu_bf_ref, wd_bf_ref, wo_bf_ref, rb_ref, carry_ref):
    @pl.when(pl.program_id(0) % tiles_per_seq == 0)
    def _():
        carry_ref[...] = jnp.zeros_like(carry_ref)

    x = x_ref[...]
    ms = jnp.mean(x * x, axis=-1, keepdims=True)
    n = x * lax.rsqrt(ms + EPS) * anw_ref[...]
    cos = cos_ref[...]
    sina = sina_ref[...]
    sinb = sinb_ref[...]

    def proj(c0, width):
        return _dot(n, w_ref[:, c0:c0 + width])

    def head_norm(y, w):
        width = y.shape[-1]
        ss = _dot((y * y).astype(BF16), bd_ref[:width, :width])
        return y * lax.rsqrt(ss * (1.0 / HEAD_DIM) + EPS) * w

    def rope(ys):
        return ys * cos + pltpu.roll(ys, LANES - 32, 1) * sina + pltpu.roll(ys, 32, 1) * sinb

    def rope_store(y, c0, scale):
        for s in range(y.shape[-1] // LANES):
            r = rope(y[:, s * LANES:(s + 1) * LANES])
            if scale != 1.0:
                r = r * scale
            o_ref[:, c0 + s * LANES:c0 + (s + 1) * LANES] = r.astype(BF16)

    def plain_store(y, c0):
        o_ref[:, c0:c0 + y.shape[-1]] = y.astype(BF16)

    def store_both_orders(y, c0):
        o_ref[:, c0:c0 + LANES] = y.astype(BF16)
        o_ref[:, c0 + LANES:c0 + 2 * LANES] = pltpu.roll(y, HEAD_DIM, 1).astype(BF16)

    def kv_finish(y):
        store_both_orders(rope(head_norm(y[:, :LANES], knw_ref[...])), C_K2)
        store_both_orders(y[:, LANES:], C_V2)

    scale = HEAD_DIM ** -0.5
    kvw = ATTN_KV_HEADS * HEAD_DIM
    in_kv = ATTN_W
    in_rq = in_kv + 2 * kvw
    def backward_states():
        bdm = bdm_ref[...]
        for p in range(PAIRS):
            sl = slice(p * LANES, (p + 1) * LANES)
            carry = carry_ref[p]
            for c in reversed(range(o_ref.shape[0] // CHUNK)):
                rows = slice(c * CHUNK, (c + 1) * CHUNK)
                k = o_ref[rows, C_RK + p * LANES:C_RK + (p + 1) * LANES].astype(F32) * zb_ref[:, sl]
                kv = _dot_tn(k.astype(BF16), o_ref[rows, C_RV + p * LANES:C_RV + (p + 1) * LANES])
                rb_ref[c, p] = carry.astype(BF16)
                carry = gb_ref[p] * carry + kv * bdm
            carry_ref[p] = carry

    def rq_finish(y):
        rope_store(y, C_RQ, 1.0)
        backward_states()

    stages = [
        (0, ATTN_W, lambda y: rope_store(head_norm(y, qnw_ref[...]), C_AQ, scale * LOG2_E)),
        (in_rq + RET_W, RET_W, lambda y: rope_store(y, C_RK, scale)),
        (in_rq + 2 * RET_W, RET_W, lambda y: plain_store(y, C_RV)),
        (in_kv, 2 * kvw, kv_finish),
        (in_rq, RET_W, rq_finish),
        (in_rq + 3 * RET_W, RET_W, lambda y: plain_store(y, C_RG)),
    ]
    pending = proj(stages[0][0], stages[0][1])
    wg_bf_ref[...] = wg_ref[...].astype(BF16)
    wu_bf_ref[...] = wu_ref[...].astype(BF16)
    wd_bf_ref[...] = wd_ref[...].astype(BF16)
    wo_bf_ref[...] = wo_ref[...].astype(BF16)
    for i, (_, _, finish) in enumerate(stages):
        following = proj(stages[i + 1][0], stages[i + 1][1]) if i + 1 < len(stages) else None
        finish(pending)
        pending = following


def _slab_spec(weight, n_steps):
    rows, cols = weight.shape
    n_slabs = n_steps
    while rows % n_slabs or (rows // n_slabs) % BF16_SUBLANES or n_steps % n_slabs:
        n_slabs -= 1
    hold = n_steps // n_slabs
    return pl.BlockSpec((rows // n_slabs, cols), lambda i: (i // hold, 0))


def _proj_call(x2, anw, w_in, qnw, knw, cos, sina, sinb, bd, seq, ffn_weights, zb, gb, bdm):
    tokens, d_model = x2.shape
    tm = PROJ_TM
    n_steps = tokens // tm
    tiles_per_seq = seq // tm
    n_sub = tm // CHUNK
    row = lambda i: (n_steps - 1 - i, 0)
    pos = lambda i: ((n_steps - 1 - i) % tiles_per_seq, 0)
    slabs = [_slab_spec(w, n_steps) for w in ffn_weights]
    state_blk = (n_sub, PAIRS, LANES, LANES)
    return pl.pallas_call(
        functools.partial(_proj_kernel, tiles_per_seq),
        grid=(n_steps,),
        in_specs=[
            pl.BlockSpec((tm, d_model), row),
            _resident((1, d_model)),
            _resident(w_in.shape),
            _resident((1, ATTN_W)),
            _resident((1, LANES)),
            pl.BlockSpec((tm, LANES), pos),
            pl.BlockSpec((tm, LANES), pos),
            pl.BlockSpec((tm, LANES), pos),
            _resident((ATTN_W, ATTN_W)),
            *slabs,
            _resident((CHUNK, RET_W)),
            _resident((PAIRS, LANES, LANES)),
            _resident((LANES, LANES)),
        ],
        out_specs=[pl.BlockSpec((tm, PROJ_W), row), *slabs,
                   pl.BlockSpec(state_blk, lambda i: (n_steps - 1 - i, 0, 0, 0))],
        out_shape=[jax.ShapeDtypeStruct((tokens, PROJ_W), BF16),
                   *[jax.ShapeDtypeStruct(w.shape, BF16) for w in ffn_weights],
                   jax.ShapeDtypeStruct((tokens // CHUNK, PAIRS, LANES, LANES), BF16)],
        scratch_shapes=[pltpu.VMEM((PAIRS, LANES, LANES), F32)],
        compiler_params=pltpu.CompilerParams(
            dimension_semantics=("arbitrary",), vmem_limit_bytes=VMEM_LIMIT),
        name="proj",
    )(x2, anw, w_in, qnw, knw, cos, sina, sinb, bd, *ffn_weights, zb, gb, bdm)


def _mix_kernel(sink_ref, p_ref, prev_ref, next_ref, rb_ref, x_ref, wo_ref, bias_ref, dm_ref,
                xif_ref, xib_ref, rnw_ref, zf_ref, gf_ref, bdm_ref, o_ref, y_ref, carry_ref):
    lane = lax.broadcasted_iota(jnp.int32, (CHUNK, LANES), 1).astype(F32).astype(BF16)
    lo = lane < HEAD_DIM
    hi = jnp.logical_not(lo)
    lo_f32 = lax.broadcasted_iota(jnp.int32, (CHUNK, LANES), 1) < HEAD_DIM
    zero = jnp.zeros((), BF16)

    def keep(mask, a):
        reps = a.shape[0] // CHUNK
        m = mask if reps == 1 else jnp.concatenate([mask] * reps, axis=0)
        return jnp.where(m, a, zero)

    tile = pl.program_id(1)

    @pl.when(tile == 0)
    def _():
        carry_ref[...] = jnp.zeros_like(carry_ref)

    n_sub = p_ref.shape[0] // CHUNK
    group = ATTN_HEADS // ATTN_KV_HEADS
    k2_cols = slice(C_K2, C_K2 + KV2_W)
    v2_cols = slice(C_V2, C_V2 + KV2_W)

    def kv_window(j):
        if j == 0:
            inner = slice(0, 2 * CHUNK)
            k3 = jnp.concatenate([prev_ref[:, :KV2_W], p_ref[inner, k2_cols]], axis=0)
            v3 = jnp.concatenate([prev_ref[:, KV2_W:], p_ref[inner, v2_cols]], axis=0)
            bias = bias_ref[jnp.where(tile == 0, 0, 1)]
        elif j == n_sub - 1:
            inner = slice((j - 1) * CHUNK, (j + 1) * CHUNK)
            k3 = jnp.concatenate([p_ref[inner, k2_cols], next_ref[:, :KV2_W]], axis=0)
            v3 = jnp.concatenate([p_ref[inner, v2_cols], next_ref[:, KV2_W:]], axis=0)
            bias = bias_ref[jnp.where(tile == pl.num_programs(1) - 1, 2, 1)]
        else:
            inner = slice((j - 1) * CHUNK, (j + 2) * CHUNK)
            k3 = p_ref[inner, k2_cols]
            v3 = p_ref[inner, v2_cols]
            bias = bias_ref[1]
        return k3, v3, bias

    def attn_scores(j, g):
        k3, v3, bias = kv_window(j)
        g_low = slice(g * LANES, (g + 1) * LANES)
        g_high = slice((1 - g) * LANES, (2 - g) * LANES)
        kg = jnp.where(jnp.concatenate([lo] * 3, axis=0), k3[:, g_low], k3[:, g_high])
        v_lo = keep(lo, v3[:, g_low])
        v_hi = keep(hi, v3[:, g_high])
        qs = []
        for i in range(group):
            h = g * group + i
            qp = p_ref[j * CHUNK:(j + 1) * CHUNK,
                       C_AQ + (h // 2) * LANES:C_AQ + (h // 2 + 1) * LANES]
            qs.append(keep(lo if h % 2 == 0 else hi, qp))
        s_all = _dot_nt(jnp.concatenate(qs, axis=0), kg)
        return s_all, v_lo, v_hi, bias

    def attn_finish(j, g, s_all, v_lo, v_hi, bias):
        ps, rs = {}, {}

        def softmax_head(i):
            h = g * group + i
            s = s_all[i * CHUNK:(i + 1) * CHUNK]
            s = jnp.concatenate([s[:, :CHUNK] + bias[:, :CHUNK], s[:, CHUNK:2 * CHUNK],
                                 s[:, 2 * CHUNK:] + bias[:, 2 * CHUNK:]], axis=1)
            sink = sink_ref[h] * LOG2_E
            m = jnp.maximum(jnp.max(s, axis=-1, keepdims=True), sink)
            e = jnp.exp2(s - m)
            den = jnp.sum(e, axis=-1, keepdims=True) + jnp.exp2(sink - m)
            ps[i] = e.astype(BF16)
            rs[i] = 1.0 / den

        softmax_head(0)
        softmax_head(2)
        o_lo = _dot(jnp.concatenate([ps[0], ps[2]], axis=0), v_lo)
        softmax_head(1)
        softmax_head(3)
        o_hi = _dot(jnp.concatenate([ps[1], ps[3]], axis=0), v_hi)
        for pp in range(group // 2):
            part = slice(pp * CHUNK, (pp + 1) * CHUNK)
            pair = g * (group // 2) + pp
            y_ref[j * CHUNK:(j + 1) * CHUNK, pair * LANES:(pair + 1) * LANES] = (
                o_lo[part] * rs[2 * pp] + o_hi[part] * rs[2 * pp + 1]).astype(BF16)

    def retention_pair(j, p):
        rows = slice(j * CHUNK, (j + 1) * CHUNK)
        sl = slice(p * LANES, (p + 1) * LANES)
        col = lambda c0: slice(c0 + p * LANES, c0 + (p + 1) * LANES)
        qp, kp, vp = p_ref[rows, col(C_RQ)], p_ref[rows, col(C_RK)], p_ref[rows, col(C_RV)]
        s2 = _dot_nt(jnp.concatenate([keep(lo, qp), keep(hi, qp)], axis=0), kp)
        p0 = (s2[:CHUNK] * dm_ref[2 * p]).astype(BF16)
        p1 = (s2[CHUNK:] * dm_ref[2 * p + 1]).astype(BF16)
        qf = (qp.astype(F32) * xif_ref[:, sl]).astype(BF16)
        qb = (qp.astype(F32) * xib_ref[:, sl]).astype(BF16)
        lhs = jnp.concatenate([p0, p1, qf, qb], axis=1)
        state_f = carry_ref[p]
        rhs = jnp.concatenate(
            [keep(lo, vp), keep(hi, vp), state_f.astype(BF16), rb_ref[j, p]], axis=0)
        ret = _dot(lhs, rhs)
        kz = (kp.astype(F32) * zf_ref[:, sl]).astype(BF16)
        carry_ref[p] = gf_ref[p] * state_f + _dot_tn(kz, vp) * bdm_ref[...]
        sq = ret * ret
        ss_lo = jnp.sum(jnp.where(lo_f32, sq, 0.0), axis=-1, keepdims=True)
        ss_hi = jnp.sum(jnp.where(lo_f32, 0.0, sq), axis=-1, keepdims=True)
        ms = jnp.where(lo_f32, ss_lo, ss_hi) * (1.0 / HEAD_DIM)
        ret_n = ret * lax.rsqrt(ms + EPS) * rnw_ref[:, sl]
        gate = p_ref[rows, col(C_RG)].astype(F32)
        y_ref[rows, ATTN_W + p * LANES:ATTN_W + (p + 1) * LANES] = (
            gate * (1.0 / (1.0 + jnp.exp(-gate))) * ret_n).astype(BF16)

    units = [(j, g) for j in range(n_sub) for g in range(ATTN_KV_HEADS)]
    ret_items = [(j, p) for j in range(n_sub) for p in range(PAIRS)]
    assert len(units) * RET_PAIRS_PER_UNIT >= len(ret_items)
    pending = attn_scores(*units[0])
    for u, (j, g) in enumerate(units):
        following = attn_scores(*units[u + 1]) if u + 1 < len(units) else None
        for item in ret_items[u * RET_PAIRS_PER_UNIT:(u + 1) * RET_PAIRS_PER_UNIT]:
            retention_pair(*item)
        attn_finish(j, g, *pending)
        pending = following
    o_ref[...] = x_ref[...] + _dot(y_ref[...], wo_ref[...])


def _mix_call(sink, proj3, rb, x3, w_out, bias, dmat, xif, xib, rnw, zf, gf, bdm):
    batch, seq, d_model = x3.shape
    tm = MIX_TM
    n_sub = tm // CHUNK
    nt = seq // tm
    nc = seq // CHUNK
    kv_cols = C_K2 // (2 * KV2_W)
    assert C_K2 % (2 * KV2_W) == 0 and C_V2 == C_K2 + KV2_W
    kv_prev = pl.BlockSpec(
        (None, CHUNK, 2 * KV2_W), lambda b, t: (b, jnp.maximum(t * n_sub - 1, 0), kv_cols))
    kv_next = pl.BlockSpec(
        (None, CHUNK, 2 * KV2_W), lambda b, t: (b, jnp.minimum((t + 1) * n_sub, nc - 1), kv_cols))
    st = pl.BlockSpec((None, n_sub, PAIRS, LANES, LANES), lambda b, t: (b, t, 0, 0, 0))
    return pl.pallas_call(
        _mix_kernel,
        grid=(batch, nt),
        in_specs=[
            pl.BlockSpec(memory_space=pltpu.SMEM),
            pl.BlockSpec((None, tm, PROJ_W), lambda b, t: (b, t, 0)),
            kv_prev, kv_next,
            st,
            pl.BlockSpec((None, tm, d_model), lambda b, t: (b, t, 0)),
            _resident((d_model, d_model)),
            _resident((3, CHUNK, 3 * CHUNK)),
            _resident((RET_HEADS, CHUNK, CHUNK)),
            _resident((CHUNK, RET_W)),
            _resident((CHUNK, RET_W)),
            _resident((1, RET_W)),
            _resident((CHUNK, RET_W)),
            _resident((PAIRS, LANES, LANES)),
            _resident((LANES, LANES)),
        ],
        out_specs=pl.BlockSpec((None, tm, d_model), lambda b, t: (b, t, 0)),
        out_shape=jax.ShapeDtypeStruct((batch, seq, d_model), F32),
        scratch_shapes=[pltpu.VMEM((tm, d_model), BF16), pltpu.VMEM((PAIRS, LANES, LANES), F32)],
        compiler_params=pltpu.CompilerParams(
            dimension_semantics=("arbitrary", "arbitrary"), vmem_limit_bytes=VMEM_LIMIT),
        name="mix",
    )(sink, proj3, proj3, proj3, rb, x3, w_out, bias, dmat, xif, xib, rnw, zf, gf, bdm)


def _ffn_kernel(h_ref, fnw_ref, wg_ref, wu_ref, wd_ref, o_ref, a_ref):
    h = h_ref[...]
    ms = jnp.mean(h * h, axis=-1, keepdims=True)
    m = (h * lax.rsqrt(ms + EPS) * fnw_ref[...]).astype(BF16)
    d_ff = wg_ref.shape[1]
    for c0 in range(0, d_ff, FFN_CK):
        c1 = min(c0 + FFN_CK, d_ff)
        g = _dot(m, wg_ref[:, c0:c1])
        u = _dot(m, wu_ref[:, c0:c1])
        a_ref[:, c0:c1] = (g * (1.0 / (1.0 + jnp.exp(-g))) * u).astype(BF16)
    o_ref[...] = h + _dot(a_ref[...], wd_ref[...])


def _ffn_call(h2, fnw, wg, wu, wd):
    tokens, d_model = h2.shape
    d_ff = wg.shape[1]
    tm = FFN_TM
    row = lambda i: (i, 0)
    return pl.pallas_call(
        _ffn_kernel,
        grid=(tokens // tm,),
        in_specs=[
            pl.BlockSpec((tm, d_model), row),
            _resident((1, d_model)),
            _resident((d_model, d_ff)),
            _resident((d_model, d_ff)),
            _resident((d_ff, d_model)),
        ],
        out_specs=pl.BlockSpec((tm, d_model), row),
        out_shape=jax.ShapeDtypeStruct((tokens, d_model), F32),
        scratch_shapes=[pltpu.VMEM((tm, d_ff), BF16)],
        compiler_params=pltpu.CompilerParams(
            dimension_semantics=("arbitrary",), vmem_limit_bytes=VMEM_LIMIT),
        name="ffn",
    )(h2, fnw, wg, wu, wd)


def _rope_tables(seq):
    inv_freq = ROPE_THETA ** (-np.arange(0, HEAD_DIM, 2, dtype=np.float64) / HEAD_DIM)
    ang = np.arange(seq, dtype=np.float64)[:, None] * inv_freq[None, :]
    cos, sin = np.cos(ang), np.sin(ang)
    zeros = np.zeros_like(sin)
    reps = LANES // HEAD_DIM
    cos_t = np.tile(np.concatenate([cos, cos], -1), (1, reps))
    sina_t = np.tile(np.concatenate([-sin, zeros], -1), (1, reps))
    sinb_t = np.tile(np.concatenate([zeros, sin], -1), (1, reps))
    return tuple(jnp.asarray(t.astype(np.float32)) for t in (cos_t, sina_t, sinb_t))


def _attn_bias():
    i = np.arange(CHUNK)[:, None]
    j = np.arange(3 * CHUNK)[None, :]
    band = np.abs(i + CHUNK - j) <= CHUNK
    first = band & (j >= CHUNK)
    last = band & (j < 2 * CHUNK)
    tab = np.stack([first, band, last])
    return jnp.asarray(np.where(tab, 0.0, NEG_INF).astype(np.float32))


def _retention_tables(log_f, log_b):
    idx = jnp.arange(CHUNK, dtype=F32)
    diff = idx[:, None] - idx[None, :]
    lf, lb = log_f[:, None, None], log_b[:, None, None]
    dmat = jnp.where(diff[None] >= 0,
                     jnp.exp(lf * jnp.maximum(diff, 0.0)[None]),
                     jnp.exp(lb * jnp.maximum(-diff, 0.0)[None]))
    pos = np.arange(CHUNK, dtype=np.float32)[:, None]
    lane_f = jnp.repeat(log_f, HEAD_DIM)[None, :]
    lane_b = jnp.repeat(log_b, HEAD_DIM)[None, :]
    xif = jnp.exp(lane_f * (pos + 1.0))
    xib = jnp.exp(lane_b * (CHUNK - pos))
    zf = jnp.exp(lane_f * (CHUNK - 1.0 - pos))
    zb = jnp.exp(lane_b * pos)
    per_row = lambda lg: jnp.broadcast_to(
        jnp.exp(jnp.repeat(lg.reshape(PAIRS, 2), HEAD_DIM, axis=1) * CHUNK)[:, :, None],
        (PAIRS, LANES, LANES))
    return dmat, xif, xib, zf, zb, per_row(log_f), per_row(log_b)


def _block_diag_ones(width, dtype):
    r = np.arange(width) // HEAD_DIM
    return jnp.asarray((r[:, None] == r[None, :]).astype(np.float32), dtype=dtype)


def kernel(x, attn_norm_w, w_in, q_norm_w, k_norm_w, attn_sink, ret_log_decay_fwd,
           ret_log_decay_bwd, ret_norm_w, w_out, ffn_norm_w, w_gate, w_up, w_down):
    batch, seq, d_model = x.shape
    depth = w_in.shape[0]
    assert seq % PROJ_TM == 0 and seq % MIX_TM == 0 and (batch * seq) % FFN_TM == 0
    assert MIX_TM // CHUNK >= 2

    cos_t, sina_t, sinb_t = _rope_tables(seq)
    bias = _attn_bias()
    bd = _block_diag_ones(RET_W, BF16)
    bdm = _block_diag_ones(LANES, F32)

    h = x
    for l in range(depth):
        qnw = jnp.tile(q_norm_w[l], ATTN_HEADS)[None, :]
        knw = jnp.tile(k_norm_w[l], ATTN_KV_HEADS)[None, :]
        log_f = -jnp.abs(ret_log_decay_fwd[l].astype(F32))
        log_b = -jnp.abs(ret_log_decay_bwd[l].astype(F32))
        dmat, xif, xib, zf, zb, gf, gb = _retention_tables(log_f, log_b)

        proj, wg_bf, wu_bf, wd_bf, wo_bf, rb = _proj_call(
            h.reshape(batch * seq, d_model), attn_norm_w[l][None, :], w_in[l], qnw, knw,
            cos_t, sina_t, sinb_t, bd, seq, (w_gate[l], w_up[l], w_down[l], w_out[l]), zb, gb, bdm)
        proj3 = proj.reshape(batch, seq, PROJ_W)
        rb = rb.reshape(batch, seq // CHUNK, PAIRS, LANES, LANES)
        h = _mix_call(attn_sink[l].astype(F32), proj3, rb, h, wo_bf,
                      bias, dmat, xif, xib, ret_norm_w[l][None, :], zf, gf, bdm)
        h = _ffn_call(h.reshape(batch * seq, d_model), ffn_norm_w[l][None, :],
                      wg_bf, wu_bf, wd_bf).reshape(batch, seq, d_model)
    return h
```

```python
import functools

import jax
import jax.numpy as jnp
import numpy as np
from jax import lax
from jax.experimental import pallas as pl
from jax.experimental.pallas import tpu as pltpu

HEAD_DIM = 64
ATTN_HEADS = 8
ATTN_KV_HEADS = 2
RET_HEADS = 8
CHUNK = 128
ROPE_THETA = 10000.0
EPS = 1e-6
NEG_INF = -1e30
LOG2_E = 1.4426950408889634

LANES = 128
BF16_SUBLANES = 16
PAIRS = RET_HEADS // 2
ATTN_W = ATTN_HEADS * HEAD_DIM
RET_W = RET_HEADS * HEAD_DIM
KV2_W = 2 * ATTN_KV_HEADS * HEAD_DIM

C_AQ = 0
C_K2 = C_AQ + ATTN_W
C_V2 = C_K2 + KV2_W
C_RK = C_V2 + KV2_W
C_RV = C_RK + RET_W
C_RQ = C_RV + RET_W
C_RG = C_RQ + RET_W
PROJ_W = C_RG + RET_W

PROJ_TM = 1024
MIX_TM = 1024
RET_PAIRS_PER_UNIT = 1
FFN_TM = 1024
FFN_CK = 512
VMEM_LIMIT = 56 * 1024 * 1024

BF16 = jnp.bfloat16
F32 = jnp.float32


def _dot(a, b):
    return jnp.dot(a, b, preferred_element_type=F32)


def _dot_nt(a, b):
    return lax.dot_general(a, b, (((1,), (1,)), ((), ())), preferred_element_type=F32)


def _dot_tn(a, b):
    return lax.dot_general(a, b, (((0,), (0,)), ((), ())), preferred_element_type=F32)


def _resident(shape):
    zeros = (0,) * len(shape)
    return pl.BlockSpec(shape, lambda *_: zeros, pipeline_mode=pl.Buffered(1))


def _proj_kernel(tiles_per_seq, x_ref, anw_ref, w_ref, qnw_ref, knw_ref, cos_ref, sina_ref,
                 sinb_ref, bd_ref, wg_ref, wu_ref, wd_ref, wo_ref, zb_ref, gb_ref, bdm_ref,
                 o_ref, wg_bf_ref, wu_bf_ref, wd_bf_ref, wo_bf_ref, rb_ref, carry_ref):
    @pl.when(pl.program_id(0) % tiles_per_seq == 0)
    def _():
        carry_ref[...] = jnp.zeros_like(carry_ref)

    x = x_ref[...]
    ms = jnp.mean(x * x, axis=-1, keepdims=True)
    n = x * lax.rsqrt(ms + EPS) * anw_ref[...]
    cos = cos_ref[...]
    sina = sina_ref[...]
    sinb = sinb_ref[...]

    def proj(c0, width):
        return _dot(n, w_ref[:, c0:c0 + width])

    def head_norm(y, w):
        width = y.shape[-1]
        ss = _dot((y * y).astype(BF16), bd_ref[:width, :width])
        return y * lax.rsqrt(ss * (1.0 / HEAD_DIM) + EPS) * w

    def rope(ys):
        return ys * cos + pltpu.roll(ys, LANES - 32, 1) * sina + pltpu.roll(ys, 32, 1) * sinb

    def rope_store(y, c0, scale):
        for s in range(y.shape[-1] // LANES):
            r = rope(y[:, s * LANES:(s + 1) * LANES])
            if scale != 1.0:
                r = r * scale
            o_ref[:, c0 + s * LANES:c0 + (s + 1) * LANES] = r.astype(BF16)

    def plain_store(y, c0):
        o_ref[:, c0:c0 + y.shape[-1]] = y.astype(BF16)

    def store_both_orders(y, c0):
        o_ref[:, c0:c0 + LANES] = y.astype(BF16)
        o_ref[:, c0 + LANES:c0 + 2 * LANES] = pltpu.roll(y, HEAD_DIM, 1).astype(BF16)

    def kv_finish(y):
        store_both_orders(rope(head_norm(y[:, :LANES], knw_ref[...])), C_K2)
        store_both_orders(y[:, LANES:], C_V2)

    scale = HEAD_DIM ** -0.5
    kvw = ATTN_KV_HEADS * HEAD_DIM
    in_kv = ATTN_W
    in_rq = in_kv + 2 * kvw
    def backward_states():
        bdm = bdm_ref[...]
        for p in range(PAIRS):
            sl = slice(p * LANES, (p + 1) * LANES)
            carry = carry_ref[p]
            for c in reversed(range(o_ref.shape[0] // CHUNK)):
                rows = slice(c * CHUNK, (c + 1) * CHUNK)
                k = o_ref[rows, C_RK + p * LANES:C_RK + (p + 1) * LANES].astype(F32) * zb_ref[:, sl]
                kv = _dot_tn(k.astype(BF16), o_ref[rows, C_RV + p * LANES:C_RV + (p + 1) * LANES])
                rb_ref[c, p] = carry.astype(BF16)
                carry = gb_ref[p] * carry + kv * bdm
            carry_ref[p] = carry

    def rq_finish(y):
        rope_store(y, C_RQ, 1.0)
        backward_states()

    stages = [
        (0, ATTN_W, lambda y: rope_store(head_norm(y, qnw_ref[...]), C_AQ, scale * LOG2_E)),
        (in_rq + RET_W, RET_W, lambda y: rope_store(y, C_RK, scale)),
        (in_rq + 2 * RET_W, RET_W, lambda y: plain_store(y, C_RV)),
        (in_kv, 2 * kvw, kv_finish),
        (in_rq, RET_W, rq_finish),
        (in_rq + 3 * RET_W, RET_W, lambda y: plain_store(y, C_RG)),
    ]
    pending = proj(stages[0][0], stages[0][1])
    wg_bf_ref[...] = wg_ref[...].astype(BF16)
    wu_bf_ref[...] = wu_ref[...].astype(BF16)
    wd_bf_ref[...] = wd_ref[...].astype(BF16)
    wo_bf_ref[...] = wo_ref[...].astype(BF16)
    for i, (_, _, finish) in enumerate(stages):
        following = proj(stages[i + 1][0], stages[i + 1][1]) if i + 1 < len(stages) else None
        finish(pending)
        pending = following


def _slab_spec(weight, n_steps):
    rows, cols = weight.shape
    n_slabs = n_steps
    while rows % n_slabs or (rows // n_slabs) % BF16_SUBLANES or n_steps % n_slabs:
        n_slabs -= 1
    hold = n_steps // n_slabs
    return pl.BlockSpec((rows // n_slabs, cols), lambda i: (i // hold, 0))


def _proj_call(x2, anw, w_in, qnw, knw, cos, sina, sinb, bd, seq, ffn_weights, zb, gb, bdm):
    tokens, d_model = x2.shape
    tm = PROJ_TM
    n_steps = tokens // tm
    tiles_per_seq = seq // tm
    n_sub = tm // CHUNK
    row = lambda i: (n_steps - 1 - i, 0)
    pos = lambda i: ((n_steps - 1 - i) % tiles_per_seq, 0)
    slabs = [_slab_spec(w, n_steps) for w in ffn_weights]
    state_blk = (n_sub, PAIRS, LANES, LANES)
    return pl.pallas_call(
        functools.partial(_proj_kernel, tiles_per_seq),
        grid=(n_steps,),
        in_specs=[
            pl.BlockSpec((tm, d_model), row),
            _resident((1, d_model)),
            _resident(w_in.shape),
            _resident((1, ATTN_W)),
            _resident((1, LANES)),
            pl.BlockSpec((tm, LANES), pos),
            pl.BlockSpec((tm, LANES), pos),
            pl.BlockSpec((tm, LANES), pos),
            _resident((ATTN_W, ATTN_W)),
            *slabs,
            _resident((CHUNK, RET_W)),
            _resident((PAIRS, LANES, LANES)),
            _resident((LANES, LANES)),
        ],
        out_specs=[pl.BlockSpec((tm, PROJ_W), row), *slabs,
                   pl.BlockSpec(state_blk, lambda i: (n_steps - 1 - i, 0, 0, 0))],
        out_shape=[jax.ShapeDtypeStruct((tokens, PROJ_W), BF16),
                   *[jax.ShapeDtypeStruct(w.shape, BF16) for w in ffn_weights],
                   jax.ShapeDtypeStruct((tokens // CHUNK, PAIRS, LANES, LANES), BF16)],
        scratch_shapes=[pltpu.VMEM((PAIRS, LANES, LANES), F32)],
        compiler_params=pltpu.CompilerParams(
            dimension_semantics=("arbitrary",), vmem_limit_bytes=VMEM_LIMIT),
        name="proj",
    )(x2, anw, w_in, qnw, knw, cos, sina, sinb, bd, *ffn_weights, zb, gb, bdm)


def _mix_kernel(sink_ref, p_ref, prev_ref, next_ref, rb_ref, x_ref, wo_ref, bias_ref, dm_ref,
                xif_ref, xib_ref, rnw_ref, zf_ref, gf_ref, bdm_ref, o_ref, y_ref, carry_ref):
    lane = lax.broadcasted_iota(jnp.int32, (CHUNK, LANES), 1).astype(F32).astype(BF16)
    lo = lane < HEAD_DIM
    hi = jnp.logical_not(lo)
    lo_f32 = lax.broadcasted_iota(jnp.int32, (CHUNK, LANES), 1) < HEAD_DIM
    zero = jnp.zeros((), BF16)

    def keep(mask, a):
        reps = a.shape[0] // CHUNK
        m = mask if reps == 1 else jnp.concatenate([mask] * reps, axis=0)
        return jnp.where(m, a, zero)

    tile = pl.program_id(1)

    @pl.when(tile == 0)
    def _():
        carry_ref[...] = jnp.zeros_like(carry_ref)

    n_sub = p_ref.shape[0] // CHUNK
    group = ATTN_HEADS // ATTN_KV_HEADS
    k2_cols = slice(C_K2, C_K2 + KV2_W)
    v2_cols = slice(C_V2, C_V2 + KV2_W)

    def kv_window(j):
        if j == 0:
            inner = slice(0, 2 * CHUNK)
            k3 = jnp.concatenate([prev_ref[:, :KV2_W], p_ref[inner, k2_cols]], axis=0)
            v3 = jnp.concatenate([prev_ref[:, KV2_W:], p_ref[inner, v2_cols]], axis=0)
            bias = bias_ref[jnp.where(tile == 0, 0, 1)]
        elif j == n_sub - 1:
            inner = slice((j - 1) * CHUNK, (j + 1) * CHUNK)
            k3 = jnp.concatenate([p_ref[inner, k2_cols], next_ref[:, :KV2_W]], axis=0)
            v3 = jnp.concatenate([p_ref[inner, v2_cols], next_ref[:, KV2_W:]], axis=0)
            bias = bias_ref[jnp.where(tile == pl.num_programs(1) - 1, 2, 1)]
        else:
            inner = slice((j - 1) * CHUNK, (j + 2) * CHUNK)
            k3 = p_ref[inner, k2_cols]
            v3 = p_ref[inner, v2_cols]
            bias = bias_ref[1]
        return k3, v3, bias

    kv_cache = {}

    def kv_operands(j, g):
        if (j, g) not in kv_cache:
            k3, v3, bias = kv_window(j)
            g_low = slice(g * LANES, (g + 1) * LANES)
            g_high = slice((1 - g) * LANES, (2 - g) * LANES)
            kg = jnp.where(jnp.concatenate([lo] * 3, axis=0), k3[:, g_low], k3[:, g_high])
            kv_cache[(j, g)] = (kg,
                                keep(lo, v3[:, g_low]),
                                keep(hi, v3[:, g_high]),
                                bias)
        return kv_cache[(j, g)]

    def attn_scores(j, pair):
        kg, v_lo, v_hi, bias = kv_operands(j, pair // (PAIRS // ATTN_KV_HEADS))
        qp = p_ref[j * CHUNK:(j + 1) * CHUNK, C_AQ + pair * LANES:C_AQ + (pair + 1) * LANES]
        s_all = _dot_nt(jnp.concatenate([keep(lo, qp), keep(hi, qp)], axis=0), kg)
        return s_all, v_lo, v_hi, bias

    def attn_finish(j, pair, s_all, v_lo, v_hi, bias):
        outs = []
        for i, v_half in enumerate((v_lo, v_hi)):
            s = s_all[i * CHUNK:(i + 1) * CHUNK]
            s = jnp.concatenate([s[:, :CHUNK] + bias[:, :CHUNK], s[:, CHUNK:2 * CHUNK],
                                 s[:, 2 * CHUNK:] + bias[:, 2 * CHUNK:]], axis=1)
            sink = sink_ref[2 * pair + i] * LOG2_E
            m = jnp.maximum(jnp.max(s, axis=-1, keepdims=True), sink)
            e = jnp.exp2(s - m)
            den = jnp.sum(e, axis=-1, keepdims=True) + jnp.exp2(sink - m)
            outs.append(_dot(e.astype(BF16), v_half) * (1.0 / den))
        y_ref[j * CHUNK:(j + 1) * CHUNK, pair * LANES:(pair + 1) * LANES] = (
            outs[0] + outs[1]).astype(BF16)

    def retention_pair(j, p):
        rows = slice(j * CHUNK, (j + 1) * CHUNK)
        sl = slice(p * LANES, (p + 1) * LANES)
        col = lambda c0: slice(c0 + p * LANES, c0 + (p + 1) * LANES)
        qp, kp, vp = p_ref[rows, col(C_RQ)], p_ref[rows, col(C_RK)], p_ref[rows, col(C_RV)]
        s2 = _dot_nt(jnp.concatenate([keep(lo, qp), keep(hi, qp)], axis=0), kp)
        p0 = (s2[:CHUNK] * dm_ref[2 * p]).astype(BF16)
        p1 = (s2[CHUNK:] * dm_ref[2 * p + 1]).astype(BF16)
        qf = (qp.astype(F32) * xif_ref[:, sl]).astype(BF16)
        qb = (qp.astype(F32) * xib_ref[:, sl]).astype(BF16)
        lhs = jnp.concatenate([p0, p1, qf, qb], axis=1)
        state_f = carry_ref[p]
        rhs = jnp.concatenate(
            [keep(lo, vp), keep(hi, vp), state_f.astype(BF16), rb_ref[j, p]], axis=0)
        ret = _dot(lhs, rhs)
        kz = (kp.astype(F32) * zf_ref[:, sl]).astype(BF16)
        carry_ref[p] = gf_ref[p] * state_f + _dot_tn(kz, vp) * bdm_ref[...]
        sq = ret * ret
        ss_lo = jnp.sum(jnp.where(lo_f32, sq, 0.0), axis=-1, keepdims=True)
        ss_hi = jnp.sum(jnp.where(lo_f32, 0.0, sq), axis=-1, keepdims=True)
        ms = jnp.where(lo_f32, ss_lo, ss_hi) * (1.0 / HEAD_DIM)
        ret_n = ret * lax.rsqrt(ms + EPS) * rnw_ref[:, sl]
        gate = p_ref[rows, col(C_RG)].astype(F32)
        y_ref[rows, ATTN_W + p * LANES:ATTN_W + (p + 1) * LANES] = (
            gate * (1.0 / (1.0 + jnp.exp(-gate))) * ret_n).astype(BF16)

    units = [(j, pair) for j in range(n_sub) for pair in range(ATTN_HEADS // 2)]
    ret_items = [(j, p) for j in range(n_sub) for p in range(PAIRS)]
    assert len(units) * RET_PAIRS_PER_UNIT >= len(ret_items)
    pending = attn_scores(*units[0])
    for u, (j, g) in enumerate(units):
        following = attn_scores(*units[u + 1]) if u + 1 < len(units) else None
        for item in ret_items[u * RET_PAIRS_PER_UNIT:(u + 1) * RET_PAIRS_PER_UNIT]:
            retention_pair(*item)
        attn_finish(j, g, *pending)
        pending = following
    o_ref[...] = x_ref[...] + _dot(y_ref[...], wo_ref[...])


def _mix_call(sink, proj3, rb, x3, w_out, bias, dmat, xif, xib, rnw, zf, gf, bdm):
    batch, seq, d_model = x3.shape
    tm = MIX_TM
    n_sub = tm // CHUNK
    nt = seq // tm
    nc = seq // CHUNK
    kv_cols = C_K2 // (2 * KV2_W)
    assert C_K2 % (2 * KV2_W) == 0 and C_V2 == C_K2 + KV2_W
    kv_prev = pl.BlockSpec(
        (None, CHUNK, 2 * KV2_W), lambda b, t: (b, jnp.maximum(t * n_sub - 1, 0), kv_cols))
    kv_next = pl.BlockSpec(
        (None, CHUNK, 2 * KV2_W), lambda b, t: (b, jnp.minimum((t + 1) * n_sub, nc - 1), kv_cols))
    st = pl.BlockSpec((None, n_sub, PAIRS, LANES, LANES), lambda b, t: (b, t, 0, 0, 0))
    return pl.pallas_call(
        _mix_kernel,
        grid=(batch, nt),
        in_specs=[
            pl.BlockSpec(memory_space=pltpu.SMEM),
            pl.BlockSpec((None, tm, PROJ_W), lambda b, t: (b, t, 0)),
            kv_prev, kv_next,
            st,
            pl.BlockSpec((None, tm, d_model), lambda b, t: (b, t, 0)),
            _resident((d_model, d_model)),
            _resident((3, CHUNK, 3 * CHUNK)),
            _resident((RET_HEADS, CHUNK, CHUNK)),
            _resident((CHUNK, RET_W)),
            _resident((CHUNK, RET_W)),
            _resident((1, RET_W)),
            _resident((CHUNK, RET_W)),
            _resident((PAIRS, LANES, LANES)),
            _resident((LANES, LANES)),
        ],
        out_specs=pl.BlockSpec((None, tm, d_model), lambda b, t: (b, t, 0)),
        out_shape=jax.ShapeDtypeStruct((batch, seq, d_model), F32),
        scratch_shapes=[pltpu.VMEM((tm, d_model), BF16), pltpu.VMEM((PAIRS, LANES, LANES), F32)],
        compiler_params=pltpu.CompilerParams(
            dimension_semantics=("arbitrary", "arbitrary"), vmem_limit_bytes=VMEM_LIMIT),
        name="mix",
    )(sink, proj3, proj3, proj3, rb, x3, w_out, bias, dmat, xif, xib, rnw, zf, gf, bdm)


def _ffn_kernel(h_ref, fnw_ref, wg_ref, wu_ref, wd_ref, o_ref, a_ref):
    h = h_ref[...]
    ms = jnp.mean(h * h, axis=-1, keepdims=True)
    m = (h * lax.rsqrt(ms + EPS) * fnw_ref[...]).astype(BF16)
    d_ff = wg_ref.shape[1]
    for c0 in range(0, d_ff, FFN_CK):
        c1 = min(c0 + FFN_CK, d_ff)
        g = _dot(m, wg_ref[:, c0:c1])
        u = _dot(m, wu_ref[:, c0:c1])
        a_ref[:, c0:c1] = (g * (1.0 / (1.0 + jnp.exp(-g))) * u).astype(BF16)
    o_ref[...] = h + _dot(a_ref[...], wd_ref[...])


def _ffn_call(h2, fnw, wg, wu, wd):
    tokens, d_model = h2.shape
    d_ff = wg.shape[1]
    tm = FFN_TM
    row = lambda i: (i, 0)
    return pl.pallas_call(
        _ffn_kernel,
        grid=(tokens // tm,),
        in_specs=[
            pl.BlockSpec((tm, d_model), row),
            _resident((1, d_model)),
            _resident((d_model, d_ff)),
            _resident((d_model, d_ff)),
            _resident((d_ff, d_model)),
        ],
        out_specs=pl.BlockSpec((tm, d_model), row),
        out_shape=jax.ShapeDtypeStruct((tokens, d_model), F32),
        scratch_shapes=[pltpu.VMEM((tm, d_ff), BF16)],
        compiler_params=pltpu.CompilerParams(
            dimension_semantics=("arbitrary",), vmem_limit_bytes=VMEM_LIMIT),
        name="ffn",
    )(h2, fnw, wg, wu, wd)


def _rope_tables(seq):
    inv_freq = ROPE_THETA ** (-np.arange(0, HEAD_DIM, 2, dtype=np.float64) / HEAD_DIM)
    ang = np.arange(seq, dtype=np.float64)[:, None] * inv_freq[None, :]
    cos, sin = np.cos(ang), np.sin(ang)
    zeros = np.zeros_like(sin)
    reps = LANES // HEAD_DIM
    cos_t = np.tile(np.concatenate([cos, cos], -1), (1, reps))
    sina_t = np.tile(np.concatenate([-sin, zeros], -1), (1, reps))
    sinb_t = np.tile(np.concatenate([zeros, sin], -1), (1, reps))
    return tuple(jnp.asarray(t.astype(np.float32)) for t in (cos_t, sina_t, sinb_t))


def _attn_bias():
    i = np.arange(CHUNK)[:, None]
    j = np.arange(3 * CHUNK)[None, :]
    band = np.abs(i + CHUNK - j) <= CHUNK
    first = band & (j >= CHUNK)
    last = band & (j < 2 * CHUNK)
    tab = np.stack([first, band, last])
    return jnp.asarray(np.where(tab, 0.0, NEG_INF).astype(np.float32))


def _retention_tables(log_f, log_b):
    idx = jnp.arange(CHUNK, dtype=F32)
    diff = idx[:, None] - idx[None, :]
    lf, lb = log_f[:, None, None], log_b[:, None, None]
    dmat = jnp.where(diff[None] >= 0,
                     jnp.exp(lf * jnp.maximum(diff, 0.0)[None]),
                     jnp.exp(lb * jnp.maximum(-diff, 0.0)[None]))
    pos = np.arange(CHUNK, dtype=np.float32)[:, None]
    lane_f = jnp.repeat(log_f, HEAD_DIM)[None, :]
    lane_b = jnp.repeat(log_b, HEAD_DIM)[None, :]
    xif = jnp.exp(lane_f * (pos + 1.0))
    xib = jnp.exp(lane_b * (CHUNK - pos))
    zf = jnp.exp(lane_f * (CHUNK - 1.0 - pos))
    zb = jnp.exp(lane_b * pos)
    per_row = lambda lg: jnp.broadcast_to(
        jnp.exp(jnp.repeat(lg.reshape(PAIRS, 2), HEAD_DIM, axis=1) * CHUNK)[:, :, None],
        (PAIRS, LANES, LANES))
    return dmat, xif, xib, zf, zb, per_row(log_f), per_row(log_b)


def _block_diag_ones(width, dtype):
    r = np.arange(width) // HEAD_DIM
    return jnp.asarray((r[:, None] == r[None, :]).astype(np.float32), dtype=dtype)


def kernel(x, attn_norm_w, w_in, q_norm_w, k_norm_w, attn_sink, ret_log_decay_fwd,
           ret_log_decay_bwd, ret_norm_w, w_out, ffn_norm_w, w_gate, w_up, w_down):
    batch, seq, d_model = x.shape
    depth = w_in.shape[0]
    assert seq % PROJ_TM == 0 and seq % MIX_TM == 0 and (batch * seq) % FFN_TM == 0
    assert MIX_TM // CHUNK >= 2

    cos_t, sina_t, sinb_t = _rope_tables(seq)
    bias = _attn_bias()
    bd = _block_diag_ones(RET_W, BF16)
    bdm = _block_diag_ones(LANES, F32)

    h = x
    for l in range(depth):
        qnw = jnp.tile(q_norm_w[l], ATTN_HEADS)[None, :]
        knw = jnp.tile(k_norm_w[l], ATTN_KV_HEADS)[None, :]
        log_f = -jnp.abs(ret_log_decay_fwd[l].astype(F32))
        log_b = -jnp.abs(ret_log_decay_bwd[l].astype(F32))
        dmat, xif, xib, zf, zb, gf, gb = _retention_tables(log_f, log_b)

        proj, wg_bf, wu_bf, wd_bf, wo_bf, rb = _proj_call(
            h.reshape(batch * seq, d_model), attn_norm_w[l][None, :], w_in[l], qnw, knw,
            cos_t, sina_t, sinb_t, bd, seq, (w_gate[l], w_up[l], w_down[l], w_out[l]), zb, gb, bdm)
        proj3 = proj.reshape(batch, seq, PROJ_W)
        rb = rb.reshape(batch, seq // CHUNK, PAIRS, LANES, LANES)
        h = _mix_call(attn_sink[l].astype(F32), proj3, rb, h, wo_bf,
                      bias, dmat, xif, xib, ret_norm_w[l][None, :], zf, gf, bdm)
        h = _ffn_call(h.reshape(batch * seq, d_model), ffn_norm_w[l][None, :],
                      wg_bf, wu_bf, wd_bf).reshape(batch, seq, d_model)
    return h
```

```python
import functools

import jax
import jax.numpy as jnp
import numpy as np
from jax import lax
from jax.experimental import pallas as pl
from jax.experimental.pallas import tpu as pltpu

HEAD_DIM = 64
ATTN_HEADS = 8
ATTN_KV_HEADS = 2
RET_HEADS = 8
CHUNK = 128
ROPE_THETA = 10000.0
EPS = 1e-6
NEG_INF = -1e30
LOG2_E = 1.4426950408889634

LANES = 128
BF16_SUBLANES = 16
PAIRS = RET_HEADS // 2
ATTN_W = ATTN_HEADS * HEAD_DIM
RET_W = RET_HEADS * HEAD_DIM
KV2_W = 2 * ATTN_KV_HEADS * HEAD_DIM

C_AQ = 0
C_K2 = C_AQ + ATTN_W
C_V2 = C_K2 + KV2_W
C_RK = C_V2 + KV2_W
C_RV = C_RK + RET_W
C_RQ = C_RV + RET_W
C_RG = C_RQ + RET_W
PROJ_W = C_RG + RET_W

PROJ_TM = 1024
MIX_TM = 1024
RET_PAIRS_PER_UNIT = 1
FFN_TM = 1024
FFN_CK = 512
VMEM_LIMIT = 56 * 1024 * 1024

BF16 = jnp.bfloat16
F32 = jnp.float32


def _dot(a, b):
    return jnp.dot(a, b, preferred_element_type=F32)


def _dot_nt(a, b):
    return lax.dot_general(a, b, (((1,), (1,)), ((), ())), preferred_element_type=F32)


def _dot_tn(a, b):
    return lax.dot_general(a, b, (((0,), (0,)), ((), ())), preferred_element_type=F32)


def _resident(shape):
    zeros = (0,) * len(shape)
    return pl.BlockSpec(shape, lambda *_: zeros, pipeline_mode=pl.Buffered(1))


def _proj_kernel(tiles_per_seq, x_ref, anw_ref, w_ref, qnw_ref, knw_ref, cos_ref, sina_ref,
                 sinb_ref, bd_ref, wg_ref, wu_ref, wd_ref, wo_ref, zb_ref, gb_ref, bdm_ref,
                 o_ref, wg_bf_ref, wu_bf_ref, wd_bf_ref, wo_bf_ref, rb_ref, carry_ref):
    @pl.when(pl.program_id(0) % tiles_per_seq == 0)
    def _():
        carry_ref[...] = jnp.zeros_like(carry_ref)

    x = x_ref[...]
    ms = jnp.mean(x * x, axis=-1, keepdims=True)
    n = x * lax.rsqrt(ms + EPS) * anw_ref[...]
    cos = cos_ref[...]
    sina = sina_ref[...]
    sinb = sinb_ref[...]

    def proj(c0, width):
        return _dot(n, w_ref[:, c0:c0 + width])

    def head_norm(y, w):
        width = y.shape[-1]
        ss = _dot((y * y).astype(BF16), bd_ref[:width, :width])
        return y * lax.rsqrt(ss * (1.0 / HEAD_DIM) + EPS) * w

    def rope(ys):
        return ys * cos + pltpu.roll(ys, LANES - 32, 1) * sina + pltpu.roll(ys, 32, 1) * sinb

    def rope_store(y, c0, scale):
        for s in range(y.shape[-1] // LANES):
            r = rope(y[:, s * LANES:(s + 1) * LANES])
            if scale != 1.0:
                r = r * scale
            o_ref[:, c0 + s * LANES:c0 + (s + 1) * LANES] = r.astype(BF16)

    def plain_store(y, c0):
        o_ref[:, c0:c0 + y.shape[-1]] = y.astype(BF16)

    def store_both_orders(y, c0):
        o_ref[:, c0:c0 + LANES] = y.astype(BF16)
        o_ref[:, c0 + LANES:c0 + 2 * LANES] = pltpu.roll(y, HEAD_DIM, 1).astype(BF16)

    def kv_finish(y):
        store_both_orders(rope(head_norm(y[:, :LANES], knw_ref[...])), C_K2)
        store_both_orders(y[:, LANES:], C_V2)

    scale = HEAD_DIM ** -0.5
    kvw = ATTN_KV_HEADS * HEAD_DIM
    in_kv = ATTN_W
    in_rq = in_kv + 2 * kvw
    def backward_states():
        bdm = bdm_ref[...]
        for p in range(PAIRS):
            sl = slice(p * LANES, (p + 1) * LANES)
            carry = carry_ref[p]
            for c in reversed(range(o_ref.shape[0] // CHUNK)):
                rows = slice(c * CHUNK, (c + 1) * CHUNK)
                k = o_ref[rows, C_RK + p * LANES:C_RK + (p + 1) * LANES].astype(F32) * zb_ref[:, sl]
                kv = _dot_tn(k.astype(BF16), o_ref[rows, C_RV + p * LANES:C_RV + (p + 1) * LANES])
                rb_ref[c, p] = carry.astype(BF16)
                carry = gb_ref[p] * carry + kv * bdm
            carry_ref[p] = carry

    def rq_finish(y):
        rope_store(y, C_RQ, 1.0)
        backward_states()

    stages = [
        (0, ATTN_W, lambda y: rope_store(head_norm(y, qnw_ref[...]), C_AQ, scale * LOG2_E)),
        (in_rq + RET_W, RET_W, lambda y: rope_store(y, C_RK, scale)),
        (in_rq + 2 * RET_W, RET_W, lambda y: plain_store(y, C_RV)),
        (in_kv, 2 * kvw, kv_finish),
        (in_rq, RET_W, rq_finish),
        (in_rq + 3 * RET_W, RET_W, lambda y: plain_store(y, C_RG)),
    ]
    pending = proj(stages[0][0], stages[0][1])
    wg_bf_ref[...] = wg_ref[...].astype(BF16)
    wu_bf_ref[...] = wu_ref[...].astype(BF16)
    wd_bf_ref[...] = wd_ref[...].astype(BF16)
    wo_bf_ref[...] = wo_ref[...].astype(BF16)
    for i, (_, _, finish) in enumerate(stages):
        following = proj(stages[i + 1][0], stages[i + 1][1]) if i + 1 < len(stages) else None
        finish(pending)
        pending = following


def _slab_spec(weight, n_steps):
    rows, cols = weight.shape
    n_slabs = n_steps
    while rows % n_slabs or (rows // n_slabs) % BF16_SUBLANES or n_steps % n_slabs:
        n_slabs -= 1
    hold = n_steps // n_slabs
    return pl.BlockSpec((rows // n_slabs, cols), lambda i: (i // hold, 0))


def _proj_call(x2, anw, w_in, qnw, knw, cos, sina, sinb, bd, seq, ffn_weights, zb, gb, bdm):
    tokens, d_model = x2.shape
    tm = PROJ_TM
    n_steps = tokens // tm
    tiles_per_seq = seq // tm
    n_sub = tm // CHUNK
    row = lambda i: (n_steps - 1 - i, 0)
    pos = lambda i: ((n_steps - 1 - i) % tiles_per_seq, 0)
    slabs = [_slab_spec(w, n_steps) for w in ffn_weights]
    state_blk = (n_sub, PAIRS, LANES, LANES)
    return pl.pallas_call(
        functools.partial(_proj_kernel, tiles_per_seq),
        grid=(n_steps,),
        in_specs=[
            pl.BlockSpec((tm, d_model), row),
            _resident((1, d_model)),
            _resident(w_in.shape),
            _resident((1, ATTN_W)),
            _resident((1, LANES)),
            pl.BlockSpec((tm, LANES), pos),
            pl.BlockSpec((tm, LANES), pos),
            pl.BlockSpec((tm, LANES), pos),
            _resident((ATTN_W, ATTN_W)),
            *slabs,
            _resident((CHUNK, RET_W)),
            _resident((PAIRS, LANES, LANES)),
            _resident((LANES, LANES)),
        ],
        out_specs=[pl.BlockSpec((tm, PROJ_W), row), *slabs,
                   pl.BlockSpec(state_blk, lambda i: (n_steps - 1 - i, 0, 0, 0))],
        out_shape=[jax.ShapeDtypeStruct((tokens, PROJ_W), BF16),
                   *[jax.ShapeDtypeStruct(w.shape, BF16) for w in ffn_weights],
                   jax.ShapeDtypeStruct((tokens // CHUNK, PAIRS, LANES, LANES), BF16)],
        scratch_shapes=[pltpu.VMEM((PAIRS, LANES, LANES), F32)],
        compiler_params=pltpu.CompilerParams(
            dimension_semantics=("arbitrary",), vmem_limit_bytes=VMEM_LIMIT),
        name="proj",
    )(x2, anw, w_in, qnw, knw, cos, sina, sinb, bd, *ffn_weights, zb, gb, bdm)


def _mix_kernel(sink_ref, p_ref, prev_ref, next_ref, rb_ref, x_ref, wo_ref, bias_ref, dm_ref,
                xif_ref, xib_ref, rnw_ref, zf_ref, gf_ref, bdm_ref, o_ref, y_ref, carry_ref):
    lane = lax.broadcasted_iota(jnp.int32, (CHUNK, LANES), 1).astype(F32).astype(BF16)
    lo = lane < HEAD_DIM
    hi = jnp.logical_not(lo)
    lo_f32 = lax.broadcasted_iota(jnp.int32, (CHUNK, LANES), 1) < HEAD_DIM
    zero = jnp.zeros((), BF16)

    def keep(mask, a):
        reps = a.shape[0] // CHUNK
        m = mask if reps == 1 else jnp.concatenate([mask] * reps, axis=0)
        return jnp.where(m, a, zero)

    tile = pl.program_id(1)

    @pl.when(tile == 0)
    def _():
        carry_ref[...] = jnp.zeros_like(carry_ref)

    n_sub = p_ref.shape[0] // CHUNK
    group = ATTN_HEADS // ATTN_KV_HEADS
    k2_cols = slice(C_K2, C_K2 + KV2_W)
    v2_cols = slice(C_V2, C_V2 + KV2_W)

    def kv_window(j):
        if j == 0:
            inner = slice(0, 2 * CHUNK)
            k3 = jnp.concatenate([prev_ref[:, :KV2_W], p_ref[inner, k2_cols]], axis=0)
            v3 = jnp.concatenate([prev_ref[:, KV2_W:], p_ref[inner, v2_cols]], axis=0)
            bias = bias_ref[jnp.where(tile == 0, 0, 1)]
        elif j == n_sub - 1:
            inner = slice((j - 1) * CHUNK, (j + 1) * CHUNK)
            k3 = jnp.concatenate([p_ref[inner, k2_cols], next_ref[:, :KV2_W]], axis=0)
            v3 = jnp.concatenate([p_ref[inner, v2_cols], next_ref[:, KV2_W:]], axis=0)
            bias = bias_ref[jnp.where(tile == pl.num_programs(1) - 1, 2, 1)]
        else:
            inner = slice((j - 1) * CHUNK, (j + 2) * CHUNK)
            k3 = p_ref[inner, k2_cols]
            v3 = p_ref[inner, v2_cols]
            bias = bias_ref[1]
        return k3, v3, bias

    kv_cache = {}

    def kv_operands(j, g):
        if (j, g) not in kv_cache:
            k3, v3, bias = kv_window(j)
            g_low = slice(g * LANES, (g + 1) * LANES)
            g_high = slice((1 - g) * LANES, (2 - g) * LANES)
            kg = jnp.where(jnp.concatenate([lo] * 3, axis=0), k3[:, g_low], k3[:, g_high])
            kv_cache[(j, g)] = (kg,
                                keep(lo, v3[:, g_low]),
                                keep(hi, v3[:, g_high]),
                                bias)
        return kv_cache[(j, g)]

    def attn_scores(j, pair):
        kg, v_lo, v_hi, bias = kv_operands(j, pair // (PAIRS // ATTN_KV_HEADS))
        qp = p_ref[j * CHUNK:(j + 1) * CHUNK, C_AQ + pair * LANES:C_AQ + (pair + 1) * LANES]
        s_all = _dot_nt(jnp.concatenate([keep(lo, qp), keep(hi, qp)], axis=0), kg)
        return s_all, v_lo, v_hi, bias

    def attn_finish(j, pair, s_all, v_lo, v_hi, bias):
        outs = []
        for i, v_half in enumerate((v_lo, v_hi)):
            s = s_all[i * CHUNK:(i + 1) * CHUNK]
            s = jnp.concatenate([s[:, :CHUNK] + bias[:, :CHUNK], s[:, CHUNK:2 * CHUNK],
                                 s[:, 2 * CHUNK:] + bias[:, 2 * CHUNK:]], axis=1)
            m = jnp.max(s, axis=-1, keepdims=True)
            e = jnp.exp2(s - m)
            den = jnp.sum(e, axis=-1, keepdims=True) + jnp.exp2(sink_ref[2 * pair + i] * LOG2_E - m)
            outs.append(_dot(e.astype(BF16), v_half) * (1.0 / den))
        y_ref[j * CHUNK:(j + 1) * CHUNK, pair * LANES:(pair + 1) * LANES] = (
            outs[0] + outs[1]).astype(BF16)

    def retention_pair(j, p):
        rows = slice(j * CHUNK, (j + 1) * CHUNK)
        sl = slice(p * LANES, (p + 1) * LANES)
        col = lambda c0: slice(c0 + p * LANES, c0 + (p + 1) * LANES)
        qp, kp, vp = p_ref[rows, col(C_RQ)], p_ref[rows, col(C_RK)], p_ref[rows, col(C_RV)]
        s2 = _dot_nt(jnp.concatenate([keep(lo, qp), keep(hi, qp)], axis=0), kp)
        p0 = (s2[:CHUNK] * dm_ref[2 * p]).astype(BF16)
        p1 = (s2[CHUNK:] * dm_ref[2 * p + 1]).astype(BF16)
        qf = (qp.astype(F32) * xif_ref[:, sl]).astype(BF16)
        qb = (qp.astype(F32) * xib_ref[:, sl]).astype(BF16)
        lhs = jnp.concatenate([p0, p1, qf, qb], axis=1)
        state_f = carry_ref[p]
        rhs = jnp.concatenate(
            [keep(lo, vp), keep(hi, vp), state_f.astype(BF16), rb_ref[j, p]], axis=0)
        ret = _dot(lhs, rhs)
        kz = (kp.astype(F32) * zf_ref[:, sl]).astype(BF16)
        carry_ref[p] = gf_ref[p] * state_f + _dot_tn(kz, vp) * bdm_ref[...]
        sq = ret * ret
        ss_lo = jnp.sum(jnp.where(lo_f32, sq, 0.0), axis=-1, keepdims=True)
        ss_hi = jnp.sum(jnp.where(lo_f32, 0.0, sq), axis=-1, keepdims=True)
        ms = jnp.where(lo_f32, ss_lo, ss_hi) * (1.0 / HEAD_DIM)
        ret_n = ret * lax.rsqrt(ms + EPS) * rnw_ref[:, sl]
        gate = p_ref[rows, col(C_RG)].astype(F32)
        y_ref[rows, ATTN_W + p * LANES:ATTN_W + (p + 1) * LANES] = (
            gate * (1.0 / (1.0 + jnp.exp(-gate))) * ret_n).astype(BF16)

    units = [(j, pair) for j in range(n_sub) for pair in range(ATTN_HEADS // 2)]
    ret_items = [(j, p) for j in range(n_sub) for p in range(PAIRS)]
    assert len(units) * RET_PAIRS_PER_UNIT >= len(ret_items)
    pending = attn_scores(*units[0])
    for u, (j, g) in enumerate(units):
        following = attn_scores(*units[u + 1]) if u + 1 < len(units) else None
        for item in ret_items[u * RET_PAIRS_PER_UNIT:(u + 1) * RET_PAIRS_PER_UNIT]:
            retention_pair(*item)
        attn_finish(j, g, *pending)
        pending = following
    o_ref[...] = x_ref[...] + _dot(y_ref[...], wo_ref[...])


def _mix_call(sink, proj3, rb, x3, w_out, bias, dmat, xif, xib, rnw, zf, gf, bdm):
    batch, seq, d_model = x3.shape
    tm = MIX_TM
    n_sub = tm // CHUNK
    nt = seq // tm
    nc = seq // CHUNK
    kv_cols = C_K2 // (2 * KV2_W)
    assert C_K2 % (2 * KV2_W) == 0 and C_V2 == C_K2 + KV2_W
    kv_prev = pl.BlockSpec(
        (None, CHUNK, 2 * KV2_W), lambda b, t: (b, jnp.maximum(t * n_sub - 1, 0), kv_cols))
    kv_next = pl.BlockSpec(
        (None, CHUNK, 2 * KV2_W), lambda b, t: (b, jnp.minimum((t + 1) * n_sub, nc - 1), kv_cols))
    st = pl.BlockSpec((None, n_sub, PAIRS, LANES, LANES), lambda b, t: (b, t, 0, 0, 0))
    return pl.pallas_call(
        _mix_kernel,
        grid=(batch, nt),
        in_specs=[
            pl.BlockSpec(memory_space=pltpu.SMEM),
            pl.BlockSpec((None, tm, PROJ_W), lambda b, t: (b, t, 0)),
            kv_prev, kv_next,
            st,
            pl.BlockSpec((None, tm, d_model), lambda b, t: (b, t, 0)),
            _resident((d_model, d_model)),
            _resident((3, CHUNK, 3 * CHUNK)),
            _resident((RET_HEADS, CHUNK, CHUNK)),
            _resident((CHUNK, RET_W)),
            _resident((CHUNK, RET_W)),
            _resident((1, RET_W)),
            _resident((CHUNK, RET_W)),
            _resident((PAIRS, LANES, LANES)),
            _resident((LANES, LANES)),
        ],
        out_specs=pl.BlockSpec((None, tm, d_model), lambda b, t: (b, t, 0)),
        out_shape=jax.ShapeDtypeStruct((batch, seq, d_model), F32),
        scratch_shapes=[pltpu.VMEM((tm, d_model), BF16), pltpu.VMEM((PAIRS, LANES, LANES), F32)],
        compiler_params=pltpu.CompilerParams(
            dimension_semantics=("arbitrary", "arbitrary"), vmem_limit_bytes=VMEM_LIMIT),
        name="mix",
    )(sink, proj3, proj3, proj3, rb, x3, w_out, bias, dmat, xif, xib, rnw, zf, gf, bdm)


def _ffn_kernel(h_ref, fnw_ref, wg_ref, wu_ref, wd_ref, o_ref, a_ref):
    h = h_ref[...]
    ms = jnp.mean(h * h, axis=-1, keepdims=True)
    m = (h * lax.rsqrt(ms + EPS) * fnw_ref[...]).astype(BF16)
    d_ff = wg_ref.shape[1]
    for c0 in range(0, d_ff, FFN_CK):
        c1 = min(c0 + FFN_CK, d_ff)
        g = _dot(m, wg_ref[:, c0:c1])
        u = _dot(m, wu_ref[:, c0:c1])
        a_ref[:, c0:c1] = (g * (1.0 / (1.0 + jnp.exp(-g))) * u).astype(BF16)
    o_ref[...] = h + _dot(a_ref[...], wd_ref[...])


def _ffn_call(h2, fnw, wg, wu, wd):
    tokens, d_model = h2.shape
    d_ff = wg.shape[1]
    tm = FFN_TM
    row = lambda i: (i, 0)
    return pl.pallas_call(
        _ffn_kernel,
        grid=(tokens // tm,),
        in_specs=[
            pl.BlockSpec((tm, d_model), row),
            _resident((1, d_model)),
            _resident((d_model, d_ff)),
            _resident((d_model, d_ff)),
            _resident((d_ff, d_model)),
        ],
        out_specs=pl.BlockSpec((tm, d_model), row),
        out_shape=jax.ShapeDtypeStruct((tokens, d_model), F32),
        scratch_shapes=[pltpu.VMEM((tm, d_ff), BF16)],
        compiler_params=pltpu.CompilerParams(
            dimension_semantics=("arbitrary",), vmem_limit_bytes=VMEM_LIMIT),
        name="ffn",
    )(h2, fnw, wg, wu, wd)


def _rope_tables(seq):
    inv_freq = ROPE_THETA ** (-np.arange(0, HEAD_DIM, 2, dtype=np.float64) / HEAD_DIM)
    ang = np.arange(seq, dtype=np.float64)[:, None] * inv_freq[None, :]
    cos, sin = np.cos(ang), np.sin(ang)
    zeros = np.zeros_like(sin)
    reps = LANES // HEAD_DIM
    cos_t = np.tile(np.concatenate([cos, cos], -1), (1, reps))
    sina_t = np.tile(np.concatenate([-sin, zeros], -1), (1, reps))
    sinb_t = np.tile(np.concatenate([zeros, sin], -1), (1, reps))
    return tuple(jnp.asarray(t.astype(np.float32)) for t in (cos_t, sina_t, sinb_t))


def _attn_bias():
    i = np.arange(CHUNK)[:, None]
    j = np.arange(3 * CHUNK)[None, :]
    band = np.abs(i + CHUNK - j) <= CHUNK
    first = band & (j >= CHUNK)
    last = band & (j < 2 * CHUNK)
    tab = np.stack([first, band, last])
    return jnp.asarray(np.where(tab, 0.0, NEG_INF).astype(np.float32))


def _retention_tables(log_f, log_b):
    idx = jnp.arange(CHUNK, dtype=F32)
    diff = idx[:, None] - idx[None, :]
    lf, lb = log_f[:, None, None], log_b[:, None, None]
    dmat = jnp.where(diff[None] >= 0,
                     jnp.exp(lf * jnp.maximum(diff, 0.0)[None]),
                     jnp.exp(lb * jnp.maximum(-diff, 0.0)[None]))
    pos = np.arange(CHUNK, dtype=np.float32)[:, None]
    lane_f = jnp.repeat(log_f, HEAD_DIM)[None, :]
    lane_b = jnp.repeat(log_b, HEAD_DIM)[None, :]
    xif = jnp.exp(lane_f * (pos + 1.0))
    xib = jnp.exp(lane_b * (CHUNK - pos))
    zf = jnp.exp(lane_f * (CHUNK - 1.0 - pos))
    zb = jnp.exp(lane_b * pos)
    per_row = lambda lg: jnp.broadcast_to(
        jnp.exp(jnp.repeat(lg.reshape(PAIRS, 2), HEAD_DIM, axis=1) * CHUNK)[:, :, None],
        (PAIRS, LANES, LANES))
    return dmat, xif, xib, zf, zb, per_row(log_f), per_row(log_b)


def _block_diag_ones(width, dtype):
    r = np.arange(width) // HEAD_DIM
    return jnp.asarray((r[:, None] == r[None, :]).astype(np.float32), dtype=dtype)


def kernel(x, attn_norm_w, w_in, q_norm_w, k_norm_w, attn_sink, ret_log_decay_fwd,
           ret_log_decay_bwd, ret_norm_w, w_out, ffn_norm_w, w_gate, w_up, w_down):
    batch, seq, d_model = x.shape
    depth = w_in.shape[0]
    assert seq % PROJ_TM == 0 and seq % MIX_TM == 0 and (batch * seq) % FFN_TM == 0
    assert MIX_TM // CHUNK >= 2

    cos_t, sina_t, sinb_t = _rope_tables(seq)
    bias = _attn_bias()
    bd = _block_diag_ones(RET_W, BF16)
    bdm = _block_diag_ones(LANES, F32)

    h = x
    for l in range(depth):
        qnw = jnp.tile(q_norm_w[l], ATTN_HEADS)[None, :]
        knw = jnp.tile(k_norm_w[l], ATTN_KV_HEADS)[None, :]
        log_f = -jnp.abs(ret_log_decay_fwd[l].astype(F32))
        log_b = -jnp.abs(ret_log_decay_bwd[l].astype(F32))
        dmat, xif, xib, zf, zb, gf, gb = _retention_tables(log_f, log_b)

        proj, wg_bf, wu_bf, wd_bf, wo_bf, rb = _proj_call(
            h.reshape(batch * seq, d_model), attn_norm_w[l][None, :], w_in[l], qnw, knw,
            cos_t, sina_t, sinb_t, bd, seq, (w_gate[l], w_up[l], w_down[l], w_out[l]), zb, gb, bdm)
        proj3 = proj.reshape(batch, seq, PROJ_W)
        rb = rb.reshape(batch, seq // CHUNK, PAIRS, LANES, LANES)
        h = _mix_call(attn_sink[l].astype(F32), proj3, rb, h, wo_bf,
                      bias, dmat, xif, xib, ret_norm_w[l][None, :], zf, gf, bdm)
        h = _ffn_call(h.reshape(batch * seq, d_model), ffn_norm_w[l][None, :],
                      wg_bf, wu_bf, wd_bf).reshape(batch, seq, d_model)
    return h
```

```python
import functools

import jax
import jax.numpy as jnp
import numpy as np
from jax import lax
from jax.experimental import pallas as pl
from jax.experimental.pallas import tpu as pltpu

HEAD_DIM = 64
ATTN_HEADS = 8
ATTN_KV_HEADS = 2
RET_HEADS = 8
CHUNK = 128
ROPE_THETA = 10000.0
EPS = 1e-6
NEG_INF = -1e30
LOG2_E = 1.4426950408889634

LANES = 128
BF16_SUBLANES = 16
PAIRS = RET_HEADS // 2
ATTN_W = ATTN_HEADS * HEAD_DIM
RET_W = RET_HEADS * HEAD_DIM
KV2_W = 2 * ATTN_KV_HEADS * HEAD_DIM

C_AQ = 0
C_K2 = C_AQ + ATTN_W
C_V2 = C_K2 + KV2_W
C_RK = C_V2 + KV2_W
C_RV = C_RK + RET_W
C_RQ = C_RV + RET_W
C_RG = C_RQ + RET_W
PROJ_W = C_RG + RET_W

PROJ_TM = 1024
MIX_TM = 1024
RET_PAIRS_PER_UNIT = 1
FFN_TM = 1024
FFN_CK = 256
VMEM_LIMIT = 56 * 1024 * 1024

BF16 = jnp.bfloat16
F32 = jnp.float32


def _dot(a, b):
    return jnp.dot(a, b, preferred_element_type=F32)


def _dot_nt(a, b):
    return lax.dot_general(a, b, (((1,), (1,)), ((), ())), preferred_element_type=F32)


def _dot_tn(a, b):
    return lax.dot_general(a, b, (((0,), (0,)), ((), ())), preferred_element_type=F32)


def _resident(shape):
    zeros = (0,) * len(shape)
    return pl.BlockSpec(shape, lambda *_: zeros, pipeline_mode=pl.Buffered(1))


def _proj_kernel(tiles_per_seq, x_ref, anw_ref, w_ref, qnw_ref, knw_ref, cos_ref, sina_ref,
                 sinb_ref, bd_ref, wg_ref, wu_ref, wd_ref, wo_ref, zb_ref, gb_ref, bdm_ref,
                 o_ref, wg_bf_ref, wu_bf_ref, wd_bf_ref, wo_bf_ref, rb_ref, carry_ref):
    @pl.when(pl.program_id(0) % tiles_per_seq == 0)
    def _():
        carry_ref[...] = jnp.zeros_like(carry_ref)

    x = x_ref[...]
    ms = jnp.mean(x * x, axis=-1, keepdims=True)
    n = x * lax.rsqrt(ms + EPS) * anw_ref[...]
    cos = cos_ref[...]
    sina = sina_ref[...]
    sinb = sinb_ref[...]

    def proj(c0, width):
        return _dot(n, w_ref[:, c0:c0 + width])

    def head_norm(y, w):
        width = y.shape[-1]
        ss = _dot((y * y).astype(BF16), bd_ref[:width, :width])
        return y * lax.rsqrt(ss * (1.0 / HEAD_DIM) + EPS) * w

    def rope(ys):
        return ys * cos + pltpu.roll(ys, LANES - 32, 1) * sina + pltpu.roll(ys, 32, 1) * sinb

    def rope_store(y, c0, scale):
        for s in range(y.shape[-1] // LANES):
            r = rope(y[:, s * LANES:(s + 1) * LANES])
            if scale != 1.0:
                r = r * scale
            o_ref[:, c0 + s * LANES:c0 + (s + 1) * LANES] = r.astype(BF16)

    def plain_store(y, c0):
        o_ref[:, c0:c0 + y.shape[-1]] = y.astype(BF16)

    def store_both_orders(y, c0):
        o_ref[:, c0:c0 + LANES] = y.astype(BF16)
        o_ref[:, c0 + LANES:c0 + 2 * LANES] = pltpu.roll(y, HEAD_DIM, 1).astype(BF16)

    def kv_finish(y):
        store_both_orders(rope(head_norm(y[:, :LANES], knw_ref[...])), C_K2)
        store_both_orders(y[:, LANES:], C_V2)

    scale = HEAD_DIM ** -0.5
    kvw = ATTN_KV_HEADS * HEAD_DIM
    in_kv = ATTN_W
    in_rq = in_kv + 2 * kvw
    def backward_states():
        bdm = bdm_ref[...]
        for p in range(PAIRS):
            sl = slice(p * LANES, (p + 1) * LANES)
            carry = carry_ref[p]
            for c in reversed(range(o_ref.shape[0] // CHUNK)):
                rows = slice(c * CHUNK, (c + 1) * CHUNK)
                k = o_ref[rows, C_RK + p * LANES:C_RK + (p + 1) * LANES].astype(F32) * zb_ref[:, sl]
                kv = _dot_tn(k.astype(BF16), o_ref[rows, C_RV + p * LANES:C_RV + (p + 1) * LANES])
                rb_ref[c, p] = carry.astype(BF16)
                carry = gb_ref[p] * carry + kv * bdm
            carry_ref[p] = carry

    def rq_finish(y):
        rope_store(y, C_RQ, 1.0)
        backward_states()

    stages = [
        (0, ATTN_W, lambda y: rope_store(head_norm(y, qnw_ref[...]), C_AQ, scale * LOG2_E)),
        (in_rq + RET_W, RET_W, lambda y: rope_store(y, C_RK, scale)),
        (in_rq + 2 * RET_W, RET_W, lambda y: plain_store(y, C_RV)),
        (in_kv, 2 * kvw, kv_finish),
        (in_rq, RET_W, rq_finish),
        (in_rq + 3 * RET_W, RET_W, lambda y: plain_store(y, C_RG)),
    ]
    pending = proj(stages[0][0], stages[0][1])
    wg_bf_ref[...] = wg_ref[...].astype(BF16)
    wu_bf_ref[...] = wu_ref[...].astype(BF16)
    wd_bf_ref[...] = wd_ref[...].astype(BF16)
    wo_bf_ref[...] = wo_ref[...].astype(BF16)
    for i, (_, _, finish) in enumerate(stages):
        following = proj(stages[i + 1][0], stages[i + 1][1]) if i + 1 < len(stages) else None
        finish(pending)
        pending = following


def _slab_spec(weight, n_steps):
    rows, cols = weight.shape
    n_slabs = n_steps
    while rows % n_slabs or (rows // n_slabs) % BF16_SUBLANES or n_steps % n_slabs:
        n_slabs -= 1
    hold = n_steps // n_slabs
    return pl.BlockSpec((rows // n_slabs, cols), lambda i: (i // hold, 0))


def _proj_call(x2, anw, w_in, qnw, knw, cos, sina, sinb, bd, seq, ffn_weights, zb, gb, bdm):
    tokens, d_model = x2.shape
    tm = PROJ_TM
    n_steps = tokens // tm
    tiles_per_seq = seq // tm
    n_sub = tm // CHUNK
    row = lambda i: (n_steps - 1 - i, 0)
    pos = lambda i: ((n_steps - 1 - i) % tiles_per_seq, 0)
    slabs = [_slab_spec(w, n_steps) for w in ffn_weights]
    state_blk = (n_sub, PAIRS, LANES, LANES)
    return pl.pallas_call(
        functools.partial(_proj_kernel, tiles_per_seq),
        grid=(n_steps,),
        in_specs=[
            pl.BlockSpec((tm, d_model), row),
            _resident((1, d_model)),
            _resident(w_in.shape),
            _resident((1, ATTN_W)),
            _resident((1, LANES)),
            pl.BlockSpec((tm, LANES), pos),
            pl.BlockSpec((tm, LANES), pos),
            pl.BlockSpec((tm, LANES), pos),
            _resident((ATTN_W, ATTN_W)),
            *slabs,
            _resident((CHUNK, RET_W)),
            _resident((PAIRS, LANES, LANES)),
            _resident((LANES, LANES)),
        ],
        out_specs=[pl.BlockSpec((tm, PROJ_W), row), *slabs,
                   pl.BlockSpec(state_blk, lambda i: (n_steps - 1 - i, 0, 0, 0))],
        out_shape=[jax.ShapeDtypeStruct((tokens, PROJ_W), BF16),
                   *[jax.ShapeDtypeStruct(w.shape, BF16) for w in ffn_weights],
                   jax.ShapeDtypeStruct((tokens // CHUNK, PAIRS, LANES, LANES), BF16)],
        scratch_shapes=[pltpu.VMEM((PAIRS, LANES, LANES), F32)],
        compiler_params=pltpu.CompilerParams(
            dimension_semantics=("arbitrary",), vmem_limit_bytes=VMEM_LIMIT),
        name="proj",
    )(x2, anw, w_in, qnw, knw, cos, sina, sinb, bd, *ffn_weights, zb, gb, bdm)


def _mix_kernel(sink_ref, p_ref, prev_ref, next_ref, rb_ref, x_ref, wo_ref, bias_ref, dm_ref,
                xif_ref, xib_ref, rnw_ref, zf_ref, gf_ref, bdm_ref, o_ref, y_ref, carry_ref):
    lane = lax.broadcasted_iota(jnp.int32, (CHUNK, LANES), 1).astype(F32).astype(BF16)
    lo = lane < HEAD_DIM
    hi = jnp.logical_not(lo)
    lo_f32 = lax.broadcasted_iota(jnp.int32, (CHUNK, LANES), 1) < HEAD_DIM
    zero = jnp.zeros((), BF16)

    def keep(mask, a):
        reps = a.shape[0] // CHUNK
        m = mask if reps == 1 else jnp.concatenate([mask] * reps, axis=0)
        return jnp.where(m, a, zero)

    tile = pl.program_id(1)

    @pl.when(tile == 0)
    def _():
        carry_ref[...] = jnp.zeros_like(carry_ref)

    n_sub = p_ref.shape[0] // CHUNK
    group = ATTN_HEADS // ATTN_KV_HEADS
    k2_cols = slice(C_K2, C_K2 + KV2_W)
    v2_cols = slice(C_V2, C_V2 + KV2_W)

    def kv_window(j):
        if j == 0:
            inner = slice(0, 2 * CHUNK)
            k3 = jnp.concatenate([prev_ref[:, :KV2_W], p_ref[inner, k2_cols]], axis=0)
            v3 = jnp.concatenate([prev_ref[:, KV2_W:], p_ref[inner, v2_cols]], axis=0)
            bias = bias_ref[jnp.where(tile == 0, 0, 1)]
        elif j == n_sub - 1:
            inner = slice((j - 1) * CHUNK, (j + 1) * CHUNK)
            k3 = jnp.concatenate([p_ref[inner, k2_cols], next_ref[:, :KV2_W]], axis=0)
            v3 = jnp.concatenate([p_ref[inner, v2_cols], next_ref[:, KV2_W:]], axis=0)
            bias = bias_ref[jnp.where(tile == pl.num_programs(1) - 1, 2, 1)]
        else:
            inner = slice((j - 1) * CHUNK, (j + 2) * CHUNK)
            k3 = p_ref[inner, k2_cols]
            v3 = p_ref[inner, v2_cols]
            bias = bias_ref[1]
        return k3, v3, bias

    kv_cache = {}

    def kv_operands(j, g):
        if (j, g) not in kv_cache:
            k3, v3, bias = kv_window(j)
            g_low = slice(g * LANES, (g + 1) * LANES)
            g_high = slice((1 - g) * LANES, (2 - g) * LANES)
            kg = jnp.where(jnp.concatenate([lo] * 3, axis=0), k3[:, g_low], k3[:, g_high])
            kv_cache[(j, g)] = (kg,
                                keep(lo, v3[:, g_low]),
                                keep(hi, v3[:, g_high]),
                                bias)
        return kv_cache[(j, g)]

    def attn_scores(j, pair):
        kg, v_lo, v_hi, bias = kv_operands(j, pair // (PAIRS // ATTN_KV_HEADS))
        qp = p_ref[j * CHUNK:(j + 1) * CHUNK, C_AQ + pair * LANES:C_AQ + (pair + 1) * LANES]
        s_all = _dot_nt(jnp.concatenate([keep(lo, qp), keep(hi, qp)], axis=0), kg)
        return s_all, v_lo, v_hi, bias

    def attn_finish(j, pair, s_all, v_lo, v_hi, bias):
        outs = []
        for i, v_half in enumerate((v_lo, v_hi)):
            s = s_all[i * CHUNK:(i + 1) * CHUNK]
            s = jnp.concatenate([s[:, :CHUNK] + bias[:, :CHUNK], s[:, CHUNK:2 * CHUNK],
                                 s[:, 2 * CHUNK:] + bias[:, 2 * CHUNK:]], axis=1)
            sink = sink_ref[2 * pair + i] * LOG2_E
            m = jnp.maximum(jnp.max(s, axis=-1, keepdims=True), sink)
            e = jnp.exp2(s - m)
            den = jnp.sum(e, axis=-1, keepdims=True) + jnp.exp2(sink - m)
            outs.append(_dot(e.astype(BF16), v_half) * (1.0 / den))
        y_ref[j * CHUNK:(j + 1) * CHUNK, pair * LANES:(pair + 1) * LANES] = (
            outs[0] + outs[1]).astype(BF16)

    def retention_pair(j, p):
        rows = slice(j * CHUNK, (j + 1) * CHUNK)
        sl = slice(p * LANES, (p + 1) * LANES)
        col = lambda c0: slice(c0 + p * LANES, c0 + (p + 1) * LANES)
        qp, kp, vp = p_ref[rows, col(C_RQ)], p_ref[rows, col(C_RK)], p_ref[rows, col(C_RV)]
        s2 = _dot_nt(jnp.concatenate([keep(lo, qp), keep(hi, qp)], axis=0), kp)
        p0 = (s2[:CHUNK] * dm_ref[2 * p]).astype(BF16)
        p1 = (s2[CHUNK:] * dm_ref[2 * p + 1]).astype(BF16)
        qf = (qp.astype(F32) * xif_ref[:, sl]).astype(BF16)
        qb = (qp.astype(F32) * xib_ref[:, sl]).astype(BF16)
        lhs = jnp.concatenate([p0, p1, qf, qb], axis=1)
        state_f = carry_ref[p]
        rhs = jnp.concatenate(
            [keep(lo, vp), keep(hi, vp), state_f.astype(BF16), rb_ref[j, p]], axis=0)
        ret = _dot(lhs, rhs)
        kz = (kp.astype(F32) * zf_ref[:, sl]).astype(BF16)
        carry_ref[p] = gf_ref[p] * state_f + _dot_tn(kz, vp) * bdm_ref[...]
        sq = ret * ret
        ss_lo = jnp.sum(jnp.where(lo_f32, sq, 0.0), axis=-1, keepdims=True)
        ss_hi = jnp.sum(jnp.where(lo_f32, 0.0, sq), axis=-1, keepdims=True)
        ms = jnp.where(lo_f32, ss_lo, ss_hi) * (1.0 / HEAD_DIM)
        ret_n = ret * lax.rsqrt(ms + EPS) * rnw_ref[:, sl]
        gate = p_ref[rows, col(C_RG)].astype(F32)
        y_ref[rows, ATTN_W + p * LANES:ATTN_W + (p + 1) * LANES] = (
            gate * (1.0 / (1.0 + jnp.exp(-gate))) * ret_n).astype(BF16)

    units = [(j, pair) for j in range(n_sub) for pair in range(ATTN_HEADS // 2)]
    ret_items = [(j, p) for j in range(n_sub) for p in range(PAIRS)]
    assert len(units) * RET_PAIRS_PER_UNIT >= len(ret_items)
    pending = attn_scores(*units[0])
    for u, (j, g) in enumerate(units):
        following = attn_scores(*units[u + 1]) if u + 1 < len(units) else None
        for item in ret_items[u * RET_PAIRS_PER_UNIT:(u + 1) * RET_PAIRS_PER_UNIT]:
            retention_pair(*item)
        attn_finish(j, g, *pending)
        pending = following
    o_ref[...] = x_ref[...] + _dot(y_ref[...], wo_ref[...])


def _mix_call(sink, proj3, rb, x3, w_out, bias, dmat, xif, xib, rnw, zf, gf, bdm):
    batch, seq, d_model = x3.shape
    tm = MIX_TM
    n_sub = tm // CHUNK
    nt = seq // tm
    nc = seq // CHUNK
    kv_cols = C_K2 // (2 * KV2_W)
    assert C_K2 % (2 * KV2_W) == 0 and C_V2 == C_K2 + KV2_W
    kv_prev = pl.BlockSpec(
        (None, CHUNK, 2 * KV2_W), lambda b, t: (b, jnp.maximum(t * n_sub - 1, 0), kv_cols))
    kv_next = pl.BlockSpec(
        (None, CHUNK, 2 * KV2_W), lambda b, t: (b, jnp.minimum((t + 1) * n_sub, nc - 1), kv_cols))
    st = pl.BlockSpec((None, n_sub, PAIRS, LANES, LANES), lambda b, t: (b, t, 0, 0, 0))
    return pl.pallas_call(
        _mix_kernel,
        grid=(batch, nt),
        in_specs=[
            pl.BlockSpec(memory_space=pltpu.SMEM),
            pl.BlockSpec((None, tm, PROJ_W), lambda b, t: (b, t, 0)),
            kv_prev, kv_next,
            st,
            pl.BlockSpec((None, tm, d_model), lambda b, t: (b, t, 0)),
            _resident((d_model, d_model)),
            _resident((3, CHUNK, 3 * CHUNK)),
            _resident((RET_HEADS, CHUNK, CHUNK)),
            _resident((CHUNK, RET_W)),
            _resident((CHUNK, RET_W)),
            _resident((1, RET_W)),
            _resident((CHUNK, RET_W)),
            _resident((PAIRS, LANES, LANES)),
            _resident((LANES, LANES)),
        ],
        out_specs=pl.BlockSpec((None, tm, d_model), lambda b, t: (b, t, 0)),
        out_shape=jax.ShapeDtypeStruct((batch, seq, d_model), F32),
        scratch_shapes=[pltpu.VMEM((tm, d_model), BF16), pltpu.VMEM((PAIRS, LANES, LANES), F32)],
        compiler_params=pltpu.CompilerParams(
            dimension_semantics=("arbitrary", "arbitrary"), vmem_limit_bytes=VMEM_LIMIT),
        name="mix",
    )(sink, proj3, proj3, proj3, rb, x3, w_out, bias, dmat, xif, xib, rnw, zf, gf, bdm)


def _ffn_kernel(h_ref, fnw_ref, wg_ref, wu_ref, wd_ref, o_ref, a_ref):
    h = h_ref[...]
    ms = jnp.mean(h * h, axis=-1, keepdims=True)
    m = (h * lax.rsqrt(ms + EPS) * fnw_ref[...]).astype(BF16)
    d_ff = wg_ref.shape[1]
    for c0 in range(0, d_ff, FFN_CK):
        c1 = min(c0 + FFN_CK, d_ff)
        g = _dot(m, wg_ref[:, c0:c1])
        u = _dot(m, wu_ref[:, c0:c1])
        a_ref[:, c0:c1] = (g * (1.0 / (1.0 + jnp.exp(-g))) * u).astype(BF16)
    o_ref[...] = h + _dot(a_ref[...], wd_ref[...])


def _ffn_call(h2, fnw, wg, wu, wd):
    tokens, d_model = h2.shape
    d_ff = wg.shape[1]
    tm = FFN_TM
    row = lambda i: (i, 0)
    return pl.pallas_call(
        _ffn_kernel,
        grid=(tokens // tm,),
        in_specs=[
            pl.BlockSpec((tm, d_model), row),
            _resident((1, d_model)),
            _resident((d_model, d_ff)),
            _resident((d_model, d_ff)),
            _resident((d_ff, d_model)),
        ],
        out_specs=pl.BlockSpec((tm, d_model), row),
        out_shape=jax.ShapeDtypeStruct((tokens, d_model), F32),
        scratch_shapes=[pltpu.VMEM((tm, d_ff), BF16)],
        compiler_params=pltpu.CompilerParams(
            dimension_semantics=("arbitrary",), vmem_limit_bytes=VMEM_LIMIT),
        name="ffn",
    )(h2, fnw, wg, wu, wd)


def _rope_tables(seq):
    inv_freq = ROPE_THETA ** (-np.arange(0, HEAD_DIM, 2, dtype=np.float64) / HEAD_DIM)
    ang = np.arange(seq, dtype=np.float64)[:, None] * inv_freq[None, :]
    cos, sin = np.cos(ang), np.sin(ang)
    zeros = np.zeros_like(sin)
    reps = LANES // HEAD_DIM
    cos_t = np.tile(np.concatenate([cos, cos], -1), (1, reps))
    sina_t = np.tile(np.concatenate([-sin, zeros], -1), (1, reps))
    sinb_t = np.tile(np.concatenate([zeros, sin], -1), (1, reps))
    return tuple(jnp.asarray(t.astype(np.float32)) for t in (cos_t, sina_t, sinb_t))


def _attn_bias():
    i = np.arange(CHUNK)[:, None]
    j = np.arange(3 * CHUNK)[None, :]
    band = np.abs(i + CHUNK - j) <= CHUNK
    first = band & (j >= CHUNK)
    last = band & (j < 2 * CHUNK)
    tab = np.stack([first, band, last])
    return jnp.asarray(np.where(tab, 0.0, NEG_INF).astype(np.float32))


def _retention_tables(log_f, log_b):
    idx = jnp.arange(CHUNK, dtype=F32)
    diff = idx[:, None] - idx[None, :]
    lf, lb = log_f[:, None, None], log_b[:, None, None]
    dmat = jnp.where(diff[None] >= 0,
                     jnp.exp(lf * jnp.maximum(diff, 0.0)[None]),
                     jnp.exp(lb * jnp.maximum(-diff, 0.0)[None]))
    pos = np.arange(CHUNK, dtype=np.float32)[:, None]
    lane_f = jnp.repeat(log_f, HEAD_DIM)[None, :]
    lane_b = jnp.repeat(log_b, HEAD_DIM)[None, :]
    xif = jnp.exp(lane_f * (pos + 1.0))
    xib = jnp.exp(lane_b * (CHUNK - pos))
    zf = jnp.exp(lane_f * (CHUNK - 1.0 - pos))
    zb = jnp.exp(lane_b * pos)
    per_row = lambda lg: jnp.broadcast_to(
        jnp.exp(jnp.repeat(lg.reshape(PAIRS, 2), HEAD_DIM, axis=1) * CHUNK)[:, :, None],
        (PAIRS, LANES, LANES))
    return dmat, xif, xib, zf, zb, per_row(log_f), per_row(log_b)


def _block_diag_ones(width, dtype):
    r = np.arange(width) // HEAD_DIM
    return jnp.asarray((r[:, None] == r[None, :]).astype(np.float32), dtype=dtype)


def kernel(x, attn_norm_w, w_in, q_norm_w, k_norm_w, attn_sink, ret_log_decay_fwd,
           ret_log_decay_bwd, ret_norm_w, w_out, ffn_norm_w, w_gate, w_up, w_down):
    batch, seq, d_model = x.shape
    depth = w_in.shape[0]
    assert seq % PROJ_TM == 0 and seq % MIX_TM == 0 and (batch * seq) % FFN_TM == 0
    assert MIX_TM // CHUNK >= 2

    cos_t, sina_t, sinb_t = _rope_tables(seq)
    bias = _attn_bias()
    bd = _block_diag_ones(RET_W, BF16)
    bdm = _block_diag_ones(LANES, F32)

    h = x
    for l in range(depth):
        qnw = jnp.tile(q_norm_w[l], ATTN_HEADS)[None, :]
        knw = jnp.tile(k_norm_w[l], ATTN_KV_HEADS)[None, :]
        log_f = -jnp.abs(ret_log_decay_fwd[l].astype(F32))
        log_b = -jnp.abs(ret_log_decay_bwd[l].astype(F32))
        dmat, xif, xib, zf, zb, gf, gb = _retention_tables(log_f, log_b)

        proj, wg_bf, wu_bf, wd_bf, wo_bf, rb = _proj_call(
            h.reshape(batch * seq, d_model), attn_norm_w[l][None, :], w_in[l], qnw, knw,
            cos_t, sina_t, sinb_t, bd, seq, (w_gate[l], w_up[l], w_down[l], w_out[l]), zb, gb, bdm)
        proj3 = proj.reshape(batch, seq, PROJ_W)
        rb = rb.reshape(batch, seq // CHUNK, PAIRS, LANES, LANES)
        h = _mix_call(attn_sink[l].astype(F32), proj3, rb, h, wo_bf,
                      bias, dmat, xif, xib, ret_norm_w[l][None, :], zf, gf, bdm)
        h = _ffn_call(h.reshape(batch * seq, d_model), ffn_norm_w[l][None, :],
                      wg_bf, wu_bf, wd_bf).reshape(batch, seq, d_model)
    return h
```

```python
import functools

import jax
import jax.numpy as jnp
import numpy as np
from jax import lax
from jax.experimental import pallas as pl
from jax.experimental.pallas import tpu as pltpu

HEAD_DIM = 64
ATTN_HEADS = 8
ATTN_KV_HEADS = 2
RET_HEADS = 8
CHUNK = 128
ROPE_THETA = 10000.0
EPS = 1e-6
NEG_INF = -1e30
LOG2_E = 1.4426950408889634

LANES = 128
BF16_SUBLANES = 16
PAIRS = RET_HEADS // 2
ATTN_W = ATTN_HEADS * HEAD_DIM
RET_W = RET_HEADS * HEAD_DIM
KV2_W = 2 * ATTN_KV_HEADS * HEAD_DIM
V4_W = 2 * KV2_W
KV_W = KV2_W + V4_W

C_K2 = 0
C_V4 = C_K2 + KV2_W
C_AQ = C_V4 + V4_W
C_RK = C_AQ + ATTN_W
C_RV = C_RK + RET_W
C_RQ = C_RV + RET_W
C_RG = C_RQ + RET_W
PROJ_W = C_RG + RET_W

PROJ_TM = 1024
MIX_TM = 1024
RET_PAIRS_PER_UNIT = 1
FFN_TM = 1024
FFN_CK = 256
VMEM_LIMIT = 56 * 1024 * 1024

BF16 = jnp.bfloat16
F32 = jnp.float32


def _dot(a, b):
    return jnp.dot(a, b, preferred_element_type=F32)


def _dot_nt(a, b):
    return lax.dot_general(a, b, (((1,), (1,)), ((), ())), preferred_element_type=F32)


def _dot_tn(a, b):
    return lax.dot_general(a, b, (((0,), (0,)), ((), ())), preferred_element_type=F32)


def _resident(shape):
    zeros = (0,) * len(shape)
    return pl.BlockSpec(shape, lambda *_: zeros, pipeline_mode=pl.Buffered(1))


def _proj_kernel(tiles_per_seq, x_ref, anw_ref, w_ref, qnw_ref, knw_ref, cos_ref, sina_ref,
                 sinb_ref, bd_ref, wg_ref, wu_ref, wd_ref, wo_ref, zb_ref, gb_ref, bdm_ref,
                 o_ref, wg_bf_ref, wu_bf_ref, wd_bf_ref, wo_bf_ref, rb_ref, carry_ref):
    @pl.when(pl.program_id(0) % tiles_per_seq == 0)
    def _():
        carry_ref[...] = jnp.zeros_like(carry_ref)

    x = x_ref[...]
    ms = jnp.mean(x * x, axis=-1, keepdims=True)
    n = x * lax.rsqrt(ms + EPS) * anw_ref[...]
    cos = cos_ref[...]
    sina = sina_ref[...]
    sinb = sinb_ref[...]

    def proj(c0, width):
        return _dot(n, w_ref[:, c0:c0 + width])

    def head_norm(y, w):
        width = y.shape[-1]
        ss = _dot((y * y).astype(BF16), bd_ref[:width, :width])
        return y * lax.rsqrt(ss * (1.0 / HEAD_DIM) + EPS) * w

    def rope(ys):
        return ys * cos + pltpu.roll(ys, LANES - 32, 1) * sina + pltpu.roll(ys, 32, 1) * sinb

    def rope_store(y, c0, scale):
        for s in range(y.shape[-1] // LANES):
            r = rope(y[:, s * LANES:(s + 1) * LANES])
            if scale != 1.0:
                r = r * scale
            o_ref[:, c0 + s * LANES:c0 + (s + 1) * LANES] = r.astype(BF16)

    def plain_store(y, c0):
        o_ref[:, c0:c0 + y.shape[-1]] = y.astype(BF16)

    def kv_finish(y):
        first = lax.broadcasted_iota(jnp.int32, (y.shape[0], LANES), 1) < HEAD_DIM
        k = rope(head_norm(y[:, :LANES], knw_ref[...]))
        k_sw = pltpu.roll(k, HEAD_DIM, 1)
        v = y[:, LANES:]
        v_sw = pltpu.roll(v, HEAD_DIM, 1)
        slabs = [jnp.where(first, k, k_sw), jnp.where(first, k_sw, k),
                 jnp.where(first, v, 0.0), jnp.where(first, 0.0, v_sw),
                 jnp.where(first, v_sw, 0.0), jnp.where(first, 0.0, v)]
        for s, slab in enumerate(slabs):
            o_ref[:, C_K2 + s * LANES:C_K2 + (s + 1) * LANES] = slab.astype(BF16)

    scale = HEAD_DIM ** -0.5
    kvw = ATTN_KV_HEADS * HEAD_DIM
    in_kv = ATTN_W
    in_rq = in_kv + 2 * kvw
    def backward_states():
        bdm = bdm_ref[...]
        for p in range(PAIRS):
            sl = slice(p * LANES, (p + 1) * LANES)
            carry = carry_ref[p]
            for c in reversed(range(o_ref.shape[0] // CHUNK)):
                rows = slice(c * CHUNK, (c + 1) * CHUNK)
                k = o_ref[rows, C_RK + p * LANES:C_RK + (p + 1) * LANES].astype(F32) * zb_ref[:, sl]
                kv = _dot_tn(k.astype(BF16), o_ref[rows, C_RV + p * LANES:C_RV + (p + 1) * LANES])
                rb_ref[c, p] = carry.astype(BF16)
                carry = gb_ref[p] * carry + kv * bdm
            carry_ref[p] = carry

    def rq_finish(y):
        rope_store(y, C_RQ, 1.0)
        backward_states()

    stages = [
        (0, ATTN_W, lambda y: rope_store(head_norm(y, qnw_ref[...]), C_AQ, scale * LOG2_E)),
        (in_rq + RET_W, RET_W, lambda y: rope_store(y, C_RK, scale)),
        (in_rq + 2 * RET_W, RET_W, lambda y: plain_store(y, C_RV)),
        (in_kv, 2 * kvw, kv_finish),
        (in_rq, RET_W, rq_finish),
        (in_rq + 3 * RET_W, RET_W, lambda y: plain_store(y, C_RG)),
    ]
    pending = proj(stages[0][0], stages[0][1])
    wg_bf_ref[...] = wg_ref[...].astype(BF16)
    wu_bf_ref[...] = wu_ref[...].astype(BF16)
    wd_bf_ref[...] = wd_ref[...].astype(BF16)
    wo_bf_ref[...] = wo_ref[...].astype(BF16)
    for i, (_, _, finish) in enumerate(stages):
        following = proj(stages[i + 1][0], stages[i + 1][1]) if i + 1 < len(stages) else None
        finish(pending)
        pending = following


def _slab_spec(weight, n_steps):
    rows, cols = weight.shape
    n_slabs = n_steps
    while rows % n_slabs or (rows // n_slabs) % BF16_SUBLANES or n_steps % n_slabs:
        n_slabs -= 1
    hold = n_steps // n_slabs
    return pl.BlockSpec((rows // n_slabs, cols), lambda i: (i // hold, 0))


def _proj_call(x2, anw, w_in, qnw, knw, cos, sina, sinb, bd, seq, ffn_weights, zb, gb, bdm):
    tokens, d_model = x2.shape
    tm = PROJ_TM
    n_steps = tokens // tm
    tiles_per_seq = seq // tm
    n_sub = tm // CHUNK
    row = lambda i: (n_steps - 1 - i, 0)
    pos = lambda i: ((n_steps - 1 - i) % tiles_per_seq, 0)
    slabs = [_slab_spec(w, n_steps) for w in ffn_weights]
    state_blk = (n_sub, PAIRS, LANES, LANES)
    return pl.pallas_call(
        functools.partial(_proj_kernel, tiles_per_seq),
        grid=(n_steps,),
        in_specs=[
            pl.BlockSpec((tm, d_model), row),
            _resident((1, d_model)),
            _resident(w_in.shape),
            _resident((1, ATTN_W)),
            _resident((1, LANES)),
            pl.BlockSpec((tm, LANES), pos),
            pl.BlockSpec((tm, LANES), pos),
            pl.BlockSpec((tm, LANES), pos),
            _resident((ATTN_W, ATTN_W)),
            *slabs,
            _resident((CHUNK, RET_W)),
            _resident((PAIRS, LANES, LANES)),
            _resident((LANES, LANES)),
        ],
        out_specs=[pl.BlockSpec((tm, PROJ_W), row), *slabs,
                   pl.BlockSpec(state_blk, lambda i: (n_steps - 1 - i, 0, 0, 0))],
        out_shape=[jax.ShapeDtypeStruct((tokens, PROJ_W), BF16),
                   *[jax.ShapeDtypeStruct(w.shape, BF16) for w in ffn_weights],
                   jax.ShapeDtypeStruct((tokens // CHUNK, PAIRS, LANES, LANES), BF16)],
        scratch_shapes=[pltpu.VMEM((PAIRS, LANES, LANES), F32)],
        compiler_params=pltpu.CompilerParams(
            dimension_semantics=("arbitrary",), vmem_limit_bytes=VMEM_LIMIT),
        name="proj",
    )(x2, anw, w_in, qnw, knw, cos, sina, sinb, bd, *ffn_weights, zb, gb, bdm)


def _mix_kernel(sink_ref, p_ref, prev_ref, next_ref, rb_ref, x_ref, wo_ref, bias_ref, dm_ref,
                xif_ref, xib_ref, rnw_ref, zf_ref, gf_ref, bdm_ref, o_ref, y_ref, carry_ref):
    lane = lax.broadcasted_iota(jnp.int32, (CHUNK, LANES), 1).astype(F32).astype(BF16)
    lo = lane < HEAD_DIM
    hi = jnp.logical_not(lo)
    lo_f32 = lax.broadcasted_iota(jnp.int32, (CHUNK, LANES), 1) < HEAD_DIM
    zero = jnp.zeros((), BF16)

    def keep(mask, a):
        reps = a.shape[0] // CHUNK
        m = mask if reps == 1 else jnp.concatenate([mask] * reps, axis=0)
        return jnp.where(m, a, zero)

    tile = pl.program_id(1)

    @pl.when(tile == 0)
    def _():
        carry_ref[...] = jnp.zeros_like(carry_ref)

    n_sub = p_ref.shape[0] // CHUNK
    group = ATTN_HEADS // ATTN_KV_HEADS
    def kv_operands(j, g):
        def window(c0):
            cols = slice(c0, c0 + LANES)
            if j == 0:
                return jnp.concatenate([prev_ref[:, cols], p_ref[0:2 * CHUNK, cols]], axis=0)
            if j == n_sub - 1:
                return jnp.concatenate(
                    [p_ref[(j - 1) * CHUNK:(j + 1) * CHUNK, cols], next_ref[:, cols]], axis=0)
            return p_ref[(j - 1) * CHUNK:(j + 2) * CHUNK, cols]

        if j == 0:
            bias = bias_ref[jnp.where(tile == 0, 0, 1)]
        elif j == n_sub - 1:
            bias = bias_ref[jnp.where(tile == pl.num_programs(1) - 1, 2, 1)]
        else:
            bias = bias_ref[1]
        return (window(C_K2 + g * LANES), window(C_V4 + 2 * g * LANES),
                window(C_V4 + (2 * g + 1) * LANES), bias)

    def attn_scores(j, pair):
        kg, v_lo, v_hi, bias = kv_operands(j, pair // (PAIRS // ATTN_KV_HEADS))
        qp = p_ref[j * CHUNK:(j + 1) * CHUNK, C_AQ + pair * LANES:C_AQ + (pair + 1) * LANES]
        s_all = _dot_nt(jnp.concatenate([keep(lo, qp), keep(hi, qp)], axis=0), kg)
        return s_all, v_lo, v_hi, bias

    def attn_finish(j, pair, s_all, v_lo, v_hi, bias):
        outs = []
        for i, v_half in enumerate((v_lo, v_hi)):
            s = s_all[i * CHUNK:(i + 1) * CHUNK]
            s = jnp.concatenate([s[:, :CHUNK] + bias[:, :CHUNK], s[:, CHUNK:2 * CHUNK],
                                 s[:, 2 * CHUNK:] + bias[:, 2 * CHUNK:]], axis=1)
            sink = sink_ref[2 * pair + i] * LOG2_E
            m = jnp.maximum(jnp.max(s, axis=-1, keepdims=True), sink)
            e = jnp.exp2(s - m)
            den = jnp.sum(e, axis=-1, keepdims=True) + jnp.exp2(sink - m)
            outs.append(_dot(e.astype(BF16), v_half) * (1.0 / den))
        y_ref[j * CHUNK:(j + 1) * CHUNK, pair * LANES:(pair + 1) * LANES] = (
            outs[0] + outs[1]).astype(BF16)

    def retention_pair(j, p):
        rows = slice(j * CHUNK, (j + 1) * CHUNK)
        sl = slice(p * LANES, (p + 1) * LANES)
        col = lambda c0: slice(c0 + p * LANES, c0 + (p + 1) * LANES)
        qp, kp, vp = p_ref[rows, col(C_RQ)], p_ref[rows, col(C_RK)], p_ref[rows, col(C_RV)]
        s2 = _dot_nt(jnp.concatenate([keep(lo, qp), keep(hi, qp)], axis=0), kp)
        p0 = (s2[:CHUNK] * dm_ref[2 * p]).astype(BF16)
        p1 = (s2[CHUNK:] * dm_ref[2 * p + 1]).astype(BF16)
        qf = (qp.astype(F32) * xif_ref[:, sl]).astype(BF16)
        qb = (qp.astype(F32) * xib_ref[:, sl]).astype(BF16)
        lhs = jnp.concatenate([p0, p1, qf, qb], axis=1)
        state_f = carry_ref[p]
        rhs = jnp.concatenate(
            [keep(lo, vp), keep(hi, vp), state_f.astype(BF16), rb_ref[j, p]], axis=0)
        ret = _dot(lhs, rhs)
        kz = (kp.astype(F32) * zf_ref[:, sl]).astype(BF16)
        carry_ref[p] = gf_ref[p] * state_f + _dot_tn(kz, vp) * bdm_ref[...]
        sq = ret * ret
        ss_lo = jnp.sum(jnp.where(lo_f32, sq, 0.0), axis=-1, keepdims=True)
        ss_hi = jnp.sum(jnp.where(lo_f32, 0.0, sq), axis=-1, keepdims=True)
        ms = jnp.where(lo_f32, ss_lo, ss_hi) * (1.0 / HEAD_DIM)
        ret_n = ret * lax.rsqrt(ms + EPS) * rnw_ref[:, sl]
        gate = p_ref[rows, col(C_RG)].astype(F32)
        y_ref[rows, ATTN_W + p * LANES:ATTN_W + (p + 1) * LANES] = (
            gate * (1.0 / (1.0 + jnp.exp(-gate))) * ret_n).astype(BF16)

    units = [(j, pair) for j in range(n_sub) for pair in range(ATTN_HEADS // 2)]
    ret_items = [(j, p) for j in range(n_sub) for p in range(PAIRS)]
    assert len(units) * RET_PAIRS_PER_UNIT >= len(ret_items)
    pending = attn_scores(*units[0])
    for u, (j, g) in enumerate(units):
        following = attn_scores(*units[u + 1]) if u + 1 < len(units) else None
        for item in ret_items[u * RET_PAIRS_PER_UNIT:(u + 1) * RET_PAIRS_PER_UNIT]:
            retention_pair(*item)
        attn_finish(j, g, *pending)
        pending = following
    o_ref[...] = x_ref[...] + _dot(y_ref[...], wo_ref[...])


def _mix_call(sink, proj3, rb, x3, w_out, bias, dmat, xif, xib, rnw, zf, gf, bdm):
    batch, seq, d_model = x3.shape
    tm = MIX_TM
    n_sub = tm // CHUNK
    nt = seq // tm
    nc = seq // CHUNK
    assert C_K2 == 0 and C_V4 == KV2_W
    kv_prev = pl.BlockSpec(
        (None, CHUNK, KV_W), lambda b, t: (b, jnp.maximum(t * n_sub - 1, 0), 0))
    kv_next = pl.BlockSpec(
        (None, CHUNK, KV_W), lambda b, t: (b, jnp.minimum((t + 1) * n_sub, nc - 1), 0))
    st = pl.BlockSpec((None, n_sub, PAIRS, LANES, LANES), lambda b, t: (b, t, 0, 0, 0))
    return pl.pallas_call(
        _mix_kernel,
        grid=(batch, nt),
        in_specs=[
            pl.BlockSpec(memory_space=pltpu.SMEM),
            pl.BlockSpec((None, tm, PROJ_W), lambda b, t: (b, t, 0)),
            kv_prev, kv_next,
            st,
            pl.BlockSpec((None, tm, d_model), lambda b, t: (b, t, 0)),
            _resident((d_model, d_model)),
            _resident((3, CHUNK, 3 * CHUNK)),
            _resident((RET_HEADS, CHUNK, CHUNK)),
            _resident((CHUNK, RET_W)),
            _resident((CHUNK, RET_W)),
            _resident((1, RET_W)),
            _resident((CHUNK, RET_W)),
            _resident((PAIRS, LANES, LANES)),
            _resident((LANES, LANES)),
        ],
        out_specs=pl.BlockSpec((None, tm, d_model), lambda b, t: (b, t, 0)),
        out_shape=jax.ShapeDtypeStruct((batch, seq, d_model), F32),
        scratch_shapes=[pltpu.VMEM((tm, d_model), BF16), pltpu.VMEM((PAIRS, LANES, LANES), F32)],
        compiler_params=pltpu.CompilerParams(
            dimension_semantics=("arbitrary", "arbitrary"), vmem_limit_bytes=VMEM_LIMIT),
        name="mix",
    )(sink, proj3, proj3, proj3, rb, x3, w_out, bias, dmat, xif, xib, rnw, zf, gf, bdm)


def _ffn_kernel(h_ref, fnw_ref, wg_ref, wu_ref, wd_ref, o_ref, a_ref):
    h = h_ref[...]
    ms = jnp.mean(h * h, axis=-1, keepdims=True)
    m = (h * lax.rsqrt(ms + EPS) * fnw_ref[...]).astype(BF16)
    d_ff = wg_ref.shape[1]
    for c0 in range(0, d_ff, FFN_CK):
        c1 = min(c0 + FFN_CK, d_ff)
        g = _dot(m, wg_ref[:, c0:c1])
        u = _dot(m, wu_ref[:, c0:c1])
        a_ref[:, c0:c1] = (g * (1.0 / (1.0 + jnp.exp(-g))) * u).astype(BF16)
    o_ref[...] = h + _dot(a_ref[...], wd_ref[...])


def _ffn_call(h2, fnw, wg, wu, wd):
    tokens, d_model = h2.shape
    d_ff = wg.shape[1]
    tm = FFN_TM
    row = lambda i: (i, 0)
    return pl.pallas_call(
        _ffn_kernel,
        grid=(tokens // tm,),
        in_specs=[
            pl.BlockSpec((tm, d_model), row),
            _resident((1, d_model)),
            _resident((d_model, d_ff)),
            _resident((d_model, d_ff)),
            _resident((d_ff, d_model)),
        ],
        out_specs=pl.BlockSpec((tm, d_model), row),
        out_shape=jax.ShapeDtypeStruct((tokens, d_model), F32),
        scratch_shapes=[pltpu.VMEM((tm, d_ff), BF16)],
        compiler_params=pltpu.CompilerParams(
            dimension_semantics=("arbitrary",), vmem_limit_bytes=VMEM_LIMIT),
        name="ffn",
    )(h2, fnw, wg, wu, wd)


def _rope_tables(seq):
    inv_freq = ROPE_THETA ** (-np.arange(0, HEAD_DIM, 2, dtype=np.float64) / HEAD_DIM)
    ang = np.arange(seq, dtype=np.float64)[:, None] * inv_freq[None, :]
    cos, sin = np.cos(ang), np.sin(ang)
    zeros = np.zeros_like(sin)
    reps = LANES // HEAD_DIM
    cos_t = np.tile(np.concatenate([cos, cos], -1), (1, reps))
    sina_t = np.tile(np.concatenate([-sin, zeros], -1), (1, reps))
    sinb_t = np.tile(np.concatenate([zeros, sin], -1), (1, reps))
    return tuple(jnp.asarray(t.astype(np.float32)) for t in (cos_t, sina_t, sinb_t))


def _attn_bias():
    i = np.arange(CHUNK)[:, None]
    j = np.arange(3 * CHUNK)[None, :]
    band = np.abs(i + CHUNK - j) <= CHUNK
    first = band & (j >= CHUNK)
    last = band & (j < 2 * CHUNK)
    tab = np.stack([first, band, last])
    return jnp.asarray(np.where(tab, 0.0, NEG_INF).astype(np.float32))


def _retention_tables(log_f, log_b):
    idx = jnp.arange(CHUNK, dtype=F32)
    diff = idx[:, None] - idx[None, :]
    lf, lb = log_f[:, None, None], log_b[:, None, None]
    dmat = jnp.where(diff[None] >= 0,
                     jnp.exp(lf * jnp.maximum(diff, 0.0)[None]),
                     jnp.exp(lb * jnp.maximum(-diff, 0.0)[None]))
    pos = np.arange(CHUNK, dtype=np.float32)[:, None]
    lane_f = jnp.repeat(log_f, HEAD_DIM)[None, :]
    lane_b = jnp.repeat(log_b, HEAD_DIM)[None, :]
    xif = jnp.exp(lane_f * (pos + 1.0))
    xib = jnp.exp(lane_b * (CHUNK - pos))
    zf = jnp.exp(lane_f * (CHUNK - 1.0 - pos))
    zb = jnp.exp(lane_b * pos)
    per_row = lambda lg: jnp.broadcast_to(
        jnp.exp(jnp.repeat(lg.reshape(PAIRS, 2), HEAD_DIM, axis=1) * CHUNK)[:, :, None],
        (PAIRS, LANES, LANES))
    return dmat, xif, xib, zf, zb, per_row(log_f), per_row(log_b)


def _block_diag_ones(width, dtype):
    r = np.arange(width) // HEAD_DIM
    return jnp.asarray((r[:, None] == r[None, :]).astype(np.float32), dtype=dtype)


def kernel(x, attn_norm_w, w_in, q_norm_w, k_norm_w, attn_sink, ret_log_decay_fwd,
           ret_log_decay_bwd, ret_norm_w, w_out, ffn_norm_w, w_gate, w_up, w_down):
    batch, seq, d_model = x.shape
    depth = w_in.shape[0]
    assert seq % PROJ_TM == 0 and seq % MIX_TM == 0 and (batch * seq) % FFN_TM == 0
    assert MIX_TM // CHUNK >= 2

    cos_t, sina_t, sinb_t = _rope_tables(seq)
    bias = _attn_bias()
    bd = _block_diag_ones(RET_W, BF16)
    bdm = _block_diag_ones(LANES, F32)

    h = x
    for l in range(depth):
        qnw = jnp.tile(q_norm_w[l], ATTN_HEADS)[None, :]
        knw = jnp.tile(k_norm_w[l], ATTN_KV_HEADS)[None, :]
        log_f = -jnp.abs(ret_log_decay_fwd[l].astype(F32))
        log_b = -jnp.abs(ret_log_decay_bwd[l].astype(F32))
        dmat, xif, xib, zf, zb, gf, gb = _retention_tables(log_f, log_b)

        proj, wg_bf, wu_bf, wd_bf, wo_bf, rb = _proj_call(
            h.reshape(batch * seq, d_model), attn_norm_w[l][None, :], w_in[l], qnw, knw,
            cos_t, sina_t, sinb_t, bd, seq, (w_gate[l], w_up[l], w_down[l], w_out[l]), zb, gb, bdm)
        proj3 = proj.reshape(batch, seq, PROJ_W)
        rb = rb.reshape(batch, seq // CHUNK, PAIRS, LANES, LANES)
        h = _mix_call(attn_sink[l].astype(F32), proj3, rb, h, wo_bf,
                      bias, dmat, xif, xib, ret_norm_w[l][None, :], zf, gf, bdm)
        h = _ffn_call(h.reshape(batch * seq, d_model), ffn_norm_w[l][None, :],
                      wg_bf, wu_bf, wd_bf).reshape(batch, seq, d_model)
    return h
```

```python
import functools

import jax
import jax.numpy as jnp
import numpy as np
from jax import lax
from jax.experimental import pallas as pl
from jax.experimental.pallas import tpu as pltpu

HEAD_DIM = 64
ATTN_HEADS = 8
ATTN_KV_HEADS = 2
RET_HEADS = 8
CHUNK = 128
ROPE_THETA = 10000.0
EPS = 1e-6
NEG_INF = -1e30
LOG2_E = 1.4426950408889634

LANES = 128
BF16_SUBLANES = 16
PAIRS = RET_HEADS // 2
ATTN_W = ATTN_HEADS * HEAD_DIM
RET_W = RET_HEADS * HEAD_DIM
KV2_W = 2 * ATTN_KV_HEADS * HEAD_DIM
V4_W = 2 * KV2_W
KV_W = KV2_W + V4_W

C_K2 = 0
C_V4 = C_K2 + KV2_W
C_AQ = C_V4 + V4_W
C_RK = C_AQ + ATTN_W
C_RV = C_RK + RET_W
C_RQ = C_RV + RET_W
C_RG = C_RQ + RET_W
PROJ_W = C_RG + RET_W

PROJ_TM = 1024
MIX_TM = 1024
RET_PAIRS_PER_UNIT = 1
FFN_TM = 1024
FFN_CK = 256
VMEM_LIMIT = 56 * 1024 * 1024

BF16 = jnp.bfloat16
F32 = jnp.float32


def _dot(a, b):
    return jnp.dot(a, b, preferred_element_type=F32)


def _dot_nt(a, b):
    return lax.dot_general(a, b, (((1,), (1,)), ((), ())), preferred_element_type=F32)


def _dot_tn(a, b):
    return lax.dot_general(a, b, (((0,), (0,)), ((), ())), preferred_element_type=F32)


def _resident(shape):
    zeros = (0,) * len(shape)
    return pl.BlockSpec(shape, lambda *_: zeros, pipeline_mode=pl.Buffered(1))


def _proj_kernel(tiles_per_seq, x_ref, anw_ref, w_ref, qnw_ref, knw_ref, cos_ref, sina_ref,
                 sinb_ref, bd_ref, wg_ref, wu_ref, wd_ref, wo_ref, zb_ref, gb_ref, bdm_ref,
                 o_ref, wg_bf_ref, wu_bf_ref, wd_bf_ref, wo_bf_ref, rb_ref, carry_ref):
    @pl.when(pl.program_id(0) % tiles_per_seq == 0)
    def _():
        carry_ref[...] = jnp.zeros_like(carry_ref)

    x = x_ref[...]
    ms = jnp.mean(x * x, axis=-1, keepdims=True)
    n = x * lax.rsqrt(ms + EPS) * anw_ref[...]
    cos = cos_ref[...]
    sina = sina_ref[...]
    sinb = sinb_ref[...]

    def proj(c0, width):
        return _dot(n, w_ref[:, c0:c0 + width])

    def head_norm(y, w):
        width = y.shape[-1]
        ss = _dot((y * y).astype(BF16), bd_ref[:width, :width])
        return y * lax.rsqrt(ss * (1.0 / HEAD_DIM) + EPS) * w

    def rope(ys):
        return ys * cos + pltpu.roll(ys, LANES - 32, 1) * sina + pltpu.roll(ys, 32, 1) * sinb

    def rope_store(y, c0, scale):
        for s in range(y.shape[-1] // LANES):
            r = rope(y[:, s * LANES:(s + 1) * LANES])
            if scale != 1.0:
                r = r * scale
            o_ref[:, c0 + s * LANES:c0 + (s + 1) * LANES] = r.astype(BF16)

    def plain_store(y, c0):
        o_ref[:, c0:c0 + y.shape[-1]] = y.astype(BF16)

    def kv_finish(y):
        first = lax.broadcasted_iota(jnp.int32, (y.shape[0], LANES), 1) < HEAD_DIM
        k = rope(head_norm(y[:, :LANES], knw_ref[...]))
        k_sw = pltpu.roll(k, HEAD_DIM, 1)
        v = y[:, LANES:]
        v_sw = pltpu.roll(v, HEAD_DIM, 1)
        slabs = [jnp.where(first, k, k_sw), jnp.where(first, k_sw, k),
                 jnp.where(first, v, 0.0), jnp.where(first, 0.0, v_sw),
                 jnp.where(first, v_sw, 0.0), jnp.where(first, 0.0, v)]
        for s, slab in enumerate(slabs):
            o_ref[:, C_K2 + s * LANES:C_K2 + (s + 1) * LANES] = slab.astype(BF16)

    scale = HEAD_DIM ** -0.5
    kvw = ATTN_KV_HEADS * HEAD_DIM
    in_kv = ATTN_W
    in_rq = in_kv + 2 * kvw

    def backward_states():
        bdm = bdm_ref[...]
        for p in range(PAIRS):
            sl = slice(p * LANES, (p + 1) * LANES)
            carry = carry_ref[p]
            for c in reversed(range(o_ref.shape[0] // CHUNK)):
                rows = slice(c * CHUNK, (c + 1) * CHUNK)
                k = o_ref[rows, C_RK + p * LANES:C_RK + (p + 1) * LANES].astype(F32) * zb_ref[:, sl]
                kv = _dot_tn(k.astype(BF16), o_ref[rows, C_RV + p * LANES:C_RV + (p + 1) * LANES])
                rb_ref[c, p] = carry.astype(BF16)
                carry = gb_ref[p] * carry + kv * bdm
            carry_ref[p] = carry

    def rq_finish(y):
        rope_store(y, C_RQ, 1.0)
        backward_states()

    stages = [
        (0, ATTN_W, lambda y: rope_store(head_norm(y, qnw_ref[...]), C_AQ, scale * LOG2_E)),
        (in_rq + RET_W, RET_W, lambda y: rope_store(y, C_RK, scale)),
        (in_rq + 2 * RET_W, RET_W, lambda y: plain_store(y, C_RV)),
        (in_kv, 2 * kvw, kv_finish),
        (in_rq, RET_W, rq_finish),
        (in_rq + 3 * RET_W, RET_W, lambda y: plain_store(y, C_RG)),
    ]
    pending = proj(stages[0][0], stages[0][1])
    wg_bf_ref[...] = wg_ref[...].astype(BF16)
    wu_bf_ref[...] = wu_ref[...].astype(BF16)
    wd_bf_ref[...] = wd_ref[...].astype(BF16)
    wo_bf_ref[...] = wo_ref[...].astype(BF16)
    for i, (_, _, finish) in enumerate(stages):
        following = proj(stages[i + 1][0], stages[i + 1][1]) if i + 1 < len(stages) else None
        finish(pending)
        pending = following


def _slab_spec(weight, n_steps):
    rows, cols = weight.shape
    n_slabs = n_steps
    while rows % n_slabs or (rows // n_slabs) % BF16_SUBLANES or n_steps % n_slabs:
        n_slabs -= 1
    hold = n_steps // n_slabs
    return pl.BlockSpec((rows // n_slabs, cols), lambda i: (i // hold, 0))


def _proj_call(x2, anw, w_in, qnw, knw, cos, sina, sinb, bd, seq, ffn_weights, zb, gb, bdm):
    tokens, d_model = x2.shape
    tm = PROJ_TM
    n_steps = tokens // tm
    tiles_per_seq = seq // tm
    n_sub = tm // CHUNK
    row = lambda i: (n_steps - 1 - i, 0)
    pos = lambda i: ((n_steps - 1 - i) % tiles_per_seq, 0)
    slabs = [_slab_spec(w, n_steps) for w in ffn_weights]
    state_blk = (n_sub, PAIRS, LANES, LANES)
    return pl.pallas_call(
        functools.partial(_proj_kernel, tiles_per_seq),
        grid=(n_steps,),
        in_specs=[
            pl.BlockSpec((tm, d_model), row),
            _resident((1, d_model)),
            _resident(w_in.shape),
            _resident((1, ATTN_W)),
            _resident((1, LANES)),
            pl.BlockSpec((tm, LANES), pos),
            pl.BlockSpec((tm, LANES), pos),
            pl.BlockSpec((tm, LANES), pos),
            _resident((ATTN_W, ATTN_W)),
            *slabs,
            _resident((CHUNK, RET_W)),
            _resident((PAIRS, LANES, LANES)),
            _resident((LANES, LANES)),
        ],
        out_specs=[pl.BlockSpec((tm, PROJ_W), row), *slabs,
                   pl.BlockSpec(state_blk, lambda i: (n_steps - 1 - i, 0, 0, 0))],
        out_shape=[jax.ShapeDtypeStruct((tokens, PROJ_W), BF16),
                   *[jax.ShapeDtypeStruct(w.shape, BF16) for w in ffn_weights],
                   jax.ShapeDtypeStruct((tokens // CHUNK, PAIRS, LANES, LANES), BF16)],
        scratch_shapes=[pltpu.VMEM((PAIRS, LANES, LANES), F32)],
        compiler_params=pltpu.CompilerParams(
            dimension_semantics=("arbitrary",), vmem_limit_bytes=VMEM_LIMIT),
        name="proj",
    )(x2, anw, w_in, qnw, knw, cos, sina, sinb, bd, *ffn_weights, zb, gb, bdm)


def _mix_kernel(sink_ref, p_ref, prev_ref, next_ref, rb_ref, x_ref, wo_ref, bias_ref, dm_ref,
                xif_ref, xib_ref, rnw_ref, zf_ref, gf_ref, bdm_ref, o_ref, y_ref, carry_ref):
    lane = lax.broadcasted_iota(jnp.int32, (CHUNK, LANES), 1).astype(F32).astype(BF16)
    lo = lane < HEAD_DIM
    hi = jnp.logical_not(lo)
    lo_f32 = lax.broadcasted_iota(jnp.int32, (CHUNK, LANES), 1) < HEAD_DIM
    zero = jnp.zeros((), BF16)

    def keep(mask, a):
        reps = a.shape[0] // CHUNK
        m = mask if reps == 1 else jnp.concatenate([mask] * reps, axis=0)
        return jnp.where(m, a, zero)

    tile = pl.program_id(1)

    @pl.when(tile == 0)
    def _():
        carry_ref[...] = jnp.zeros_like(carry_ref)

    n_sub = p_ref.shape[0] // CHUNK
    group = ATTN_HEADS // ATTN_KV_HEADS
    def kv_operands(j, g):
        def window(c0):
            cols = slice(c0, c0 + LANES)
            if j == 0:
                return jnp.concatenate([prev_ref[:, cols], p_ref[0:2 * CHUNK, cols]], axis=0)
            if j == n_sub - 1:
                return jnp.concatenate(
                    [p_ref[(j - 1) * CHUNK:(j + 1) * CHUNK, cols], next_ref[:, cols]], axis=0)
            return p_ref[(j - 1) * CHUNK:(j + 2) * CHUNK, cols]

        if j == 0:
            bias = bias_ref[jnp.where(tile == 0, 0, 1)]
        elif j == n_sub - 1:
            bias = bias_ref[jnp.where(tile == pl.num_programs(1) - 1, 2, 1)]
        else:
            bias = bias_ref[1]
        return (window(C_K2 + g * LANES), window(C_V4 + 2 * g * LANES),
                window(C_V4 + (2 * g + 1) * LANES), bias)

    def attn_scores(j, pair):
        kg, v_lo, v_hi, bias = kv_operands(j, pair // (PAIRS // ATTN_KV_HEADS))
        qp = p_ref[j * CHUNK:(j + 1) * CHUNK, C_AQ + pair * LANES:C_AQ + (pair + 1) * LANES]
        s_all = _dot_nt(jnp.concatenate([keep(lo, qp), keep(hi, qp)], axis=0), kg)
        return s_all, v_lo, v_hi, bias

    def attn_finish(j, pair, s_all, v_lo, v_hi, bias):
        outs = []
        for i, v_half in enumerate((v_lo, v_hi)):
            s = s_all[i * CHUNK:(i + 1) * CHUNK]
            s = jnp.concatenate([s[:, :CHUNK] + bias[:, :CHUNK], s[:, CHUNK:2 * CHUNK],
                                 s[:, 2 * CHUNK:] + bias[:, 2 * CHUNK:]], axis=1)
            sink = sink_ref[2 * pair + i] * LOG2_E
            m = jnp.maximum(jnp.max(s, axis=-1, keepdims=True), sink)
            e = jnp.exp2(s - m)
            den = jnp.sum(e, axis=-1, keepdims=True) + jnp.exp2(sink - m)
            outs.append(_dot(e.astype(BF16), v_half) * (1.0 / den))
        y_ref[j * CHUNK:(j + 1) * CHUNK, pair * LANES:(pair + 1) * LANES] = (
            outs[0] + outs[1]).astype(BF16)

    def retention_pair(j, p):
        rows = slice(j * CHUNK, (j + 1) * CHUNK)
        sl = slice(p * LANES, (p + 1) * LANES)
        col = lambda c0: slice(c0 + p * LANES, c0 + (p + 1) * LANES)
        qp, kp, vp = p_ref[rows, col(C_RQ)], p_ref[rows, col(C_RK)], p_ref[rows, col(C_RV)]
        s2 = _dot_nt(jnp.concatenate([keep(lo, qp), keep(hi, qp)], axis=0), kp)
        p0 = (s2[:CHUNK] * dm_ref[2 * p]).astype(BF16)
        p1 = (s2[CHUNK:] * dm_ref[2 * p + 1]).astype(BF16)
        qf = (qp.astype(F32) * xif_ref[:, sl]).astype(BF16)
        qb = (qp.astype(F32) * xib_ref[:, sl]).astype(BF16)
        lhs = jnp.concatenate([p0, p1, qf, qb], axis=1)
        state_f = carry_ref[p]
        rhs = jnp.concatenate(
            [keep(lo, vp), keep(hi, vp), state_f.astype(BF16), rb_ref[j, p]], axis=0)
        ret = _dot(lhs, rhs)
        kz = (kp.astype(F32) * zf_ref[:, sl]).astype(BF16)
        carry_ref[p] = gf_ref[p] * state_f + _dot_tn(kz, vp) * bdm_ref[...]
        sq = ret * ret
        ss_lo = jnp.sum(jnp.where(lo_f32, sq, 0.0), axis=-1, keepdims=True)
        ss_hi = jnp.sum(jnp.where(lo_f32, 0.0, sq), axis=-1, keepdims=True)
        ms = jnp.where(lo_f32, ss_lo, ss_hi) * (1.0 / HEAD_DIM)
        ret_n = ret * lax.rsqrt(ms + EPS) * rnw_ref[:, sl]
        gate = p_ref[rows, col(C_RG)].astype(F32)
        y_ref[rows, ATTN_W + p * LANES:ATTN_W + (p + 1) * LANES] = (
            gate * (1.0 / (1.0 + jnp.exp(-gate))) * ret_n).astype(BF16)

    units = [(j, pair) for j in range(n_sub) for pair in range(ATTN_HEADS // 2)]
    ret_items = [(j, p) for j in range(n_sub) for p in range(PAIRS)]
    assert len(units) * RET_PAIRS_PER_UNIT >= len(ret_items)
    pending = attn_scores(*units[0])
    for u, (j, g) in enumerate(units):
        following = attn_scores(*units[u + 1]) if u + 1 < len(units) else None
        for item in ret_items[u * RET_PAIRS_PER_UNIT:(u + 1) * RET_PAIRS_PER_UNIT]:
            retention_pair(*item)
        attn_finish(j, g, *pending)
        pending = following
    o_ref[...] = x_ref[...] + _dot(y_ref[...], wo_ref[...])


def _mix_call(sink, proj3, rb, x3, w_out, bias, dmat, xif, xib, rnw, zf, gf, bdm):
    batch, seq, d_model = x3.shape
    tm = MIX_TM
    n_sub = tm // CHUNK
    nt = seq // tm
    nc = seq // CHUNK
    assert C_K2 == 0 and C_V4 == KV2_W
    kv_prev = pl.BlockSpec(
        (None, CHUNK, KV_W), lambda b, t: (b, jnp.maximum(t * n_sub - 1, 0), 0))
    kv_next = pl.BlockSpec(
        (None, CHUNK, KV_W), lambda b, t: (b, jnp.minimum((t + 1) * n_sub, nc - 1), 0))
    st = pl.BlockSpec((None, n_sub, PAIRS, LANES, LANES), lambda b, t: (b, t, 0, 0, 0))
    return pl.pallas_call(
        _mix_kernel,
        grid=(batch, nt),
        in_specs=[
            pl.BlockSpec(memory_space=pltpu.SMEM),
            pl.BlockSpec((None, tm, PROJ_W), lambda b, t: (b, t, 0)),
            kv_prev, kv_next,
            st,
            pl.BlockSpec((None, tm, d_model), lambda b, t: (b, t, 0)),
            _resident((d_model, d_model)),
            _resident((3, CHUNK, 3 * CHUNK)),
            _resident((RET_HEADS, CHUNK, CHUNK)),
            _resident((CHUNK, RET_W)),
            _resident((CHUNK, RET_W)),
            _resident((1, RET_W)),
            _resident((CHUNK, RET_W)),
            _resident((PAIRS, LANES, LANES)),
            _resident((LANES, LANES)),
        ],
        out_specs=pl.BlockSpec((None, tm, d_model), lambda b, t: (b, t, 0)),
        out_shape=jax.ShapeDtypeStruct((batch, seq, d_model), F32),
        scratch_shapes=[pltpu.VMEM((tm, d_model), BF16), pltpu.VMEM((PAIRS, LANES, LANES), F32)],
        compiler_params=pltpu.CompilerParams(
            dimension_semantics=("arbitrary", "arbitrary"), vmem_limit_bytes=VMEM_LIMIT),
        name="mix",
    )(sink, proj3, proj3, proj3, rb, x3, w_out, bias, dmat, xif, xib, rnw, zf, gf, bdm)


def _ffn_kernel(h_ref, fnw_ref, wg_ref, wu_ref, wd_ref, o_ref, a_ref):
    h = h_ref[...]
    ms = jnp.mean(h * h, axis=-1, keepdims=True)
    m = (h * lax.rsqrt(ms + EPS) * fnw_ref[...]).astype(BF16)
    d_ff = wg_ref.shape[1]
    for c0 in range(0, d_ff, FFN_CK):
        c1 = min(c0 + FFN_CK, d_ff)
        g = _dot(m, wg_ref[:, c0:c1])
        u = _dot(m, wu_ref[:, c0:c1])
        a_ref[:, c0:c1] = (g * (1.0 / (1.0 + jnp.exp(-g))) * u).astype(BF16)
    o_ref[...] = h + _dot(a_ref[...], wd_ref[...])


def _ffn_call(h2, fnw, wg, wu, wd):
    tokens, d_model = h2.shape
    d_ff = wg.shape[1]
    tm = FFN_TM
    row = lambda i: (i, 0)
    return pl.pallas_call(
        _ffn_kernel,
        grid=(tokens // tm,),
        in_specs=[
            pl.BlockSpec((tm, d_model), row),
            _resident((1, d_model)),
            _resident((d_model, d_ff)),
            _resident((d_model, d_ff)),
            _resident((d_ff, d_model)),
        ],
        out_specs=pl.BlockSpec((tm, d_model), row),
        out_shape=jax.ShapeDtypeStruct((tokens, d_model), F32),
        scratch_shapes=[pltpu.VMEM((tm, d_ff), BF16)],
        compiler_params=pltpu.CompilerParams(
            dimension_semantics=("arbitrary",), vmem_limit_bytes=VMEM_LIMIT),
        name="ffn",
    )(h2, fnw, wg, wu, wd)


def _rope_tables(seq):
    inv_freq = ROPE_THETA ** (-np.arange(0, HEAD_DIM, 2, dtype=np.float64) / HEAD_DIM)
    ang = np.arange(seq, dtype=np.float64)[:, None] * inv_freq[None, :]
    cos, sin = np.cos(ang), np.sin(ang)
    zeros = np.zeros_like(sin)
    reps = LANES // HEAD_DIM
    cos_t = np.tile(np.concatenate([cos, cos], -1), (1, reps))
    sina_t = np.tile(np.concatenate([-sin, zeros], -1), (1, reps))
    sinb_t = np.tile(np.concatenate([zeros, sin], -1), (1, reps))
    return tuple(jnp.asarray(t.astype(np.float32)) for t in (cos_t, sina_t, sinb_t))


def _attn_bias():
    i = np.arange(CHUNK)[:, None]
    j = np.arange(3 * CHUNK)[None, :]
    band = np.abs(i + CHUNK - j) <= CHUNK
    first = band & (j >= CHUNK)
    last = band & (j < 2 * CHUNK)
    tab = np.stack([first, band, last])
    return jnp.asarray(np.where(tab, 0.0, NEG_INF).astype(np.float32))


def _retention_tables(log_f, log_b):
    idx = jnp.arange(CHUNK, dtype=F32)
    diff = idx[:, None] - idx[None, :]
    lf, lb = log_f[:, None, None], log_b[:, None, None]
    dmat = jnp.where(diff[None] >= 0,
                     jnp.exp(lf * jnp.maximum(diff, 0.0)[None]),
                     jnp.exp(lb * jnp.maximum(-diff, 0.0)[None]))
    pos = np.arange(CHUNK, dtype=np.float32)[:, None]
    lane_f = jnp.repeat(log_f, HEAD_DIM)[None, :]
    lane_b = jnp.repeat(log_b, HEAD_DIM)[None, :]
    xif = jnp.exp(lane_f * (pos + 1.0))
    xib = jnp.exp(lane_b * (CHUNK - pos))
    zf = jnp.exp(lane_f * (CHUNK - 1.0 - pos))
    zb = jnp.exp(lane_b * pos)
    per_row = lambda lg: jnp.broadcast_to(
        jnp.exp(jnp.repeat(lg.reshape(PAIRS, 2), HEAD_DIM, axis=1) * CHUNK)[:, :, None],
        (PAIRS, LANES, LANES))
    return dmat, xif, xib, zf, zb, per_row(log_f), per_row(log_b)


def _block_diag_ones(width, dtype):
    r = np.arange(width) // HEAD_DIM
    return jnp.asarray((r[:, None] == r[None, :]).astype(np.float32), dtype=dtype)


def kernel(x, attn_norm_w, w_in, q_norm_w, k_norm_w, attn_sink, ret_log_decay_fwd,
           ret_log_decay_bwd, ret_norm_w, w_out, ffn_norm_w, w_gate, w_up, w_down):
    batch, seq, d_model = x.shape
    depth = w_in.shape[0]
    assert seq % PROJ_TM == 0 and seq % MIX_TM == 0 and (batch * seq) % FFN_TM == 0
    assert MIX_TM // CHUNK >= 2

    cos_t, sina_t, sinb_t = _rope_tables(seq)
    bias = _attn_bias()
    bd = _block_diag_ones(RET_W, BF16)
    bdm = _block_diag_ones(LANES, F32)

    h = x
    for l in range(depth):
        qnw = jnp.tile(q_norm_w[l], ATTN_HEADS)[None, :]
        knw = jnp.tile(k_norm_w[l], ATTN_KV_HEADS)[None, :]
        log_f = -jnp.abs(ret_log_decay_fwd[l].astype(F32))
        log_b = -jnp.abs(ret_log_decay_bwd[l].astype(F32))
        dmat, xif, xib, zf, zb, gf, gb = _retention_tables(log_f, log_b)

        proj, wg_bf, wu_bf, wd_bf, wo_bf, rb = _proj_call(
            h.reshape(batch * seq, d_model), attn_norm_w[l][None, :], w_in[l], qnw, knw,
            cos_t, sina_t, sinb_t, bd, seq, (w_gate[l], w_up[l], w_down[l], w_out[l]), zb, gb, bdm)
        proj3 = proj.reshape(batch, seq, PROJ_W)
        rb = rb.reshape(batch, seq // CHUNK, PAIRS, LANES, LANES)
        h = _mix_call(attn_sink[l].astype(F32), proj3, rb, h, wo_bf,
                      bias, dmat, xif, xib, ret_norm_w[l][None, :], zf, gf, bdm)
        h = _ffn_call(h.reshape(batch * seq, d_model), ffn_norm_w[l][None, :],
                      wg_bf, wu_bf, wd_bf).reshape(batch, seq, d_model)
    return h
```

```python
import functools

import jax
import jax.numpy as jnp
import numpy as np
from jax import lax
from jax.experimental import pallas as pl
from jax.experimental.pallas import tpu as pltpu

HEAD_DIM = 64
ATTN_HEADS = 8
ATTN_KV_HEADS = 2
RET_HEADS = 8
CHUNK = 128
ROPE_THETA = 10000.0
EPS = 1e-6
NEG_INF = -1e30
LOG2_E = 1.4426950408889634

LANES = 128
BF16_SUBLANES = 16
PAIRS = RET_HEADS // 2
ATTN_W = ATTN_HEADS * HEAD_DIM
RET_W = RET_HEADS * HEAD_DIM
KV2_W = 2 * ATTN_KV_HEADS * HEAD_DIM
V4_W = 2 * KV2_W
KV_W = KV2_W + V4_W

C_K2 = 0
C_V4 = C_K2 + KV2_W
C_AQ = C_V4 + V4_W
C_RK = C_AQ + ATTN_W
C_RV = C_RK + RET_W
C_RQ = C_RV + RET_W
C_RG = C_RQ + RET_W
PROJ_W = C_RG + RET_W

PROJ_TM = 1024
MIX_TM = 1024
RET_PAIRS_PER_UNIT = 1
FFN_TM = 1024
FFN_CK = 256
VMEM_LIMIT = 56 * 1024 * 1024

BF16 = jnp.bfloat16
F32 = jnp.float32


def _dot(a, b):
    return jnp.dot(a, b, preferred_element_type=F32)


def _dot_nt(a, b):
    return lax.dot_general(a, b, (((1,), (1,)), ((), ())), preferred_element_type=F32)


def _dot_tn(a, b):
    return lax.dot_general(a, b, (((0,), (0,)), ((), ())), preferred_element_type=F32)


def _resident(shape):
    zeros = (0,) * len(shape)
    return pl.BlockSpec(shape, lambda *_: zeros, pipeline_mode=pl.Buffered(1))


def _proj_kernel(tiles_per_seq, x_ref, anw_ref, w_ref, qnw_ref, knw_ref, cos_ref, sina_ref,
                 sinb_ref, bd_ref, wg_ref, wu_ref, wd_ref, wo_ref, zb_ref, gb_ref, bdm_ref,
                 o_ref, wg_bf_ref, wu_bf_ref, wd_bf_ref, wo_bf_ref, rb_ref, carry_ref):
    @pl.when(pl.program_id(0) % tiles_per_seq == 0)
    def _():
        carry_ref[...] = jnp.zeros_like(carry_ref)

    x = x_ref[...]
    ms = jnp.mean(x * x, axis=-1, keepdims=True)
    n = x * lax.rsqrt(ms + EPS) * anw_ref[...]
    cos = cos_ref[...]
    sina = sina_ref[...]
    sinb = sinb_ref[...]

    def proj(c0, width):
        return _dot(n, w_ref[:, c0:c0 + width])

    def head_norm(y, w):
        width = y.shape[-1]
        ss = _dot((y * y).astype(BF16), bd_ref[:width, :width])
        return y * lax.rsqrt(ss * (1.0 / HEAD_DIM) + EPS) * w

    def rope(ys):
        return ys * cos + pltpu.roll(ys, LANES - 32, 1) * sina + pltpu.roll(ys, 32, 1) * sinb

    def rope_store(y, c0, scale):
        for s in range(y.shape[-1] // LANES):
            r = rope(y[:, s * LANES:(s + 1) * LANES])
            if scale != 1.0:
                r = r * scale
            o_ref[:, c0 + s * LANES:c0 + (s + 1) * LANES] = r.astype(BF16)

    def plain_store(y, c0):
        o_ref[:, c0:c0 + y.shape[-1]] = y.astype(BF16)

    def kv_finish(y):
        first = lax.broadcasted_iota(jnp.int32, (y.shape[0], LANES), 1) < HEAD_DIM
        k = rope(head_norm(y[:, :LANES], knw_ref[...]))
        k_sw = pltpu.roll(k, HEAD_DIM, 1)
        v = y[:, LANES:]
        v_sw = pltpu.roll(v, HEAD_DIM, 1)
        slabs = [jnp.where(first, k, k_sw), jnp.where(first, k_sw, k),
                 jnp.where(first, v, 0.0), jnp.where(first, 0.0, v_sw),
                 jnp.where(first, v_sw, 0.0), jnp.where(first, 0.0, v)]
        for s, slab in enumerate(slabs):
            o_ref[:, C_K2 + s * LANES:C_K2 + (s + 1) * LANES] = slab.astype(BF16)

    scale = HEAD_DIM ** -0.5
    kvw = ATTN_KV_HEADS * HEAD_DIM
    in_kv = ATTN_W
    in_rq = in_kv + 2 * kvw

    def backward_states():
        bdm = bdm_ref[...]
        for p in range(PAIRS):
            sl = slice(p * LANES, (p + 1) * LANES)
            carry = carry_ref[p]
            for c in reversed(range(o_ref.shape[0] // CHUNK)):
                rows = slice(c * CHUNK, (c + 1) * CHUNK)
                k = o_ref[rows, C_RK + p * LANES:C_RK + (p + 1) * LANES].astype(F32) * zb_ref[:, sl]
                kv = _dot_tn(k.astype(BF16), o_ref[rows, C_RV + p * LANES:C_RV + (p + 1) * LANES])
                rb_ref[c, p] = carry.astype(BF16)
                carry = gb_ref[p] * carry + kv * bdm
            carry_ref[p] = carry

    def rq_finish(y):
        rope_store(y, C_RQ, 1.0)
        backward_states()

    stages = [
        (0, ATTN_W, lambda y: rope_store(head_norm(y, qnw_ref[...]), C_AQ, scale * LOG2_E)),
        (in_rq + RET_W, RET_W, lambda y: rope_store(y, C_RK, scale)),
        (in_rq + 2 * RET_W, RET_W, lambda y: plain_store(y, C_RV)),
        (in_kv, 2 * kvw, kv_finish),
        (in_rq, RET_W, rq_finish),
        (in_rq + 3 * RET_W, RET_W, lambda y: plain_store(y, C_RG)),
    ]
    pending = proj(stages[0][0], stages[0][1])
    wg_bf_ref[...] = wg_ref[...].astype(BF16)
    wu_bf_ref[...] = wu_ref[...].astype(BF16)
    wd_bf_ref[...] = wd_ref[...].astype(BF16)
    wo_bf_ref[...] = wo_ref[...].astype(BF16)
    for i, (_, _, finish) in enumerate(stages):
        following = proj(stages[i + 1][0], stages[i + 1][1]) if i + 1 < len(stages) else None
        finish(pending)
        pending = following


def _slab_spec(weight, n_steps):
    rows, cols = weight.shape
    n_slabs = n_steps
    while rows % n_slabs or (rows // n_slabs) % BF16_SUBLANES or n_steps % n_slabs:
        n_slabs -= 1
    hold = n_steps // n_slabs
    return pl.BlockSpec((rows // n_slabs, cols), lambda i: (i // hold, 0))


def _proj_call(x2, anw, w_in, qnw, knw, cos, sina, sinb, bd, seq, ffn_weights, zb, gb, bdm):
    tokens, d_model = x2.shape
    tm = PROJ_TM
    n_steps = tokens // tm
    tiles_per_seq = seq // tm
    n_sub = tm // CHUNK
    row = lambda i: (n_steps - 1 - i, 0)
    pos = lambda i: ((n_steps - 1 - i) % tiles_per_seq, 0)
    slabs = [_slab_spec(w, n_steps) for w in ffn_weights]
    state_blk = (n_sub, PAIRS, LANES, LANES)
    return pl.pallas_call(
        functools.partial(_proj_kernel, tiles_per_seq),
        grid=(n_steps,),
        in_specs=[
            pl.BlockSpec((tm, d_model), row),
            _resident((1, d_model)),
            _resident(w_in.shape),
            _resident((1, ATTN_W)),
            _resident((1, LANES)),
            pl.BlockSpec((tm, LANES), pos),
            pl.BlockSpec((tm, LANES), pos),
            pl.BlockSpec((tm, LANES), pos),
            _resident((ATTN_W, ATTN_W)),
            *slabs,
            _resident((CHUNK, RET_W)),
            _resident((PAIRS, LANES, LANES)),
            _resident((LANES, LANES)),
        ],
        out_specs=[pl.BlockSpec((tm, PROJ_W), row), *slabs,
                   pl.BlockSpec(state_blk, lambda i: (n_steps - 1 - i, 0, 0, 0))],
        out_shape=[jax.ShapeDtypeStruct((tokens, PROJ_W), BF16),
                   *[jax.ShapeDtypeStruct(w.shape, BF16) for w in ffn_weights],
                   jax.ShapeDtypeStruct((tokens // CHUNK, PAIRS, LANES, LANES), BF16)],
        scratch_shapes=[pltpu.VMEM((PAIRS, LANES, LANES), F32)],
        compiler_params=pltpu.CompilerParams(
            dimension_semantics=("arbitrary",), vmem_limit_bytes=VMEM_LIMIT),
        name="proj",
    )(x2, anw, w_in, qnw, knw, cos, sina, sinb, bd, *ffn_weights, zb, gb, bdm)


def _mix_kernel(sink_ref, p_ref, prev_ref, next_ref, rb_ref, x_ref, wo_ref, bias_ref, dm_ref,
                xif_ref, xib_ref, rnw_ref, zf_ref, gf_ref, bdm_ref, o_ref, y_ref, carry_ref):
    lane = lax.broadcasted_iota(jnp.int32, (CHUNK, LANES), 1).astype(F32).astype(BF16)
    lo = lane < HEAD_DIM
    hi = jnp.logical_not(lo)
    lo_f32 = lax.broadcasted_iota(jnp.int32, (CHUNK, LANES), 1) < HEAD_DIM
    zero = jnp.zeros((), BF16)

    def keep(mask, a):
        reps = a.shape[0] // CHUNK
        m = mask if reps == 1 else jnp.concatenate([mask] * reps, axis=0)
        return jnp.where(m, a, zero)

    tile = pl.program_id(1)

    @pl.when(tile == 0)
    def _():
        carry_ref[...] = jnp.zeros_like(carry_ref)

    n_sub = p_ref.shape[0] // CHUNK
    group = ATTN_HEADS // ATTN_KV_HEADS
    def kv_operands(j, g):
        def window(c0):
            cols = slice(c0, c0 + LANES)
            if j == 0:
                return jnp.concatenate([prev_ref[:, cols], p_ref[0:2 * CHUNK, cols]], axis=0)
            if j == n_sub - 1:
                return jnp.concatenate(
                    [p_ref[(j - 1) * CHUNK:(j + 1) * CHUNK, cols], next_ref[:, cols]], axis=0)
            return p_ref[(j - 1) * CHUNK:(j + 2) * CHUNK, cols]

        if j == 0:
            bias = bias_ref[jnp.where(tile == 0, 0, 1)]
        elif j == n_sub - 1:
            bias = bias_ref[jnp.where(tile == pl.num_programs(1) - 1, 2, 1)]
        else:
            bias = bias_ref[1]
        return (window(C_K2 + g * LANES), window(C_V4 + 2 * g * LANES),
                window(C_V4 + (2 * g + 1) * LANES), bias)

    def attn_scores(j, pair):
        kg, v_lo, v_hi, bias = kv_operands(j, pair // (PAIRS // ATTN_KV_HEADS))
        qp = p_ref[j * CHUNK:(j + 1) * CHUNK, C_AQ + pair * LANES:C_AQ + (pair + 1) * LANES]
        s_all = _dot_nt(jnp.concatenate([keep(lo, qp), keep(hi, qp)], axis=0), kg)
        return s_all, v_lo, v_hi, bias

    def attn_finish(j, pair, s_all, v_lo, v_hi, bias):
        outs = []
        for i, v_half in enumerate((v_lo, v_hi)):
            s = s_all[i * CHUNK:(i + 1) * CHUNK]
            s = jnp.concatenate([s[:, :CHUNK] + bias[:, :CHUNK], s[:, CHUNK:2 * CHUNK],
                                 s[:, 2 * CHUNK:] + bias[:, 2 * CHUNK:]], axis=1)
            sink = sink_ref[2 * pair + i] * LOG2_E
            m = jnp.maximum(jnp.max(s, axis=-1, keepdims=True), sink)
            e = jnp.exp2(s - m)
            den = jnp.sum(e, axis=-1, keepdims=True) + jnp.exp2(sink - m)
            outs.append(_dot(e.astype(BF16), v_half) * (1.0 / den))
        y_ref[j * CHUNK:(j + 1) * CHUNK, pair * LANES:(pair + 1) * LANES] = (
            outs[0] + outs[1]).astype(BF16)

    def retention_pair(j, p):
        rows = slice(j * CHUNK, (j + 1) * CHUNK)
        sl = slice(p * LANES, (p + 1) * LANES)
        col = lambda c0: slice(c0 + p * LANES, c0 + (p + 1) * LANES)
        qp, kp, vp = p_ref[rows, col(C_RQ)], p_ref[rows, col(C_RK)], p_ref[rows, col(C_RV)]
        s2 = _dot_nt(jnp.concatenate([keep(lo, qp), keep(hi, qp)], axis=0), kp)
        p0 = (s2[:CHUNK] * dm_ref[2 * p]).astype(BF16)
        p1 = (s2[CHUNK:] * dm_ref[2 * p + 1]).astype(BF16)
        qf = (qp.astype(F32) * xif_ref[:, sl]).astype(BF16)
        qb = (qp.astype(F32) * xib_ref[:, sl]).astype(BF16)
        lhs = jnp.concatenate([p0, p1, qf, qb], axis=1)
        state_f = carry_ref[p]
        rhs = jnp.concatenate(
            [keep(lo, vp), keep(hi, vp), state_f.astype(BF16), rb_ref[j, p]], axis=0)
        ret = _dot(lhs, rhs)
        kz = (kp.astype(F32) * zf_ref[:, sl]).astype(BF16)
        carry_ref[p] = gf_ref[p] * state_f + _dot_tn(kz, vp) * bdm_ref[...]
        sq = ret * ret
        ss_lo = jnp.sum(jnp.where(lo_f32, sq, 0.0), axis=-1, keepdims=True)
        ss_hi = jnp.sum(jnp.where(lo_f32, 0.0, sq), axis=-1, keepdims=True)
        ms = jnp.where(lo_f32, ss_lo, ss_hi) * (1.0 / HEAD_DIM)
        ret_n = ret * lax.rsqrt(ms + EPS) * rnw_ref[:, sl]
        gate = p_ref[rows, col(C_RG)].astype(F32)
        y_ref[rows, ATTN_W + p * LANES:ATTN_W + (p + 1) * LANES] = (
            gate * (1.0 / (1.0 + jnp.exp(-gate))) * ret_n).astype(BF16)

    units = [(j, pair) for j in range(n_sub) for pair in range(ATTN_HEADS // 2)]
    ret_items = [(j, p) for j in range(n_sub) for p in range(PAIRS)]
    assert len(units) * RET_PAIRS_PER_UNIT >= len(ret_items)
    pending = attn_scores(*units[0])
    for u, (j, g) in enumerate(units):
        following = attn_scores(*units[u + 1]) if u + 1 < len(units) else None
        for item in ret_items[u * RET_PAIRS_PER_UNIT:(u + 1) * RET_PAIRS_PER_UNIT]:
            retention_pair(*item)
        attn_finish(j, g, *pending)
        pending = following
    o_ref[...] = x_ref[...] + _dot(y_ref[...], wo_ref[...])


def _mix_call(sink, proj3, rb, x3, w_out, bias, dmat, xif, xib, rnw, zf, gf, bdm):
    batch, seq, d_model = x3.shape
    tm = MIX_TM
    n_sub = tm // CHUNK
    nt = seq // tm
    nc = seq // CHUNK
    assert C_K2 == 0 and C_V4 == KV2_W
    kv_prev = pl.BlockSpec(
        (None, CHUNK, KV_W), lambda b, t: (b, jnp.maximum(t * n_sub - 1, 0), 0))
    kv_next = pl.BlockSpec(
        (None, CHUNK, KV_W), lambda b, t: (b, jnp.minimum((t + 1) * n_sub, nc - 1), 0))
    st = pl.BlockSpec((None, n_sub, PAIRS, LANES, LANES), lambda b, t: (b, t, 0, 0, 0))
    return pl.pallas_call(
        _mix_kernel,
        grid=(batch, nt),
        in_specs=[
            pl.BlockSpec(memory_space=pltpu.SMEM),
            pl.BlockSpec((None, tm, PROJ_W), lambda b, t: (b, t, 0)),
            kv_prev, kv_next,
            st,
            pl.BlockSpec((None, tm, d_model), lambda b, t: (b, t, 0)),
            _resident((d_model, d_model)),
            _resident((3, CHUNK, 3 * CHUNK)),
            _resident((RET_HEADS, CHUNK, CHUNK)),
            _resident((CHUNK, RET_W)),
            _resident((CHUNK, RET_W)),
            _resident((1, RET_W)),
            _resident((CHUNK, RET_W)),
            _resident((PAIRS, LANES, LANES)),
            _resident((LANES, LANES)),
        ],
        out_specs=pl.BlockSpec((None, tm, d_model), lambda b, t: (b, t, 0)),
        out_shape=jax.ShapeDtypeStruct((batch, seq, d_model), F32),
        scratch_shapes=[pltpu.VMEM((tm, d_model), BF16), pltpu.VMEM((PAIRS, LANES, LANES), F32)],
        compiler_params=pltpu.CompilerParams(
            dimension_semantics=("arbitrary", "arbitrary"), vmem_limit_bytes=VMEM_LIMIT),
        name="mix",
    )(sink, proj3, proj3, proj3, rb, x3, w_out, bias, dmat, xif, xib, rnw, zf, gf, bdm)


def _ffn_kernel(h_ref, hn_ref, fnw_ref, wg_ref, wu_ref, wd_ref, o_ref, a_ref, m_ref):
    def normed(h):
        ms = jnp.mean(h * h, axis=-1, keepdims=True)
        return h * lax.rsqrt(ms + EPS) * fnw_ref[...]

    @pl.when(pl.program_id(0) == 0)
    def _():
        m_ref[...] = normed(h_ref[...]).astype(BF16)

    d_ff = wg_ref.shape[1]
    starts = list(range(0, d_ff, FFN_CK))
    m_next = None
    for idx, c0 in enumerate(starts):
        c1 = min(c0 + FFN_CK, d_ff)
        g = _dot(m_ref[...], wg_ref[:, c0:c1])
        u = _dot(m_ref[...], wu_ref[:, c0:c1])
        a_ref[:, c0:c1] = (g * (1.0 / (1.0 + jnp.exp(-g))) * u).astype(BF16)
        if idx == len(starts) // 2:
            y = normed(hn_ref[...])
            m_next = y.astype(BF16)
            col = jnp.sum(y, axis=0, keepdims=True)
            acc = col[:, :LANES]
            for k in range(1, col.shape[1] // LANES):
                acc = acc + col[:, k * LANES:(k + 1) * LANES]
            bits = lax.bitcast_convert_type(acc, jnp.uint32)
            zero = ((bits >> 16) >> 16).astype(F32)
            tile = a_ref[0:BF16_SUBLANES, c0:c0 + LANES].astype(F32) + zero
            a_ref[0:BF16_SUBLANES, c0:c0 + LANES] = tile.astype(BF16)
    out = h_ref[...] + _dot(a_ref[...], wd_ref[...])
    m_ref[...] = m_next
    o_ref[...] = out


def _ffn_call(h2, fnw, wg, wu, wd):
    tokens, d_model = h2.shape
    d_ff = wg.shape[1]
    tm = FFN_TM
    row = lambda i: (i, 0)
    n_tiles = tokens // tm
    return pl.pallas_call(
        _ffn_kernel,
        grid=(n_tiles,),
        in_specs=[
            pl.BlockSpec((tm, d_model), row),
            pl.BlockSpec((tm, d_model), lambda i: (jnp.minimum(i + 1, n_tiles - 1), 0)),
            _resident((1, d_model)),
            _resident((d_model, d_ff)),
            _resident((d_model, d_ff)),
            _resident((d_ff, d_model)),
        ],
        out_specs=pl.BlockSpec((tm, d_model), row),
        out_shape=jax.ShapeDtypeStruct((tokens, d_model), F32),
        scratch_shapes=[pltpu.VMEM((tm, d_ff), BF16), pltpu.VMEM((tm, d_model), BF16)],
        compiler_params=pltpu.CompilerParams(
            dimension_semantics=("arbitrary",), vmem_limit_bytes=VMEM_LIMIT),
        name="ffn",
    )(h2, h2, fnw, wg, wu, wd)


def _rope_tables(seq):
    inv_freq = ROPE_THETA ** (-np.arange(0, HEAD_DIM, 2, dtype=np.float64) / HEAD_DIM)
    ang = np.arange(seq, dtype=np.float64)[:, None] * inv_freq[None, :]
    cos, sin = np.cos(ang), np.sin(ang)
    zeros = np.zeros_like(sin)
    reps = LANES // HEAD_DIM
    cos_t = np.tile(np.concatenate([cos, cos], -1), (1, reps))
    sina_t = np.tile(np.concatenate([-sin, zeros], -1), (1, reps))
    sinb_t = np.tile(np.concatenate([zeros, sin], -1), (1, reps))
    return tuple(jnp.asarray(t.astype(np.float32)) for t in (cos_t, sina_t, sinb_t))


def _attn_bias():
    i = np.arange(CHUNK)[:, None]
    j = np.arange(3 * CHUNK)[None, :]
    band = np.abs(i + CHUNK - j) <= CHUNK
    first = band & (j >= CHUNK)
    last = band & (j < 2 * CHUNK)
    tab = np.stack([first, band, last])
    return jnp.asarray(np.where(tab, 0.0, NEG_INF).astype(np.float32))


def _retention_tables(log_f, log_b):
    idx = jnp.arange(CHUNK, dtype=F32)
    diff = idx[:, None] - idx[None, :]
    lf, lb = log_f[:, None, None], log_b[:, None, None]
    dmat = jnp.where(diff[None] >= 0,
                     jnp.exp(lf * jnp.maximum(diff, 0.0)[None]),
                     jnp.exp(lb * jnp.maximum(-diff, 0.0)[None]))
    pos = np.arange(CHUNK, dtype=np.float32)[:, None]
    lane_f = jnp.repeat(log_f, HEAD_DIM)[None, :]
    lane_b = jnp.repeat(log_b, HEAD_DIM)[None, :]
    xif = jnp.exp(lane_f * (pos + 1.0))
    xib = jnp.exp(lane_b * (CHUNK - pos))
    zf = jnp.exp(lane_f * (CHUNK - 1.0 - pos))
    zb = jnp.exp(lane_b * pos)
    per_row = lambda lg: jnp.broadcast_to(
        jnp.exp(jnp.repeat(lg.reshape(PAIRS, 2), HEAD_DIM, axis=1) * CHUNK)[:, :, None],
        (PAIRS, LANES, LANES))
    return dmat, xif, xib, zf, zb, per_row(log_f), per_row(log_b)


def _block_diag_ones(width, dtype):
    r = np.arange(width) // HEAD_DIM
    return jnp.asarray((r[:, None] == r[None, :]).astype(np.float32), dtype=dtype)


def kernel(x, attn_norm_w, w_in, q_norm_w, k_norm_w, attn_sink, ret_log_decay_fwd,
           ret_log_decay_bwd, ret_norm_w, w_out, ffn_norm_w, w_gate, w_up, w_down):
    batch, seq, d_model = x.shape
    depth = w_in.shape[0]
    assert seq % PROJ_TM == 0 and seq % MIX_TM == 0 and (batch * seq) % FFN_TM == 0
    assert MIX_TM // CHUNK >= 2

    cos_t, sina_t, sinb_t = _rope_tables(seq)
    bias = _attn_bias()
    bd = _block_diag_ones(RET_W, BF16)
    bdm = _block_diag_ones(LANES, F32)

    h = x
    for l in range(depth):
        qnw = jnp.tile(q_norm_w[l], ATTN_HEADS)[None, :]
        knw = jnp.tile(k_norm_w[l], ATTN_KV_HEADS)[None, :]
        log_f = -jnp.abs(ret_log_decay_fwd[l].astype(F32))
        log_b = -jnp.abs(ret_log_decay_bwd[l].astype(F32))
        dmat, xif, xib, zf, zb, gf, gb = _retention_tables(log_f, log_b)

        proj, wg_bf, wu_bf, wd_bf, wo_bf, rb = _proj_call(
            h.reshape(batch * seq, d_model), attn_norm_w[l][None, :], w_in[l], qnw, knw,
            cos_t, sina_t, sinb_t, bd, seq, (w_gate[l], w_up[l], w_down[l], w_out[l]), zb, gb, bdm)
        proj3 = proj.reshape(batch, seq, PROJ_W)
        rb = rb.reshape(batch, seq // CHUNK, PAIRS, LANES, LANES)
        h = _mix_call(attn_sink[l].astype(F32), proj3, rb, h, wo_bf,
                      bias, dmat, xif, xib, ret_norm_w[l][None, :], zf, gf, bdm)
        h = _ffn_call(h.reshape(batch * seq, d_model), ffn_norm_w[l][None, :],
                      wg_bf, wu_bf, wd_bf).reshape(batch, seq, d_model)
    return h
```
